```python
import math
import jax
import jax.numpy as jnp
from jax import lax
import numpy as np

D_MODEL = 1024
BATCH = 4
SEQ = 8192
DEPTH = 1

GRID_W = 64
CTX_LEN = 256
QBLOCK = 128
ROPE_THETA = 10000.0
EPS = 1e-6

MLA_HEADS = 8
MLA_Q_RANK = 256
MLA_KV_RANK = 128
MLA_NOPE = 64
MLA_ROPE = 32
MLA_V = 64
DIFF_HEADS = 4
DIFF_HD = 64
N_EXPERTS = 16
EXPERT_FF = 1024
EC_CAPACITY = 2

MLA_Q_OFF = 0
MLA_KV_OFF = MLA_Q_OFF + MLA_Q_RANK
MLA_KR_OFF = MLA_KV_OFF + MLA_KV_RANK
DIFF_Q_OFF = MLA_KR_OFF + MLA_ROPE
DIFF_K_OFF = DIFF_Q_OFF + DIFF_HEADS * 2 * DIFF_HD
DIFF_V_OFF = DIFF_K_OFF + DIFF_HEADS * 2 * DIFF_HD
GATE_OFF = DIFF_V_OFF + DIFF_HEADS * 2 * DIFF_HD
N_IN = GATE_OFF + 2 * D_MODEL
MLA_WIDTH = MLA_HEADS * MLA_V
DIFF_WIDTH = DIFF_HEADS * 2 * DIFF_HD
MLA_SCALE = (MLA_NOPE + MLA_ROPE) ** -0.5
DIFF_SCALE = DIFF_HD ** -0.5
DEEPNORM_ALPHA = (2 * DEPTH) ** 0.25
DEEPNORM_BETA = (8 * DEPTH) ** -0.25

kernel_name = 'hybrid_mla_diffattn_ecmoe_dit_block'


def layer_norm(x, g, b):
    xf = x.astype(jnp.float32)
    mu = jnp.mean(xf, axis=-1, keepdims=True)
    var = jnp.mean(jnp.square(xf - mu), axis=-1, keepdims=True)
    y = (xf - mu) * lax.rsqrt(var + EPS) * g.astype(jnp.float32) + b.astype(jnp.float32)
    return y.astype(x.dtype)


def rms_norm(x, g):
    xf = x.astype(jnp.float32)
    y = xf * lax.rsqrt(jnp.mean(xf * xf, axis=-1, keepdims=True) + EPS) * g.astype(jnp.float32)
    return y.astype(x.dtype)


def rope_1d(x, pos):
    d = x.shape[-1]
    half = d // 2
    inv = ROPE_THETA ** (-(jnp.arange(half, dtype=jnp.float32) * 2.0 / d))
    ang = pos.astype(jnp.float32)[:, None] * inv
    cos, sin = jnp.cos(ang), jnp.sin(ang)
    xf = x.astype(jnp.float32)
    x1, x2 = xf[..., :half], xf[..., half:]
    return jnp.concatenate([x1 * cos - x2 * sin, x2 * cos + x1 * sin], axis=-1).astype(x.dtype)


def axial_rope(x, row, col):
    h = x.shape[-1] // 2
    return jnp.concatenate([rope_1d(x[..., :h], row), rope_1d(x[..., h:], col)], axis=-1)


def split_heads(t, n_heads):
    b, s, _ = t.shape
    return t.reshape(b, s, n_heads, -1).transpose(0, 2, 1, 3)


def merge_heads(t):
    b, h, s, d = t.shape
    return t.transpose(0, 2, 1, 3).reshape(b, s, h * d)


def over_query_blocks(fn, q):
    b, h, sq = q.shape[:3]
    rest = q.shape[3:]
    nb = sq // QBLOCK
    qb = jnp.moveaxis(q.reshape(b, h, nb, QBLOCK, *rest), 2, 0)
    out = lax.map(fn, qb)
    out = jnp.moveaxis(out, 0, 2)
    return out.reshape(b, h, sq, out.shape[-1])


def softmax_attention(q, k, v, scale):
    def one(qb):
        s = jnp.einsum('bhqd,bhkd->bhqk', qb, k).astype(jnp.float32) * scale
        p = jax.nn.softmax(s, axis=-1)
        return jnp.einsum('bhqk,bhkd->bhqd', p.astype(v.dtype), v)
    return over_query_blocks(one, q)


def diff_attention(q, k, v, lam, scale):
    def one(qb):
        s = jnp.einsum('bhqmd,bhkmd->bhmqk', qb, k).astype(jnp.float32) * scale
        p = jax.nn.softmax(s, axis=-1)
        w = p[:, :, 0] - lam * p[:, :, 1]
        return jnp.einsum('bhqk,bhkd->bhqd', w.astype(v.dtype), v)
    return over_query_blocks(one, q)


def mixer_projections(h, w_in, g_q, w_uq, g_kv, w_ukv, pos):
    b, s, _ = h.shape
    p = h @ w_in
    c_q = rms_norm(p[..., MLA_Q_OFF:MLA_KV_OFF], g_q)
    q = split_heads(c_q @ w_uq, MLA_HEADS)
    c_kv = rms_norm(p[..., MLA_KV_OFF:MLA_KR_OFF], g_kv)
    kv = split_heads(c_kv @ w_ukv, MLA_HEADS)
    q_nope, q_rope = q[..., :MLA_NOPE], q[..., MLA_NOPE:]
    k_nope, v_mla = kv[..., :MLA_NOPE], kv[..., MLA_NOPE:]
    k_rope = p[..., MLA_KR_OFF:DIFF_Q_OFF][:, None]
    dq = p[..., DIFF_Q_OFF:DIFF_K_OFF].reshape(b, s, DIFF_HEADS, 2, DIFF_HD).transpose(0, 2, 3, 1, 4)
    dk = p[..., DIFF_K_OFF:DIFF_V_OFF].reshape(b, s, DIFF_HEADS, 2, DIFF_HD).transpose(0, 2, 3, 1, 4)
    dv = split_heads(p[..., DIFF_V_OFF:GATE_OFF], DIFF_HEADS)
    if pos is not None:
        row, col = pos
        q_rope = axial_rope(q_rope, row, col)
        k_rope = axial_rope(k_rope, row, col)
        dq = axial_rope(dq, row, col)
        dk = axial_rope(dk, row, col)
    q_mla = jnp.concatenate([q_nope, q_rope], axis=-1)
    k_mla = jnp.concatenate([k_nope, jnp.broadcast_to(k_rope, (b, MLA_HEADS, s, MLA_ROPE))], axis=-1)
    dq = dq.transpose(0, 1, 3, 2, 4)
    dk = dk.transpose(0, 1, 3, 2, 4)
    gates = jax.nn.sigmoid(p[..., GATE_OFF:].astype(jnp.float32)).astype(h.dtype)
    return (q_mla, k_mla, v_mla, dq, dk, dv, gates)


def mixer_output(q_mla, dq, gates, k_mla, v_mla, dk, dv, lam, lambda_init, g_subln, w_o_mla, w_o_diff, w_out):
    o_mla = merge_heads(softmax_attention(q_mla, k_mla, v_mla, MLA_SCALE))
    o_diff = rms_norm(diff_attention(dq, dk, dv, lam, DIFF_SCALE), g_subln) * (1.0 - lambda_init)
    o_diff = merge_heads(o_diff)
    y = gates[..., :D_MODEL] * (o_mla @ w_o_mla) + gates[..., D_MODEL:] * (o_diff @ w_o_diff)
    return y @ w_out


def ec_moe(h, w_router, b_router, w1, w3, w2):
    b, n, _ = h.shape
    cap = EC_CAPACITY * n // N_EXPERTS
    aff = jax.nn.softmax((h @ w_router + b_router).astype(jnp.float32), axis=-1)
    g, idx = lax.top_k(jnp.swapaxes(aff, 1, 2), cap)
    bidx = jnp.arange(b)[:, None, None]
    xe = h[bidx, idx]
    hid = jax.nn.silu(jnp.einsum('becd,edf->becf', xe, w1)) * jnp.einsum('becd,edf->becf', xe, w3)
    ye = jnp.einsum('becf,efd->becd', hid, w2) * g[..., None].astype(h.dtype)
    return jnp.zeros_like(h).at[bidx, idx].add(ye)


def setup_inputs(seed: int = 0) -> dict:
    key = jax.random.key(seed)
    ks = jax.random.split(key, 32)
    L, D = DEPTH, D_MODEL

    def nrm(k, shape, scale):
        return jax.random.normal(k, shape, jnp.float32) * scale

    return {
        'x': nrm(ks[0], (BATCH, SEQ, D), 1.0),
        'c': nrm(ks[1], (BATCH, D), 1.0),
        'ctx': nrm(ks[2], (BATCH, CTX_LEN, D), 1.0),
        'c_ctx': nrm(ks[3], (D,), 1.0),
        'w_ada': nrm(ks[4], (L, D, 6 * D), 0.5 * D ** -0.5),
        'b_ada': nrm(ks[5], (L, 6 * D), 0.02),
        'w_in': nrm(ks[6], (L, D, N_IN), D ** -0.5),
        'mla_g_q': 1.0 + nrm(ks[7], (L, MLA_Q_RANK), 0.02),
        'mla_w_uq': nrm(ks[8], (L, MLA_Q_RANK, MLA_HEADS * (MLA_NOPE + MLA_ROPE)), MLA_Q_RANK ** -0.5),
        'mla_g_kv': 1.0 + nrm(ks[9], (L, MLA_KV_RANK), 0.02),
        'mla_w_ukv': nrm(ks[10], (L, MLA_KV_RANK, MLA_HEADS * (MLA_NOPE + MLA_V)), MLA_KV_RANK ** -0.5),
        'mla_w_o': nrm(ks[11], (L, MLA_WIDTH, D), MLA_WIDTH ** -0.5),
        'diff_lambda': nrm(ks[12], (L, 4, DIFF_HD), 0.1),
        'diff_g_subln': 1.0 + nrm(ks[13], (L, 2 * DIFF_HD), 0.02),
        'diff_w_o': nrm(ks[14], (L, DIFF_WIDTH, D), DIFF_WIDTH ** -0.5),
        'w_out': nrm(ks[15], (L, D, D), DEEPNORM_BETA * D ** -0.5),
        'ln1_g': 1.0 + nrm(ks[16], (L, D), 0.02),
        'ln1_b': nrm(ks[17], (L, D), 0.02),
        'moe_w_router': nrm(ks[18], (L, D, N_EXPERTS), D ** -0.5),
        'moe_b_router': nrm(ks[19], (L, N_EXPERTS), 0.01),
        'moe_w1': nrm(ks[20], (L, N_EXPERTS, D, EXPERT_FF), D ** -0.5),
        'moe_w3': nrm(ks[21], (L, N_EXPERTS, D, EXPERT_FF), D ** -0.5),
        'moe_w2': nrm(ks[22], (L, N_EXPERTS, EXPERT_FF, D), DEEPNORM_BETA * EXPERT_FF ** -0.5),
        'ln2_g': 1.0 + nrm(ks[23], (L, D), 0.02),
        'ln2_b': nrm(ks[24], (L, D), 0.02),
    }


def reference(x, c, ctx, c_ctx, w_ada, b_ada, w_in, mla_g_q, mla_w_uq, mla_g_kv, mla_w_ukv, mla_w_o,
              diff_lambda, diff_g_subln, diff_w_o, w_out, ln1_g, ln1_b, moe_w_router, moe_b_router,
              moe_w1, moe_w3, moe_w2, ln2_g, ln2_b):
    rows = x.shape[1] // GRID_W
    row = jnp.repeat(jnp.arange(rows, dtype=jnp.int32), GRID_W)
    col = jnp.tile(jnp.arange(GRID_W, dtype=jnp.int32), rows)
    for l in range(DEPTH):
        lambda_init = 0.8 - 0.6 * math.exp(-0.3 * l)
        lq1, lk1, lq2, lk2 = diff_lambda[l].astype(jnp.float32)
        lam = jnp.exp(jnp.sum(lq1 * lk1)) - jnp.exp(jnp.sum(lq2 * lk2)) + lambda_init
        mod = jax.nn.silu(c) @ w_ada[l] + b_ada[l]
        sh1, sc1, g1, sh2, sc2, g2 = jnp.split(mod[:, None, :], 6, axis=-1)
        mod_c = jax.nn.silu(c_ctx) @ w_ada[l] + b_ada[l]
        sh1c, sc1c, g1c, sh2c, sc2c, g2c = jnp.split(mod_c, 6)
        proj = (w_in[l], mla_g_q[l], mla_w_uq[l], mla_g_kv[l], mla_w_ukv[l])
        outp = (lam, lambda_init, diff_g_subln[l], mla_w_o[l], diff_w_o[l], w_out[l])
        moe = (moe_w_router[l], moe_b_router[l], moe_w1[l], moe_w3[l], moe_w2[l])
        qm_c, km_c, vm_c, dq_c, dk_c, dv_c, gt_c = mixer_projections(ctx * (1.0 + sc1c) + sh1c, *proj, None)
        qm, km, vm, dq, dk, dv, gt = mixer_projections(x * (1.0 + sc1) + sh1, *proj, (row, col))
        o_lat = mixer_output(qm, dq, gt,
                             jnp.concatenate([km_c, km], axis=2), jnp.concatenate([vm_c, vm], axis=2),
                             jnp.concatenate([dk_c, dk], axis=2), jnp.concatenate([dv_c, dv], axis=2),
                             *outp)
        x = layer_norm(DEEPNORM_ALPHA * x + g1 * o_lat, ln1_g[l], ln1_b[l])
        x = layer_norm(DEEPNORM_ALPHA * x + g2 * ec_moe(x * (1.0 + sc2) + sh2, *moe), ln2_g[l], ln2_b[l])
        if l + 1 < DEPTH:
            o_ctx = mixer_output(qm_c, dq_c, gt_c, km_c, vm_c, dk_c, dv_c, *outp)
            ctx = layer_norm(DEEPNORM_ALPHA * ctx + g1c * o_ctx, ln1_g[l], ln1_b[l])
            ctx = layer_norm(DEEPNORM_ALPHA * ctx + g2c * ec_moe(ctx * (1.0 + sc2c) + sh2c, *moe),
                             ln2_g[l], ln2_b[l])
    return x
```

```python
import functools
import math

import numpy as np
import jax
import jax.numpy as jnp
from jax import lax
from jax.experimental import pallas as pl
from jax.experimental.pallas import tpu as pltpu

F32 = jnp.float32
BF16 = jnp.bfloat16
I32 = jnp.int32

D_MODEL = 1024
DEPTH = 1
GRID_W = 64
ROPE_THETA = 10000.0
EPS = 1e-6
MLA_HEADS = 8
MLA_Q_RANK = 256
MLA_KV_RANK = 128
MLA_NOPE = 64
MLA_ROPE = 32
MLA_V = 64
DIFF_HEADS = 4
DIFF_HD = 64
N_EXPERTS = 16
EXPERT_FF = 1024
EC_CAPACITY = 2

MLA_Q_OFF = 0
MLA_KV_OFF = MLA_Q_OFF + MLA_Q_RANK
MLA_KR_OFF = MLA_KV_OFF + MLA_KV_RANK
DIFF_Q_OFF = MLA_KR_OFF + MLA_ROPE
DIFF_K_OFF = DIFF_Q_OFF + DIFF_HEADS * 2 * DIFF_HD
DIFF_V_OFF = DIFF_K_OFF + DIFF_HEADS * 2 * DIFF_HD
GATE_OFF = DIFF_V_OFF + DIFF_HEADS * 2 * DIFF_HD
N_IN = GATE_OFF + 2 * D_MODEL
DIFF_WIDTH = DIFF_HEADS * 2 * DIFF_HD
MLA_SCALE = (MLA_NOPE + MLA_ROPE) ** -0.5
DIFF_SCALE = DIFF_HD ** -0.5
DEEPNORM_ALPHA = (2 * DEPTH) ** 0.25
LAMBDA_INIT = 0.8 - 0.6 * math.exp(-0.3 * 0)
LOG2E = 1.4426950408889634

LANE = 128
MLA_W = MLA_HEADS * LANE
PX_Q = 0
PX_KV = PX_Q + MLA_Q_RANK
PX_KR = PX_KV + MLA_KV_RANK
PX_KRR = PX_KR + LANE
PX_DQ = PX_KRR + LANE
PX_DQR = PX_DQ + DIFF_WIDTH
PX_DK = PX_DQR + DIFF_WIDTH
PX_DKR = PX_DK + DIFF_WIDTH
PX_DV = PX_DKR + DIFF_WIDTH
PX_G = PX_DV + DIFF_WIDTH
PX_N = PX_G + 2 * D_MODEL
CX_KV = 0
CX_KR = CX_KV + MLA_KV_RANK
CX_DK = CX_KR + LANE
CX_DV = CX_DK + DIFF_WIDTH
CX_N = CX_DV + DIFF_WIDTH

VMEM_LIMIT = 56 * 1024 * 1024


def _cparams(sem):
    return pltpu.CompilerParams(dimension_semantics=sem, vmem_limit_bytes=VMEM_LIMIT)


def _rot_cols(w):
    d = w.shape[-1]
    h = d // 2
    q = h // 2
    parts = []
    for s in (0, h):
        parts += [-w[..., s + q:s + h], w[..., s:s + q]]
    return jnp.concatenate(parts, axis=-1)


def _rope_tables(seq):
    t = np.arange(seq)
    row = (t // GRID_W).astype(np.float64)
    col = (t % GRID_W).astype(np.float64)

    def axial(d):
        h = d // 2
        inv = ROPE_THETA ** (-(np.arange(h // 2, dtype=np.float64) * 2.0 / h))
        ar, ac = row[:, None] * inv, col[:, None] * inv
        cos = np.concatenate([np.cos(ar), np.cos(ar), np.cos(ac), np.cos(ac)], -1)
        sin = np.concatenate([np.sin(ar), np.sin(ar), np.sin(ac), np.sin(ac)], -1)
        return cos, sin

    c32, s32 = axial(MLA_ROPE)
    c64, s64 = axial(DIFF_HD)
    cos_m = np.ones((seq, LANE))
    sin_m = np.zeros((seq, LANE))
    cos_m[:, MLA_NOPE:MLA_NOPE + MLA_ROPE] = c32
    sin_m[:, MLA_NOPE:MLA_NOPE + MLA_ROPE] = s32
    cos_d = np.concatenate([c64, c64], -1)
    sin_d = np.concatenate([s64, s64], -1)
    return tuple(jnp.asarray(a, F32) for a in (cos_m, sin_m, cos_d, sin_d))


def _rms(x, g):
    return x * lax.rsqrt(jnp.mean(x * x, axis=-1, keepdims=True) + EPS) * g


def _layer_norm(x, g, b):
    mu = jnp.mean(x, axis=-1, keepdims=True)
    xc = x - mu
    var = jnp.mean(xc * xc, axis=-1, keepdims=True)
    return xc * lax.rsqrt(var + EPS) * g + b


def _dot(a, b):
    return jnp.dot(a, b, preferred_element_type=F32)


def _dot_nt(a, b):
    return lax.dot_general(a, b, (((1,), (1,)), ((), ())), preferred_element_type=F32)


def _dot_tn(a, b):
    return lax.dot_general(a, b, (((0,), (0,)), ((), ())), preferred_element_type=F32)


def _ada_kernel(c_ref, w_ref, b_ref, o_ref):
    c = c_ref[...]
    s = (c / (1.0 + jnp.exp(-c))).astype(BF16)
    o_ref[...] = _dot(s, w_ref[...].astype(BF16)) + b_ref[...]


def _ada(cc, w_ada, b_ada):
    n = w_ada.shape[1]
    tn = 1024
    return pl.pallas_call(
        _ada_kernel,
        grid=(n // tn,),
        in_specs=[pl.BlockSpec((8, D_MODEL), lambda j: (0, 0)),
                  pl.BlockSpec((D_MODEL, tn), lambda j: (0, j)),
                  pl.BlockSpec((1, tn), lambda j: (0, j))],
        out_specs=pl.BlockSpec((8, tn), lambda j: (0, j)),
        out_shape=jax.ShapeDtypeStruct((8, n), F32),
        compiler_params=_cparams(("arbitrary",)),
        name="ada",
    )(cc, w_ada, b_ada.reshape(1, n))


def _proj_lat_kernel(x_ref, mod_ref, w_ref, gq_ref, wuq_ref, wuqr_ref, gkv_ref, wukv_ref,
                     cm_ref, sm_ref, cd_ref, sd_ref,
                     q_ref, k_ref, v_ref, dq_ref, dk_ref, dv_ref, g_ref):
    tm = x_ref.shape[1]
    sh = mod_ref[0, :, 0:D_MODEL]
    sc = mod_ref[0, :, D_MODEL:2 * D_MODEL]
    h = (x_ref[0] * (1.0 + sc) + sh).astype(BF16)

    def proj(a, n):
        return _dot(h, w_ref[:, a:a + n])

    lane = lax.broadcasted_iota(I32, (tm, LANE), 1)
    lo = lane < MLA_NOPE
    cm, sm, cd, sd = cm_ref[...], sm_ref[...], cd_ref[...], sd_ref[...]

    cq = _rms(proj(PX_Q, MLA_Q_RANK), gq_ref[...]).astype(BF16)
    q = _dot(cq, wuq_ref[...])
    qr = _dot(cq, wuqr_ref[...])
    for hd in range(MLA_HEADS):
        sl = slice(hd * LANE, (hd + 1) * LANE)
        q_ref[0, :, sl] = ((q[:, sl] * cm + qr[:, sl] * sm) * (MLA_SCALE * LOG2E)).astype(BF16)

    ckv = _rms(proj(PX_KV, MLA_KV_RANK), gkv_ref[...]).astype(BF16)
    kv = _dot(ckv, wukv_ref[...])
    kr = proj(PX_KR, LANE) * cm + proj(PX_KRR, LANE) * sm
    for hd in range(MLA_HEADS):
        sl = slice(hd * LANE, (hd + 1) * LANE)
        kvh = kv[:, sl]
        k_ref[0, :, sl] = jnp.where(lo, kvh, kr).astype(BF16)
        v_ref[0, :, sl] = jnp.where(lo, 0.0, kvh).astype(BF16)

    dq = proj(PX_DQ, DIFF_WIDTH)
    dqr = proj(PX_DQR, DIFF_WIDTH)
    dk = proj(PX_DK, DIFF_WIDTH)
    dkr = proj(PX_DKR, DIFF_WIDTH)
    for hd in range(DIFF_HEADS):
        sl = slice(hd * LANE, (hd + 1) * LANE)
        qb = (dq[:, sl] * cd + dqr[:, sl] * sd) * (DIFF_SCALE * LOG2E)
        dq_ref[0, :, (2 * hd) * LANE:(2 * hd + 1) * LANE] = jnp.where(lo, qb, 0.0).astype(BF16)
        dq_ref[0, :, (2 * hd + 1) * LANE:(2 * hd + 2) * LANE] = jnp.where(lo, 0.0, qb).astype(BF16)
        dk_ref[0, :, sl] = (dk[:, sl] * cd + dkr[:, sl] * sd).astype(BF16)
    dv_ref[0] = proj(PX_DV, DIFF_WIDTH).astype(BF16)
    pg = proj(PX_G, 2 * D_MODEL)
    g_ref[0] = 1.0 / (1.0 + jnp.exp(-pg))


def _proj_ctx_kernel(x_ref, mod_ref, w_ref, gkv_ref, wukv_ref, k_ref, v_ref, dk_ref, dv_ref):
    tm = x_ref.shape[1]
    sh = mod_ref[0, :, 0:D_MODEL]
    sc = mod_ref[0, :, D_MODEL:2 * D_MODEL]
    h = (x_ref[0] * (1.0 + sc) + sh).astype(BF16)

    def proj(a, n):
        return _dot(h, w_ref[:, a:a + n])

    lo = lax.broadcasted_iota(I32, (tm, LANE), 1) < MLA_NOPE
    ckv = _rms(proj(CX_KV, MLA_KV_RANK), gkv_ref[...]).astype(BF16)
    kv = _dot(ckv, wukv_ref[...])
    kr = proj(CX_KR, LANE)
    for hd in range(MLA_HEADS):
        sl = slice(hd * LANE, (hd + 1) * LANE)
        kvh = kv[:, sl]
        k_ref[0, :, sl] = jnp.where(lo, kvh, kr).astype(BF16)
        v_ref[0, :, sl] = jnp.where(lo, 0.0, kvh).astype(BF16)
    dk_ref[0] = proj(CX_DK, DIFF_WIDTH).astype(BF16)
    dv_ref[0] = proj(CX_DV, DIFF_WIDTH).astype(BF16)


def _full(shape):
    nd = len(shape)
    return pl.BlockSpec(shape, lambda *_: (0,) * nd)


def _proj_lat(x, mod3, wx, gq, wuq, wuqr, gkv, wukv, tabs, tm):
    b, s, _ = x.shape
    row = lambda w: pl.BlockSpec((1, tm, w), lambda bi, ti: (bi, ti, 0))
    tab = pl.BlockSpec((tm, LANE), lambda bi, ti: (ti, 0))
    outs = [(MLA_W, BF16), (MLA_W, BF16), (MLA_W, BF16), (2 * DIFF_WIDTH, BF16), (DIFF_WIDTH, BF16),
            (DIFF_WIDTH, BF16), (2 * D_MODEL, F32)]
    return pl.pallas_call(
        _proj_lat_kernel,
        grid=(b, s // tm),
        in_specs=[row(D_MODEL),
                  pl.BlockSpec((1, 1, 6 * D_MODEL), lambda bi, ti: (bi, 0, 0)),
                  _full(wx.shape), _full(gq.shape), _full(wuq.shape), _full(wuqr.shape),
                  _full(gkv.shape), _full(wukv.shape), tab, tab, tab, tab],
        out_specs=[row(w) for w, _ in outs],
        out_shape=[jax.ShapeDtypeStruct((b, s, w), dt) for w, dt in outs],
        compiler_params=_cparams(("arbitrary", "arbitrary")),
        name="proj_lat",
    )(x, mod3, wx, gq, wuq, wuqr, gkv, wukv, *tabs)


def _proj_ctx(ctx, mod3, wc, gkv, wukv):
    b, s, _ = ctx.shape
    row = lambda w: pl.BlockSpec((1, s, w), lambda bi: (bi, 0, 0))
    outs = [MLA_W, MLA_W, DIFF_WIDTH, DIFF_WIDTH]
    return pl.pallas_call(
        _proj_ctx_kernel,
        grid=(b,),
        in_specs=[row(D_MODEL),
                  pl.BlockSpec((1, 1, 6 * D_MODEL), lambda bi: (b, 0, 0)),
                  _full(wc.shape), _full(gkv.shape), _full(wukv.shape)],
        out_specs=[row(w) for w in outs],
        out_shape=[jax.ShapeDtypeStruct((b, s, w), BF16) for w in outs],
        compiler_params=_cparams(("arbitrary",)),
        name="proj_ctx",
    )(ctx, mod3, wc, gkv, wukv)


def _flash(q, kc_ref, vc_ref, kl_ref, vl_ref, tk):
    m_rows = q.shape[0]

    def step(k, v, carry):
        m_i, l_i, acc = carry
        s = _dot_nt(q, k)
        m_new = jnp.maximum(m_i, jnp.max(s, axis=-1, keepdims=True))
        p = jnp.exp2(s - m_new)
        alpha = jnp.exp2(m_i - m_new)
        l_new = alpha * l_i + jnp.sum(p, axis=-1, keepdims=True)
        acc = alpha * acc + _dot(p.astype(BF16), v)
        return m_new, l_new, acc

    init = (jnp.full((m_rows, 1), -jnp.inf, F32), jnp.zeros((m_rows, 1), F32), jnp.zeros((m_rows, LANE), F32))
    carry = step(kc_ref[0], vc_ref[0], init)

    def body(i, carry):
        off = pl.multiple_of(i * tk, tk)
        return step(kl_ref[0, pl.ds(off, tk), :], vl_ref[0, pl.ds(off, tk), :], carry)

    _, l_i, acc = lax.fori_loop(0, kl_ref.shape[1] // tk, body, carry)
    return acc, l_i


def _mla_attn_kernel(q_ref, kc_ref, vc_ref, kl_ref, vl_ref, o_ref, *, tk):
    acc, l_i = _flash(q_ref[0], kc_ref, vc_ref, kl_ref, vl_ref, tk)
    o_ref[0] = (acc / l_i).astype(BF16)


def _diff_attn_kernel(lam_ref, g_ref, q1_ref, q2_ref, kc_ref, vc_ref, kl_ref, vl_ref, o_ref, *, tk):
    tq = q1_ref.shape[1]
    q = jnp.concatenate([q1_ref[0], q2_ref[0]], axis=0)
    acc, l_i = _flash(q, kc_ref, vc_ref, kl_ref, vl_ref, tk)
    o = acc / l_i
    dl = lam_ref[...]
    lam = (jnp.exp(jnp.sum(dl[0:1] * dl[1:2], axis=-1, keepdims=True))
           - jnp.exp(jnp.sum(dl[2:3] * dl[3:4], axis=-1, keepdims=True)) + LAMBDA_INIT)
    w = o[:tq] - lam * o[tq:]
    o_ref[0] = (_rms(w, g_ref[...]) * (1.0 - LAMBDA_INIT)).astype(BF16)


def _mla_attn(q, kc, vc, kl, vl, tq, tk):
    b, s, _ = q.shape
    sc = kc.shape[1]
    qs = pl.BlockSpec((1, tq, LANE), lambda bi, hi, qi: (bi, qi, hi))
    cs = pl.BlockSpec((1, sc, LANE), lambda bi, hi, qi: (bi, 0, hi))
    ls = pl.BlockSpec((1, s, LANE), lambda bi, hi, qi: (bi, 0, hi))
    return pl.pallas_call(
        functools.partial(_mla_attn_kernel, tk=tk),
        grid=(b, MLA_HEADS, s // tq),
        in_specs=[qs, cs, cs, ls, ls],
        out_specs=qs,
        out_shape=jax.ShapeDtypeStruct((b, s, MLA_W), BF16),
        compiler_params=_cparams(("arbitrary", "arbitrary", "arbitrary")),
        name="mla_attn",
    )(q, kc, vc, kl, vl)


def _diff_attn(lam4, g_subln, dq, kc, vc, kl, vl, tq, tk):
    b, s, _ = dq.shape
    sc = kc.shape[1]
    q1 = pl.BlockSpec((1, tq, LANE), lambda bi, hi, qi: (bi, qi, 2 * hi))
    q2 = pl.BlockSpec((1, tq, LANE), lambda bi, hi, qi: (bi, qi, 2 * hi + 1))
    cs = pl.BlockSpec((1, sc, LANE), lambda bi, hi, qi: (bi, 0, hi))
    ls = pl.BlockSpec((1, s, LANE), lambda bi, hi, qi: (bi, 0, hi))
    return pl.pallas_call(
        functools.partial(_diff_attn_kernel, tk=tk),
        grid=(b, DIFF_HEADS, s // tq),
        in_specs=[_full(lam4.shape), _full(g_subln.shape), q1, q2, cs, cs, ls, ls],
        out_specs=pl.BlockSpec((1, tq, LANE), lambda bi, hi, qi: (bi, qi, hi)),
        out_shape=jax.ShapeDtypeStruct((b, s, DIFF_WIDTH), BF16),
        compiler_params=_cparams(("arbitrary", "arbitrary", "arbitrary")),
        name="diff_attn",
    )(lam4, g_subln, dq, dq, kc, vc, kl, vl)


def _mix_kernel(om_ref, od_ref, g_ref, x_ref, mod_ref, wom_ref, wod_ref, wout_ref, lng_ref, lnb_ref,
                wrh_ref, wrl_ref, x1_ref, h2_ref, lg_ref):
    gm = g_ref[0, :, 0:D_MODEL]
    gd = g_ref[0, :, D_MODEL:2 * D_MODEL]
    y = gm * _dot(om_ref[0], wom_ref[...]) + gd * _dot(od_ref[0], wod_ref[...])
    z = _dot(y.astype(BF16), wout_ref[...])
    g1 = mod_ref[0, :, 2 * D_MODEL:3 * D_MODEL]
    sh2 = mod_ref[0, :, 3 * D_MODEL:4 * D_MODEL]
    sc2 = mod_ref[0, :, 4 * D_MODEL:5 * D_MODEL]
    x1 = _layer_norm(DEEPNORM_ALPHA * x_ref[0] + g1 * z, lng_ref[...], lnb_ref[...])
    x1_ref[0] = x1
    h2 = x1 * (1.0 + sc2) + sh2
    hi = h2.astype(BF16)
    lo = (h2 - hi.astype(F32)).astype(BF16)
    h2_ref[0] = hi
    lg_ref[0] = _dot_nt(wrh_ref[...], hi) + (_dot_nt(wrh_ref[...], lo) + _dot_nt(wrl_ref[...], hi))


def _mix(om, od, gates, x, mod3, wom, wod, wout, lng, lnb, wrh, wrl, tm):
    b, s, _ = x.shape
    row = lambda w: pl.BlockSpec((1, tm, w), lambda bi, ti: (bi, ti, 0))
    return pl.pallas_call(
        _mix_kernel,
        grid=(b, s // tm),
        in_specs=[row(MLA_W), row(DIFF_WIDTH), row(2 * D_MODEL), row(D_MODEL),
                  pl.BlockSpec((1, 1, 6 * D_MODEL), lambda bi, ti: (bi, 0, 0)),
                  _full(wom.shape), _full(wod.shape), _full(wout.shape), _full(lng.shape), _full(lnb.shape),
                  _full(wrh.shape), _full(wrl.shape)],
        out_specs=[row(D_MODEL), row(D_MODEL),
                   pl.BlockSpec((1, N_EXPERTS, tm), lambda bi, ti: (bi, 0, ti))],
        out_shape=[jax.ShapeDtypeStruct((b, s, D_MODEL), F32), jax.ShapeDtypeStruct((b, s, D_MODEL), BF16),
                   jax.ShapeDtypeStruct((b, N_EXPERTS, s), F32)],
        compiler_params=_cparams(("arbitrary", "arbitrary")),
        name="mix",
    )(om, od, gates, x, mod3, wom, wod, wout, lng, lnb, wrh, wrl)


def _route_kernel(lg_ref, br_ref, pos_ref, gate_ref, st_ref, *, cap, tile):
    e, s = lg_ref.shape[1], lg_ref.shape[2]
    lg = lg_ref[0] + br_ref[...]
    ex = jnp.exp(lg - jnp.max(lg, axis=0, keepdims=True))
    aff = ex / jnp.sum(ex, axis=0, keepdims=True)
    gate_ref[0] = aff
    bits = pltpu.bitcast(aff, I32)

    def count(mask):
        return jnp.sum(mask.astype(I32), axis=1, keepdims=True)

    def search(i, thr):
        cand = thr | (jnp.int32(1) << (30 - i))
        return jnp.where(count(bits >= cand) >= cap, cand, thr)

    thr = lax.fori_loop(0, 31, search, jnp.zeros((e, 1), I32))
    gt = bits > thr
    eq = bits == thr
    need = cap - count(gt)

    r = lax.broadcasted_iota(I32, (LANE, LANE), 0)
    c = lax.broadcasted_iota(I32, (LANE, LANE), 1)
    tri = (r < c).astype(BF16)
    lane = lax.broadcasted_iota(I32, (e, LANE), 1)

    def prefix(mask_fn, emit):
        carry = jnp.zeros((e, 1), F32)
        for j in range(s // LANE):
            m = mask_fn(j)
            emit(j, carry + _dot(m.astype(BF16), tri), m, carry)
            carry = carry + jnp.sum(m.astype(F32), axis=1, keepdims=True)

    def eq_blk(j):
        return eq[:, j * LANE:(j + 1) * LANE]

    def emit_sel(j, rank, m, carry):
        sel = gt[:, j * LANE:(j + 1) * LANE] | (m & (rank < need.astype(F32)))
        pos_ref[0, :, j * LANE:(j + 1) * LANE] = sel.astype(I32)

    prefix(eq_blk, emit_sel)

    starts = [jnp.zeros((e, LANE), I32)]

    def sel_blk(j):
        return pos_ref[0, :, j * LANE:(j + 1) * LANE] > 0

    def emit_pos(j, rank, m, carry):
        if (j * LANE) % tile == 0:
            starts[0] = jnp.where(lane == (j * LANE) // tile, carry.astype(I32), starts[0])
        pos_ref[0, :, j * LANE:(j + 1) * LANE] = jnp.where(m, rank.astype(I32), -1)

    prefix(sel_blk, emit_pos)
    st_ref[0] = jnp.where(lane == s // tile, cap, starts[0])


def _route(lgt, b_router, cap, tile):
    b, e, s = lgt.shape
    blk = pl.BlockSpec((1, e, s), lambda bi: (bi, 0, 0))
    return pl.pallas_call(
        functools.partial(_route_kernel, cap=cap, tile=tile),
        grid=(b,),
        in_specs=[blk, _full((e, 1))],
        out_specs=[blk, blk, pl.BlockSpec((1, e, LANE), lambda bi: (bi, 0, 0))],
        out_shape=[jax.ShapeDtypeStruct((b, e, s), I32), jax.ShapeDtypeStruct((b, e, s), F32),
                   jax.ShapeDtypeStruct((b, e, LANE), I32)],
        compiler_params=_cparams(("arbitrary",)),
        name="route",
    )(lgt, b_router.reshape(e, 1))


def _ffn_kernel(st_ref, h_ref, pos_ref, gate_ref, w1_ref, w3_ref, w2_ref, ye_ref, xe_ref, gs_ref, *, chunk):
    bi, ei, ti = pl.program_id(0), pl.program_id(1), pl.program_id(2)
    cap, tile = xe_ref.shape[0], h_ref.shape[1]
    s0 = st_ref[bi, ei, ti]
    s1 = st_ref[bi, ei, ti + 1]

    @pl.when(ti == 0)
    def _():
        xe_ref[...] = jnp.zeros_like(xe_ref)
        gs_ref[...] = jnp.zeros_like(gs_ref)

    pos = pos_ref[0, 0]
    gate = gate_ref[0, 0]
    slot = lax.broadcasted_iota(I32, (chunk, tile), 0)
    for c in range(cap // chunk):
        @pl.when((s0 < (c + 1) * chunk) & (s1 > c * chunk))
        def _():
            hit = slot == (pos - c * chunk)
            rows = slice(c * chunk, (c + 1) * chunk)
            xe_ref[rows, :] += _dot(hit.astype(BF16), h_ref[0])
            gs_ref[rows, :] += jnp.sum(jnp.where(hit, gate, 0.0), axis=1, keepdims=True)

    @pl.when(ti == pl.num_programs(2) - 1)
    def _():
        xe = xe_ref[...].astype(BF16)
        a = _dot(xe, w1_ref[0])
        hid = ((a / (1.0 + jnp.exp(-a))) * _dot(xe, w3_ref[0])).astype(BF16)
        ye_ref[0, 0] = (_dot(hid, w2_ref[0]) * gs_ref[...]).astype(BF16)


def _ffn(starts, h2, pos4, gate4, w1, w3, w2, cap, tile, chunk):
    b, s, d = h2.shape
    e = w1.shape[0]
    lane_blk = pl.BlockSpec((1, 1, 1, tile), lambda bi, ei, ti, st: (bi, ei, 0, ti))
    wspec = lambda shp: pl.BlockSpec((1,) + shp, lambda bi, ei, ti, st: (ei, 0, 0))
    gs = pltpu.PrefetchScalarGridSpec(
        num_scalar_prefetch=1,
        grid=(b, e, s // tile),
        in_specs=[pl.BlockSpec((1, tile, d), lambda bi, ei, ti, st: (bi, ti, 0)),
                  lane_blk, lane_blk, wspec(w1.shape[1:]), wspec(w3.shape[1:]), wspec(w2.shape[1:])],
        out_specs=pl.BlockSpec((1, 1, cap, d), lambda bi, ei, ti, st: (bi, ei, 0, 0)),
        scratch_shapes=[pltpu.VMEM((cap, d), F32), pltpu.VMEM((cap, 1), F32)],
    )
    return pl.pallas_call(
        functools.partial(_ffn_kernel, chunk=chunk),
        grid_spec=gs,
        out_shape=jax.ShapeDtypeStruct((b, e, cap, d), BF16),
        compiler_params=_cparams(("arbitrary", "arbitrary", "arbitrary")),
        name="ffn",
    )(starts, h2, pos4, gate4, w1, w3, w2)


def _combine_kernel(st_ref, ye_ref, pos_ref, x1_ref, mod_ref, lng_ref, lnb_ref, o_ref, acc_ref, *, chunk):
    bi, ti, ei = pl.program_id(0), pl.program_id(1), pl.program_id(2)
    cap, tile = ye_ref.shape[2], x1_ref.shape[1]
    s0 = st_ref[bi, ei, ti]
    s1 = st_ref[bi, ei, ti + 1]

    @pl.when(ei == 0)
    def _():
        acc_ref[...] = jnp.zeros_like(acc_ref)

    pos = pos_ref[0, 0]
    slot = lax.broadcasted_iota(I32, (chunk, tile), 0)
    for c in range(cap // chunk):
        @pl.when((s0 < (c + 1) * chunk) & (s1 > c * chunk))
        def _():
            hit = (slot == (pos - c * chunk)).astype(BF16)
            acc_ref[...] += _dot_tn(hit, ye_ref[0, 0, c * chunk:(c + 1) * chunk, :])

    @pl.when(ei == pl.num_programs(2) - 1)
    def _():
        g2 = mod_ref[0, :, 5 * D_MODEL:6 * D_MODEL]
        o_ref[0] = _layer_norm(DEEPNORM_ALPHA * x1_ref[0] + g2 * acc_ref[...], lng_ref[...], lnb_ref[...])


def _combine(starts, ye, pos4, x1, mod3, lng, lnb, tile, chunk):
    b, s, d = x1.shape
    e, cap = ye.shape[1], ye.shape[2]
    gs = pltpu.PrefetchScalarGridSpec(
        num_scalar_prefetch=1,
        grid=(b, s // tile, e),
        in_specs=[pl.BlockSpec((1, 1, cap, d), lambda bi, ti, ei, st: (bi, ei, 0, 0)),
                  pl.BlockSpec((1, 1, 1, tile), lambda bi, ti, ei, st: (bi, ei, 0, ti)),
                  pl.BlockSpec((1, tile, d), lambda bi, ti, ei, st: (bi, ti, 0)),
                  pl.BlockSpec((1, 1, 6 * D_MODEL), lambda bi, ti, ei, st: (bi, 0, 0)),
                  pl.BlockSpec((1, d), lambda bi, ti, ei, st: (0, 0)),
                  pl.BlockSpec((1, d), lambda bi, ti, ei, st: (0, 0))],
        out_specs=pl.BlockSpec((1, tile, d), lambda bi, ti, ei, st: (bi, ti, 0)),
        scratch_shapes=[pltpu.VMEM((tile, d), F32)],
    )
    return pl.pallas_call(
        functools.partial(_combine_kernel, chunk=chunk),
        grid_spec=gs,
        out_shape=jax.ShapeDtypeStruct((b, s, d), F32),
        compiler_params=_cparams(("arbitrary", "arbitrary", "arbitrary")),
        name="combine",
    )(starts, ye, pos4, x1, mod3, lng, lnb)


def _head_pad(w, n_heads, width, offset=0):
    k = w.shape[0]
    w3 = w.reshape(k, n_heads, width)
    out = jnp.zeros((k, n_heads, LANE), w.dtype).at[:, :, offset:offset + width].set(w3)
    return out.reshape(k, n_heads * LANE)


def _prep_weights(w_in, w_uq, w_o_mla):
    kr = w_in[:, MLA_KR_OFF:DIFF_Q_OFF]
    dqw = w_in[:, DIFF_Q_OFF:DIFF_K_OFF]
    dkw = w_in[:, DIFF_K_OFF:DIFF_V_OFF]
    dvw = w_in[:, DIFF_V_OFF:GATE_OFF]
    k = w_in.shape[0]

    def rot_groups(w):
        return _rot_cols(w.reshape(k, -1, DIFF_HD)).reshape(k, -1)

    kr_pad = _head_pad(kr, 1, MLA_ROPE, MLA_NOPE)
    krr_pad = _head_pad(_rot_cols(kr), 1, MLA_ROPE, MLA_NOPE)
    wx = jnp.concatenate([w_in[:, MLA_Q_OFF:MLA_KR_OFF], kr_pad, krr_pad, dqw, rot_groups(dqw),
                          dkw, rot_groups(dkw), dvw, w_in[:, GATE_OFF:]], axis=1).astype(BF16)
    wc = jnp.concatenate([w_in[:, MLA_KV_OFF:MLA_KR_OFF], kr_pad, dkw, dvw], axis=1).astype(BF16)
    hd = MLA_NOPE + MLA_ROPE
    uq3 = w_uq.reshape(w_uq.shape[0], MLA_HEADS, hd)
    uq_rot = jnp.concatenate([jnp.zeros_like(uq3[..., :MLA_NOPE]), _rot_cols(uq3[..., MLA_NOPE:])], -1)
    wuq = _head_pad(w_uq, MLA_HEADS, hd).astype(BF16)
    wuqr = _head_pad(uq_rot.reshape(w_uq.shape[0], -1), MLA_HEADS, hd).astype(BF16)
    wom = jnp.zeros((MLA_HEADS, LANE, D_MODEL), w_o_mla.dtype).at[:, MLA_NOPE:, :].set(
        w_o_mla.reshape(MLA_HEADS, MLA_V, D_MODEL)).reshape(MLA_W, D_MODEL).astype(BF16)
    return wx, wc, wuq, wuqr, wom


def kernel(x, c, ctx, c_ctx, w_ada, b_ada, w_in, mla_g_q, mla_w_uq, mla_g_kv, mla_w_ukv, mla_w_o, diff_lambda,
           diff_g_subln, diff_w_o, w_out, ln1_g, ln1_b, moe_w_router, moe_b_router, moe_w1, moe_w3, moe_w2,
           ln2_g, ln2_b):
    b, s, d = x.shape
    assert d == D_MODEL and w_ada.shape[0] == DEPTH == 1 and b <= 4
    cap = EC_CAPACITY * s // N_EXPERTS
    tile = min(1024, s)
    chunk = min(128, cap)
    tm = min(512, s)
    tq = min(256, s)
    tk = min(512, s)

    cc = jnp.zeros((8, d), F32).at[:b].set(c).at[b].set(c_ctx)
    mod3 = _ada(cc, w_ada[0], b_ada[0]).reshape(8, 1, 6 * d)

    wx, wc, wuq, wuqr, wom = _prep_weights(w_in[0], mla_w_uq[0], mla_w_o[0])
    gq = mla_g_q[0].reshape(1, -1)
    gkv = mla_g_kv[0].reshape(1, -1)
    wukv = mla_w_ukv[0].astype(BF16)
    tabs = _rope_tables(s)

    q, k, v, dq, dk, dv, gates = _proj_lat(x, mod3, wx, gq, wuq, wuqr, gkv, wukv, tabs, tm)
    kc, vc, dkc, dvc = _proj_ctx(ctx, mod3, wc, gkv, wukv)

    o_mla = _mla_attn(q, kc, vc, k, v, tq, tk)
    o_diff = _diff_attn(diff_lambda[0], diff_g_subln[0].reshape(1, -1), dq, dkc, dvc, dk, dv, tq, tk)

    wr = moe_w_router[0].T
    wrh = wr.astype(BF16)
    wrl = (wr - wrh.astype(F32)).astype(BF16)
    x1, h2, lgt = _mix(o_mla, o_diff, gates, x, mod3, wom, diff_w_o[0].astype(BF16), w_out[0].astype(BF16),
                       ln1_g[0].reshape(1, -1), ln1_b[0].reshape(1, -1), wrh, wrl, tm)

    pos, gate, starts = _route(lgt, moe_b_router[0], cap, tile)
    pos4 = pos.reshape(b, N_EXPERTS, 1, s)
    gate4 = gate.reshape(b, N_EXPERTS, 1, s)
    ye = _ffn(starts, h2, pos4, gate4, moe_w1[0].astype(BF16), moe_w3[0].astype(BF16), moe_w2[0].astype(BF16),
              cap, tile, chunk)
    return _combine(starts, ye, pos4, x1, mod3, ln2_g[0].reshape(1, -1), ln2_b[0].reshape(1, -1), tile, chunk)
```

```python
import functools
import math

import numpy as np
import jax
import jax.numpy as jnp
from jax import lax
from jax.experimental import pallas as pl
from jax.experimental.pallas import tpu as pltpu

F32 = jnp.float32
BF16 = jnp.bfloat16
I32 = jnp.int32

D_MODEL = 1024
DEPTH = 1
GRID_W = 64
ROPE_THETA = 10000.0
EPS = 1e-6
MLA_HEADS = 8
MLA_Q_RANK = 256
MLA_KV_RANK = 128
MLA_NOPE = 64
MLA_ROPE = 32
MLA_V = 64
DIFF_HEADS = 4
DIFF_HD = 64
N_EXPERTS = 16
EXPERT_FF = 1024
EC_CAPACITY = 2

MLA_Q_OFF = 0
MLA_KV_OFF = MLA_Q_OFF + MLA_Q_RANK
MLA_KR_OFF = MLA_KV_OFF + MLA_KV_RANK
DIFF_Q_OFF = MLA_KR_OFF + MLA_ROPE
DIFF_K_OFF = DIFF_Q_OFF + DIFF_HEADS * 2 * DIFF_HD
DIFF_V_OFF = DIFF_K_OFF + DIFF_HEADS * 2 * DIFF_HD
GATE_OFF = DIFF_V_OFF + DIFF_HEADS * 2 * DIFF_HD
N_IN = GATE_OFF + 2 * D_MODEL
DIFF_WIDTH = DIFF_HEADS * 2 * DIFF_HD
MLA_SCALE = (MLA_NOPE + MLA_ROPE) ** -0.5
DIFF_SCALE = DIFF_HD ** -0.5
DEEPNORM_ALPHA = (2 * DEPTH) ** 0.25
LAMBDA_INIT = 0.8 - 0.6 * math.exp(-0.3 * 0)
LOG2E = 1.4426950408889634

LANE = 128
MLA_W = MLA_HEADS * LANE
PX_Q = 0
PX_KV = PX_Q + MLA_Q_RANK
PX_KR = PX_KV + MLA_KV_RANK
PX_KRR = PX_KR + LANE
PX_DQ = PX_KRR + LANE
PX_DQR = PX_DQ + DIFF_WIDTH
PX_DK = PX_DQR + DIFF_WIDTH
PX_DKR = PX_DK + DIFF_WIDTH
PX_DV = PX_DKR + DIFF_WIDTH
PX_G = PX_DV + DIFF_WIDTH
PX_N = PX_G + 2 * D_MODEL
CX_KV = 0
CX_KR = CX_KV + MLA_KV_RANK
CX_DK = CX_KR + LANE
CX_DV = CX_DK + DIFF_WIDTH
CX_N = CX_DV + DIFF_WIDTH

VMEM_LIMIT = 56 * 1024 * 1024


def _cparams(sem):
    return pltpu.CompilerParams(dimension_semantics=sem, vmem_limit_bytes=VMEM_LIMIT)


def _rot_cols(w):
    d = w.shape[-1]
    h = d // 2
    q = h // 2
    parts = []
    for s in (0, h):
        parts += [-w[..., s + q:s + h], w[..., s:s + q]]
    return jnp.concatenate(parts, axis=-1)


def _rope_tables(seq):
    t = np.arange(seq)
    row = (t // GRID_W).astype(np.float64)
    col = (t % GRID_W).astype(np.float64)

    def axial(d):
        h = d // 2
        inv = ROPE_THETA ** (-(np.arange(h // 2, dtype=np.float64) * 2.0 / h))
        ar, ac = row[:, None] * inv, col[:, None] * inv
        cos = np.concatenate([np.cos(ar), np.cos(ar), np.cos(ac), np.cos(ac)], -1)
        sin = np.concatenate([np.sin(ar), np.sin(ar), np.sin(ac), np.sin(ac)], -1)
        return cos, sin

    c32, s32 = axial(MLA_ROPE)
    c64, s64 = axial(DIFF_HD)
    cos_m = np.ones((seq, LANE))
    sin_m = np.zeros((seq, LANE))
    cos_m[:, MLA_NOPE:MLA_NOPE + MLA_ROPE] = c32
    sin_m[:, MLA_NOPE:MLA_NOPE + MLA_ROPE] = s32
    cos_d = np.concatenate([c64, c64], -1)
    sin_d = np.concatenate([s64, s64], -1)
    return tuple(jnp.asarray(a, F32) for a in (cos_m, sin_m, cos_d, sin_d))


def _rms(x, g):
    return x * lax.rsqrt(jnp.mean(x * x, axis=-1, keepdims=True) + EPS) * g


def _layer_norm(x, g, b):
    mu = jnp.mean(x, axis=-1, keepdims=True)
    xc = x - mu
    var = jnp.mean(xc * xc, axis=-1, keepdims=True)
    return xc * lax.rsqrt(var + EPS) * g + b


def _dot(a, b):
    return jnp.dot(a, b, preferred_element_type=F32)


def _dot_nt(a, b):
    return lax.dot_general(a, b, (((1,), (1,)), ((), ())), preferred_element_type=F32)


def _dot_tn(a, b):
    return lax.dot_general(a, b, (((0,), (0,)), ((), ())), preferred_element_type=F32)


def _ada_kernel(c_ref, w_ref, b_ref, o_ref):
    c = c_ref[...]
    s = (c / (1.0 + jnp.exp(-c))).astype(BF16)
    o_ref[...] = _dot(s, w_ref[...].astype(BF16)) + b_ref[...]


def _ada(cc, w_ada, b_ada):
    n = w_ada.shape[1]
    tn = 1024
    return pl.pallas_call(
        _ada_kernel,
        grid=(n // tn,),
        in_specs=[pl.BlockSpec((8, D_MODEL), lambda j: (0, 0)),
                  pl.BlockSpec((D_MODEL, tn), lambda j: (0, j)),
                  pl.BlockSpec((1, tn), lambda j: (0, j))],
        out_specs=pl.BlockSpec((8, tn), lambda j: (0, j)),
        out_shape=jax.ShapeDtypeStruct((8, n), F32),
        compiler_params=_cparams(("arbitrary",)),
        name="ada",
    )(cc, w_ada, b_ada.reshape(1, n))


def _proj_lat_kernel(x_ref, mod_ref, w_ref, gq_ref, wuq_ref, wuqr_ref, gkv_ref, wukv_ref,
                     cm_ref, sm_ref, cd_ref, sd_ref,
                     q_ref, k_ref, v_ref, dq_ref, dk_ref, dv_ref, g_ref):
    tm = x_ref.shape[1]
    sh = mod_ref[0, :, 0:D_MODEL]
    sc = mod_ref[0, :, D_MODEL:2 * D_MODEL]
    h = (x_ref[0] * (1.0 + sc) + sh).astype(BF16)

    def proj(a, n):
        return _dot(h, w_ref[:, a:a + n])

    lane = lax.broadcasted_iota(I32, (tm, LANE), 1)
    lo = lane < MLA_NOPE
    one0 = (lane == 0).astype(F32)
    cm, sm, cd, sd = cm_ref[...], sm_ref[...], cd_ref[...], sd_ref[...]

    cq = _rms(proj(PX_Q, MLA_Q_RANK), gq_ref[...]).astype(BF16)
    q = _dot(cq, wuq_ref[...])
    qr = _dot(cq, wuqr_ref[...])
    for hd in range(MLA_HEADS):
        sl = slice(hd * LANE, (hd + 1) * LANE)
        q_ref[0, :, sl] = ((q[:, sl] * cm + qr[:, sl] * sm) * (MLA_SCALE * LOG2E)).astype(BF16)

    ckv = _rms(proj(PX_KV, MLA_KV_RANK), gkv_ref[...]).astype(BF16)
    kv = _dot(ckv, wukv_ref[...])
    kr = proj(PX_KR, LANE) * cm + proj(PX_KRR, LANE) * sm
    for hd in range(MLA_HEADS):
        sl = slice(hd * LANE, (hd + 1) * LANE)
        kvh = kv[:, sl]
        k_ref[0, :, sl] = jnp.where(lo, kvh, kr).astype(BF16)
        v_ref[0, :, sl] = jnp.where(lo, one0, kvh).astype(BF16)

    dq = proj(PX_DQ, DIFF_WIDTH)
    dqr = proj(PX_DQR, DIFF_WIDTH)
    dk = proj(PX_DK, DIFF_WIDTH)
    dkr = proj(PX_DKR, DIFF_WIDTH)
    for hd in range(DIFF_HEADS):
        sl = slice(hd * LANE, (hd + 1) * LANE)
        qb = (dq[:, sl] * cd + dqr[:, sl] * sd) * (DIFF_SCALE * LOG2E)
        dq_ref[0, :, (2 * hd) * LANE:(2 * hd + 1) * LANE] = jnp.where(lo, qb, 0.0).astype(BF16)
        dq_ref[0, :, (2 * hd + 1) * LANE:(2 * hd + 2) * LANE] = jnp.where(lo, 0.0, qb).astype(BF16)
        dk_ref[0, :, sl] = (dk[:, sl] * cd + dkr[:, sl] * sd).astype(BF16)
    dv_ref[0] = proj(PX_DV, DIFF_WIDTH).astype(BF16)
    pg = proj(PX_G, 2 * D_MODEL)
    g_ref[0] = 1.0 / (1.0 + jnp.exp(-pg))


def _proj_ctx_kernel(x_ref, mod_ref, w_ref, gkv_ref, wukv_ref, k_ref, v_ref, dk_ref, dv_ref):
    tm = x_ref.shape[1]
    sh = mod_ref[0, :, 0:D_MODEL]
    sc = mod_ref[0, :, D_MODEL:2 * D_MODEL]
    h = (x_ref[0] * (1.0 + sc) + sh).astype(BF16)

    def proj(a, n):
        return _dot(h, w_ref[:, a:a + n])

    lane = lax.broadcasted_iota(I32, (tm, LANE), 1)
    lo = lane < MLA_NOPE
    one0 = (lane == 0).astype(F32)
    ckv = _rms(proj(CX_KV, MLA_KV_RANK), gkv_ref[...]).astype(BF16)
    kv = _dot(ckv, wukv_ref[...])
    kr = proj(CX_KR, LANE)
    for hd in range(MLA_HEADS):
        sl = slice(hd * LANE, (hd + 1) * LANE)
        kvh = kv[:, sl]
        k_ref[0, :, sl] = jnp.where(lo, kvh, kr).astype(BF16)
        v_ref[0, :, sl] = jnp.where(lo, one0, kvh).astype(BF16)
    dk_ref[0] = proj(CX_DK, DIFF_WIDTH).astype(BF16)
    dv_ref[0] = proj(CX_DV, DIFF_WIDTH).astype(BF16)


def _full(shape):
    nd = len(shape)
    return pl.BlockSpec(shape, lambda *_: (0,) * nd)


def _proj_lat(x, mod3, wx, gq, wuq, wuqr, gkv, wukv, tabs, tm):
    b, s, _ = x.shape
    row = lambda w: pl.BlockSpec((1, tm, w), lambda bi, ti: (bi, ti, 0))
    tab = pl.BlockSpec((tm, LANE), lambda bi, ti: (ti, 0))
    outs = [(MLA_W, BF16), (MLA_W, BF16), (MLA_W, BF16), (2 * DIFF_WIDTH, BF16), (DIFF_WIDTH, BF16),
            (DIFF_WIDTH, BF16), (2 * D_MODEL, F32)]
    return pl.pallas_call(
        _proj_lat_kernel,
        grid=(b, s // tm),
        in_specs=[row(D_MODEL),
                  pl.BlockSpec((1, 1, 6 * D_MODEL), lambda bi, ti: (bi, 0, 0)),
                  _full(wx.shape), _full(gq.shape), _full(wuq.shape), _full(wuqr.shape),
                  _full(gkv.shape), _full(wukv.shape), tab, tab, tab, tab],
        out_specs=[row(w) for w, _ in outs],
        out_shape=[jax.ShapeDtypeStruct((b, s, w), dt) for w, dt in outs],
        compiler_params=_cparams(("arbitrary", "arbitrary")),
        name="proj_lat",
    )(x, mod3, wx, gq, wuq, wuqr, gkv, wukv, *tabs)


def _proj_ctx(ctx, mod3, wc, gkv, wukv):
    b, s, _ = ctx.shape
    row = lambda w: pl.BlockSpec((1, s, w), lambda bi: (bi, 0, 0))
    outs = [MLA_W, MLA_W, DIFF_WIDTH, DIFF_WIDTH]
    return pl.pallas_call(
        _proj_ctx_kernel,
        grid=(b,),
        in_specs=[row(D_MODEL),
                  pl.BlockSpec((1, 1, 6 * D_MODEL), lambda bi: (b, 0, 0)),
                  _full(wc.shape), _full(gkv.shape), _full(wukv.shape)],
        out_specs=[row(w) for w in outs],
        out_shape=[jax.ShapeDtypeStruct((b, s, w), BF16) for w in outs],
        compiler_params=_cparams(("arbitrary",)),
        name="proj_ctx",
    )(ctx, mod3, wc, gkv, wukv)


def _flash_t(q_t, kc_ref, vc_ref, kl_ref, vl_ref, s_ref, p_ref, a_ref, m_ref, l_ref, acc_ref, tk):
    n = kl_ref.shape[1] // tk

    def chunk(ref, i):
        return ref[0, pl.ds(pl.multiple_of(i * tk, tk), tk), :]

    def scores(i):
        s_ref[i % 2] = _dot(chunk(kl_ref, i), q_t)

    def softmax(i):
        s_t = s_ref[i % 2]
        m_i = m_ref[...]
        m_new = jnp.maximum(m_i, jnp.max(s_t, axis=0, keepdims=True))
        p = jnp.exp2(s_t - m_new)
        alpha = jnp.exp2(m_i - m_new)
        p_ref[i % 2] = p.astype(BF16)
        a_ref[i % 2] = alpha
        m_ref[...] = m_new
        if l_ref is not None:
            l_ref[...] = alpha * l_ref[...] + jnp.sum(p, axis=0, keepdims=True)

    def values(i):
        acc_ref[...] = a_ref[i % 2] * acc_ref[...] + _dot_tn(chunk(vl_ref, i), p_ref[i % 2])

    s_t = _dot(kc_ref[0], q_t)
    m0 = jnp.max(s_t, axis=0, keepdims=True)
    p = jnp.exp2(s_t - m0)
    m_ref[...] = m0
    if l_ref is not None:
        l_ref[...] = jnp.sum(p, axis=0, keepdims=True)
    acc_ref[...] = _dot_tn(vc_ref[0], p.astype(BF16))

    scores(0)
    softmax(0)
    scores(1)

    def body(i, carry):
        values(i - 2)
        softmax(i - 1)
        scores(i)
        return carry

    lax.fori_loop(2, n, body, 0, unroll=2)
    values(n - 2)
    softmax(n - 1)
    values(n - 1)


def _attn_scratch(tk, nq, with_l):
    v = pltpu.VMEM
    shapes = [v((2, tk, nq), F32), v((2, tk, nq), BF16), v((2, 1, nq), F32), v((1, nq), F32)]
    if with_l:
        shapes.append(v((1, nq), F32))
    return shapes + [v((LANE, nq), F32)]


def _mla_attn_kernel(q_ref, kc_ref, vc_ref, kl_ref, vl_ref, o_ref, s_ref, p_ref, a_ref, m_ref, acc_ref, *, tk):
    _flash_t(q_ref[0].T, kc_ref, vc_ref, kl_ref, vl_ref, s_ref, p_ref, a_ref, m_ref, None, acc_ref, tk)
    acc = acc_ref[...]
    o_ref[0] = (acc / acc[0:1, :]).T.astype(BF16)


def _diff_attn_kernel(lam_ref, g_ref, q1_ref, q2_ref, kc_ref, vc_ref, kl_ref, vl_ref, o_ref,
                      s_ref, p_ref, a_ref, m_ref, l_ref, acc_ref, *, tk):
    tq = q1_ref.shape[1]
    q_t = jnp.concatenate([q1_ref[0].T, q2_ref[0].T], axis=1)
    _flash_t(q_t, kc_ref, vc_ref, kl_ref, vl_ref, s_ref, p_ref, a_ref, m_ref, l_ref, acc_ref, tk)
    o = acc_ref[...] / l_ref[...]
    dl = lam_ref[...]
    lam = (jnp.exp(jnp.sum(dl[0:1] * dl[1:2], axis=-1, keepdims=True))
           - jnp.exp(jnp.sum(dl[2:3] * dl[3:4], axis=-1, keepdims=True)) + LAMBDA_INIT)
    w = o[:, :tq] - lam * o[:, tq:]
    w = w * lax.rsqrt(jnp.mean(w * w, axis=0, keepdims=True) + EPS) * g_ref[...] * (1.0 - LAMBDA_INIT)
    o_ref[0] = w.T.astype(BF16)


def _mla_attn(q, kc, vc, kl, vl, tq, tk):
    b, s, _ = q.shape
    sc = kc.shape[1]
    assert sc == tk and (s // tk) >= 2
    qs = pl.BlockSpec((1, tq, LANE), lambda bi, hi, qi: (bi, qi, hi))
    cs = pl.BlockSpec((1, sc, LANE), lambda bi, hi, qi: (bi, 0, hi))
    ls = pl.BlockSpec((1, s, LANE), lambda bi, hi, qi: (bi, 0, hi))
    return pl.pallas_call(
        functools.partial(_mla_attn_kernel, tk=tk),
        grid=(b, MLA_HEADS, s // tq),
        in_specs=[qs, cs, cs, ls, ls],
        out_specs=qs,
        out_shape=jax.ShapeDtypeStruct((b, s, MLA_W), BF16),
        scratch_shapes=_attn_scratch(tk, tq, False),
        compiler_params=_cparams(("arbitrary", "arbitrary", "arbitrary")),
        name="mla_attn",
    )(q, kc, vc, kl, vl)


def _diff_attn(lam4, g_subln, dq, kc, vc, kl, vl, tq, tk):
    b, s, _ = dq.shape
    sc = kc.shape[1]
    q1 = pl.BlockSpec((1, tq, LANE), lambda bi, hi, qi: (bi, qi, 2 * hi))
    q2 = pl.BlockSpec((1, tq, LANE), lambda bi, hi, qi: (bi, qi, 2 * hi + 1))
    cs = pl.BlockSpec((1, sc, LANE), lambda bi, hi, qi: (bi, 0, hi))
    ls = pl.BlockSpec((1, s, LANE), lambda bi, hi, qi: (bi, 0, hi))
    return pl.pallas_call(
        functools.partial(_diff_attn_kernel, tk=tk),
        grid=(b, DIFF_HEADS, s // tq),
        in_specs=[_full(lam4.shape), _full(g_subln.shape), q1, q2, cs, cs, ls, ls],
        out_specs=pl.BlockSpec((1, tq, LANE), lambda bi, hi, qi: (bi, qi, hi)),
        out_shape=jax.ShapeDtypeStruct((b, s, DIFF_WIDTH), BF16),
        scratch_shapes=_attn_scratch(tk, 2 * tq, True),
        compiler_params=_cparams(("arbitrary", "arbitrary", "arbitrary")),
        name="diff_attn",
    )(lam4, g_subln, dq, dq, kc, vc, kl, vl)


def _mix_kernel(om_ref, od_ref, g_ref, x_ref, mod_ref, wom_ref, wod_ref, wout_ref, lng_ref, lnb_ref,
                wrh_ref, wrl_ref, x1_ref, h2_ref, lg_ref):
    gm = g_ref[0, :, 0:D_MODEL]
    gd = g_ref[0, :, D_MODEL:2 * D_MODEL]
    y = gm * _dot(om_ref[0], wom_ref[...]) + gd * _dot(od_ref[0], wod_ref[...])
    z = _dot(y.astype(BF16), wout_ref[...])
    g1 = mod_ref[0, :, 2 * D_MODEL:3 * D_MODEL]
    sh2 = mod_ref[0, :, 3 * D_MODEL:4 * D_MODEL]
    sc2 = mod_ref[0, :, 4 * D_MODEL:5 * D_MODEL]
    x1 = _layer_norm(DEEPNORM_ALPHA * x_ref[0] + g1 * z, lng_ref[...], lnb_ref[...])
    x1_ref[0] = x1
    h2 = x1 * (1.0 + sc2) + sh2
    hi = h2.astype(BF16)
    lo = (h2 - hi.astype(F32)).astype(BF16)
    h2_ref[0] = hi
    lg_ref[0] = _dot_nt(wrh_ref[...], hi) + (_dot_nt(wrh_ref[...], lo) + _dot_nt(wrl_ref[...], hi))


def _mix(om, od, gates, x, mod3, wom, wod, wout, lng, lnb, wrh, wrl, tm):
    b, s, _ = x.shape
    row = lambda w: pl.BlockSpec((1, tm, w), lambda bi, ti: (bi, ti, 0))
    return pl.pallas_call(
        _mix_kernel,
        grid=(b, s // tm),
        in_specs=[row(MLA_W), row(DIFF_WIDTH), row(2 * D_MODEL), row(D_MODEL),
                  pl.BlockSpec((1, 1, 6 * D_MODEL), lambda bi, ti: (bi, 0, 0)),
                  _full(wom.shape), _full(wod.shape), _full(wout.shape), _full(lng.shape), _full(lnb.shape),
                  _full(wrh.shape), _full(wrl.shape)],
        out_specs=[row(D_MODEL), row(D_MODEL),
                   pl.BlockSpec((1, N_EXPERTS, tm), lambda bi, ti: (bi, 0, ti))],
        out_shape=[jax.ShapeDtypeStruct((b, s, D_MODEL), F32), jax.ShapeDtypeStruct((b, s, D_MODEL), BF16),
                   jax.ShapeDtypeStruct((b, N_EXPERTS, s), F32)],
        compiler_params=_cparams(("arbitrary", "arbitrary")),
        name="mix",
    )(om, od, gates, x, mod3, wom, wod, wout, lng, lnb, wrh, wrl)


def _route_kernel(lg_ref, br_ref, pos_ref, gate_ref, st_ref, *, cap, tile):
    e, s = lg_ref.shape[1], lg_ref.shape[2]
    lg = lg_ref[0] + br_ref[...]
    ex = jnp.exp(lg - jnp.max(lg, axis=0, keepdims=True))
    aff = ex / jnp.sum(ex, axis=0, keepdims=True)
    gate_ref[0] = aff
    bits = pltpu.bitcast(aff, I32)

    def count(mask):
        return jnp.sum(mask.astype(I32), axis=1, keepdims=True)

    def search(i, thr):
        cand = thr | (jnp.int32(1) << (30 - i))
        return jnp.where(count(bits >= cand) >= cap, cand, thr)

    thr = lax.fori_loop(0, 31, search, jnp.zeros((e, 1), I32))
    gt = bits > thr
    eq = bits == thr
    need = cap - count(gt)

    r = lax.broadcasted_iota(I32, (LANE, LANE), 0)
    c = lax.broadcasted_iota(I32, (LANE, LANE), 1)
    tri = (r < c).astype(BF16)
    lane = lax.broadcasted_iota(I32, (e, LANE), 1)

    def prefix(mask_fn, emit):
        carry = jnp.zeros((e, 1), F32)
        for j in range(s // LANE):
            m = mask_fn(j)
            emit(j, carry + _dot(m.astype(BF16), tri), m, carry)
            carry = carry + jnp.sum(m.astype(F32), axis=1, keepdims=True)

    def eq_blk(j):
        return eq[:, j * LANE:(j + 1) * LANE]

    def emit_sel(j, rank, m, carry):
        sel = gt[:, j * LANE:(j + 1) * LANE] | (m & (rank < need.astype(F32)))
        pos_ref[0, :, j * LANE:(j + 1) * LANE] = sel.astype(I32)

    prefix(eq_blk, emit_sel)

    starts = [jnp.zeros((e, LANE), I32)]

    def sel_blk(j):
        return pos_ref[0, :, j * LANE:(j + 1) * LANE] > 0

    def emit_pos(j, rank, m, carry):
        if (j * LANE) % tile == 0:
            starts[0] = jnp.where(lane == (j * LANE) // tile, carry.astype(I32), starts[0])
        pos_ref[0, :, j * LANE:(j + 1) * LANE] = jnp.where(m, rank.astype(I32), -1)

    prefix(sel_blk, emit_pos)
    st_ref[0] = jnp.where(lane == s // tile, cap, starts[0])


def _route(lgt, b_router, cap, tile):
    b, e, s = lgt.shape
    blk = pl.BlockSpec((1, e, s), lambda bi: (bi, 0, 0))
    return pl.pallas_call(
        functools.partial(_route_kernel, cap=cap, tile=tile),
        grid=(b,),
        in_specs=[blk, _full((e, 1))],
        out_specs=[blk, blk, pl.BlockSpec((1, e, LANE), lambda bi: (bi, 0, 0))],
        out_shape=[jax.ShapeDtypeStruct((b, e, s), I32), jax.ShapeDtypeStruct((b, e, s), F32),
                   jax.ShapeDtypeStruct((b, e, LANE), I32)],
        compiler_params=_cparams(("arbitrary",)),
        name="route",
    )(lgt, b_router.reshape(e, 1))


def _ffn_kernel(st_ref, h_ref, pos_ref, gate_ref, w1_ref, w3_ref, w2_ref, ye_ref, xe_ref, gs_ref, *, chunk):
    bi, ei, ti = pl.program_id(0), pl.program_id(1), pl.program_id(2)
    cap, tile = xe_ref.shape[0], h_ref.shape[1]
    s0 = st_ref[bi, ei, ti]
    s1 = st_ref[bi, ei, ti + 1]

    @pl.when(ti == 0)
    def _():
        xe_ref[...] = jnp.zeros_like(xe_ref)
        gs_ref[...] = jnp.zeros_like(gs_ref)

    pos = pos_ref[0, 0]
    gate = gate_ref[0, 0]
    slot = lax.broadcasted_iota(I32, (chunk, tile), 0)
    for c in range(cap // chunk):
        @pl.when((s0 < (c + 1) * chunk) & (s1 > c * chunk))
        def _():
            hit = slot == (pos - c * chunk)
            rows = slice(c * chunk, (c + 1) * chunk)
            xe_ref[rows, :] += _dot(hit.astype(BF16), h_ref[0])
            gs_ref[rows, :] += jnp.sum(jnp.where(hit, gate, 0.0), axis=1, keepdims=True)

    @pl.when(ti == pl.num_programs(2) - 1)
    def _():
        xe = xe_ref[...].astype(BF16)
        a = _dot(xe, w1_ref[0])
        hid = ((a / (1.0 + jnp.exp(-a))) * _dot(xe, w3_ref[0])).astype(BF16)
        ye_ref[0, 0] = (_dot(hid, w2_ref[0]) * gs_ref[...]).astype(BF16)


def _ffn(starts, h2, pos4, gate4, w1, w3, w2, cap, tile, chunk):
    b, s, d = h2.shape
    e = w1.shape[0]
    lane_blk = pl.BlockSpec((1, 1, 1, tile), lambda bi, ei, ti, st: (bi, ei, 0, ti))
    wspec = lambda shp: pl.BlockSpec((1,) + shp, lambda bi, ei, ti, st: (ei, 0, 0))
    gs = pltpu.PrefetchScalarGridSpec(
        num_scalar_prefetch=1,
        grid=(b, e, s // tile),
        in_specs=[pl.BlockSpec((1, tile, d), lambda bi, ei, ti, st: (bi, ti, 0)),
                  lane_blk, lane_blk, wspec(w1.shape[1:]), wspec(w3.shape[1:]), wspec(w2.shape[1:])],
        out_specs=pl.BlockSpec((1, 1, cap, d), lambda bi, ei, ti, st: (bi, ei, 0, 0)),
        scratch_shapes=[pltpu.VMEM((cap, d), F32), pltpu.VMEM((cap, 1), F32)],
    )
    return pl.pallas_call(
        functools.partial(_ffn_kernel, chunk=chunk),
        grid_spec=gs,
        out_shape=jax.ShapeDtypeStruct((b, e, cap, d), BF16),
        compiler_params=_cparams(("arbitrary", "arbitrary", "arbitrary")),
        name="ffn",
    )(starts, h2, pos4, gate4, w1, w3, w2)


def _combine_kernel(st_ref, ye_ref, pos_ref, x1_ref, mod_ref, lng_ref, lnb_ref, o_ref, acc_ref, *, chunk):
    bi, ti, ei = pl.program_id(0), pl.program_id(1), pl.program_id(2)
    cap, tile = ye_ref.shape[2], x1_ref.shape[1]
    s0 = st_ref[bi, ei, ti]
    s1 = st_ref[bi, ei, ti + 1]

    @pl.when(ei == 0)
    def _():
        acc_ref[...] = jnp.zeros_like(acc_ref)

    pos = pos_ref[0, 0]
    slot = lax.broadcasted_iota(I32, (chunk, tile), 0)
    for c in range(cap // chunk):
        @pl.when((s0 < (c + 1) * chunk) & (s1 > c * chunk))
        def _():
            hit = (slot == (pos - c * chunk)).astype(BF16)
            acc_ref[...] += _dot_tn(hit, ye_ref[0, 0, c * chunk:(c + 1) * chunk, :])

    @pl.when(ei == pl.num_programs(2) - 1)
    def _():
        g2 = mod_ref[0, :, 5 * D_MODEL:6 * D_MODEL]
        o_ref[0] = _layer_norm(DEEPNORM_ALPHA * x1_ref[0] + g2 * acc_ref[...], lng_ref[...], lnb_ref[...])


def _combine(starts, ye, pos4, x1, mod3, lng, lnb, tile, chunk):
    b, s, d = x1.shape
    e, cap = ye.shape[1], ye.shape[2]
    gs = pltpu.PrefetchScalarGridSpec(
        num_scalar_prefetch=1,
        grid=(b, s // tile, e),
        in_specs=[pl.BlockSpec((1, 1, cap, d), lambda bi, ti, ei, st: (bi, ei, 0, 0)),
                  pl.BlockSpec((1, 1, 1, tile), lambda bi, ti, ei, st: (bi, ei, 0, ti)),
                  pl.BlockSpec((1, tile, d), lambda bi, ti, ei, st: (bi, ti, 0)),
                  pl.BlockSpec((1, 1, 6 * D_MODEL), lambda bi, ti, ei, st: (bi, 0, 0)),
                  pl.BlockSpec((1, d), lambda bi, ti, ei, st: (0, 0)),
                  pl.BlockSpec((1, d), lambda bi, ti, ei, st: (0, 0))],
        out_specs=pl.BlockSpec((1, tile, d), lambda bi, ti, ei, st: (bi, ti, 0)),
        scratch_shapes=[pltpu.VMEM((tile, d), F32)],
    )
    return pl.pallas_call(
        functools.partial(_combine_kernel, chunk=chunk),
        grid_spec=gs,
        out_shape=jax.ShapeDtypeStruct((b, s, d), F32),
        compiler_params=_cparams(("arbitrary", "arbitrary", "arbitrary")),
        name="combine",
    )(starts, ye, pos4, x1, mod3, lng, lnb)


def _head_pad(w, n_heads, width, offset=0):
    k = w.shape[0]
    w3 = w.reshape(k, n_heads, width)
    out = jnp.zeros((k, n_heads, LANE), w.dtype).at[:, :, offset:offset + width].set(w3)
    return out.reshape(k, n_heads * LANE)


def _prep_weights(w_in, w_uq, w_o_mla):
    kr = w_in[:, MLA_KR_OFF:DIFF_Q_OFF]
    dqw = w_in[:, DIFF_Q_OFF:DIFF_K_OFF]
    dkw = w_in[:, DIFF_K_OFF:DIFF_V_OFF]
    dvw = w_in[:, DIFF_V_OFF:GATE_OFF]
    k = w_in.shape[0]

    def rot_groups(w):
        return _rot_cols(w.reshape(k, -1, DIFF_HD)).reshape(k, -1)

    kr_pad = _head_pad(kr, 1, MLA_ROPE, MLA_NOPE)
    krr_pad = _head_pad(_rot_cols(kr), 1, MLA_ROPE, MLA_NOPE)
    wx = jnp.concatenate([w_in[:, MLA_Q_OFF:MLA_KR_OFF], kr_pad, krr_pad, dqw, rot_groups(dqw),
                          dkw, rot_groups(dkw), dvw, w_in[:, GATE_OFF:]], axis=1).astype(BF16)
    wc = jnp.concatenate([w_in[:, MLA_KV_OFF:MLA_KR_OFF], kr_pad, dkw, dvw], axis=1).astype(BF16)
    hd = MLA_NOPE + MLA_ROPE
    uq3 = w_uq.reshape(w_uq.shape[0], MLA_HEADS, hd)
    uq_rot = jnp.concatenate([jnp.zeros_like(uq3[..., :MLA_NOPE]), _rot_cols(uq3[..., MLA_NOPE:])], -1)
    wuq = _head_pad(w_uq, MLA_HEADS, hd).astype(BF16)
    wuqr = _head_pad(uq_rot.reshape(w_uq.shape[0], -1), MLA_HEADS, hd).astype(BF16)
    wom = jnp.zeros((MLA_HEADS, LANE, D_MODEL), w_o_mla.dtype).at[:, MLA_NOPE:, :].set(
        w_o_mla.reshape(MLA_HEADS, MLA_V, D_MODEL)).reshape(MLA_W, D_MODEL).astype(BF16)
    return wx, wc, wuq, wuqr, wom


def kernel(x, c, ctx, c_ctx, w_ada, b_ada, w_in, mla_g_q, mla_w_uq, mla_g_kv, mla_w_ukv, mla_w_o, diff_lambda,
           diff_g_subln, diff_w_o, w_out, ln1_g, ln1_b, moe_w_router, moe_b_router, moe_w1, moe_w3, moe_w2,
           ln2_g, ln2_b):
    b, s, d = x.shape
    assert d == D_MODEL and w_ada.shape[0] == DEPTH == 1 and b <= 4
    cap = EC_CAPACITY * s // N_EXPERTS
    tile = min(1024, s)
    chunk = min(128, cap)
    tm = min(512, s)
    tq = min(512, s)
    tk = ctx.shape[1]

    cc = jnp.zeros((8, d), F32).at[:b].set(c).at[b].set(c_ctx)
    mod3 = _ada(cc, w_ada[0], b_ada[0]).reshape(8, 1, 6 * d)

    wx, wc, wuq, wuqr, wom = _prep_weights(w_in[0], mla_w_uq[0], mla_w_o[0])
    gq = mla_g_q[0].reshape(1, -1)
    gkv = mla_g_kv[0].reshape(1, -1)
    wukv = mla_w_ukv[0].astype(BF16)
    tabs = _rope_tables(s)

    q, k, v, dq, dk, dv, gates = _proj_lat(x, mod3, wx, gq, wuq, wuqr, gkv, wukv, tabs, tm)
    kc, vc, dkc, dvc = _proj_ctx(ctx, mod3, wc, gkv, wukv)

    o_mla = _mla_attn(q, kc, vc, k, v, tq, tk)
    o_diff = _diff_attn(diff_lambda[0], diff_g_subln[0].reshape(-1, 1), dq, dkc, dvc, dk, dv, tq // 2, tk)

    wr = moe_w_router[0].T
    wrh = wr.astype(BF16)
    wrl = (wr - wrh.astype(F32)).astype(BF16)
    x1, h2, lgt = _mix(o_mla, o_diff, gates, x, mod3, wom, diff_w_o[0].astype(BF16), w_out[0].astype(BF16),
                       ln1_g[0].reshape(1, -1), ln1_b[0].reshape(1, -1), wrh, wrl, tm)

    pos, gate, starts = _route(lgt, moe_b_router[0], cap, tile)
    pos4 = pos.reshape(b, N_EXPERTS, 1, s)
    gate4 = gate.reshape(b, N_EXPERTS, 1, s)
    ye = _ffn(starts, h2, pos4, gate4, moe_w1[0].astype(BF16), moe_w3[0].astype(BF16), moe_w2[0].astype(BF16),
              cap, tile, chunk)
    return _combine(starts, ye, pos4, x1, mod3, ln2_g[0].reshape(1, -1), ln2_b[0].reshape(1, -1), tile, chunk)
```

```python
import functools
import math

import numpy as np
import jax
import jax.numpy as jnp
from jax import lax
from jax.experimental import pallas as pl
from jax.experimental.pallas import tpu as pltpu

F32 = jnp.float32
BF16 = jnp.bfloat16
I32 = jnp.int32

D_MODEL = 1024
DEPTH = 1
GRID_W = 64
ROPE_THETA = 10000.0
EPS = 1e-6
MLA_HEADS = 8
MLA_Q_RANK = 256
MLA_KV_RANK = 128
MLA_NOPE = 64
MLA_ROPE = 32
MLA_V = 64
DIFF_HEADS = 4
DIFF_HD = 64
N_EXPERTS = 16
EXPERT_FF = 1024
EC_CAPACITY = 2

MLA_Q_OFF = 0
MLA_KV_OFF = MLA_Q_OFF + MLA_Q_RANK
MLA_KR_OFF = MLA_KV_OFF + MLA_KV_RANK
DIFF_Q_OFF = MLA_KR_OFF + MLA_ROPE
DIFF_K_OFF = DIFF_Q_OFF + DIFF_HEADS * 2 * DIFF_HD
DIFF_V_OFF = DIFF_K_OFF + DIFF_HEADS * 2 * DIFF_HD
GATE_OFF = DIFF_V_OFF + DIFF_HEADS * 2 * DIFF_HD
N_IN = GATE_OFF + 2 * D_MODEL
DIFF_WIDTH = DIFF_HEADS * 2 * DIFF_HD
MLA_SCALE = (MLA_NOPE + MLA_ROPE) ** -0.5
DIFF_SCALE = DIFF_HD ** -0.5
DEEPNORM_ALPHA = (2 * DEPTH) ** 0.25
LAMBDA_INIT = 0.8 - 0.6 * math.exp(-0.3 * 0)
LOG2E = 1.4426950408889634
LAGGED_MAX_HEADROOM = 100.0

LANE = 128
MLA_W = MLA_HEADS * LANE
PX_Q = 0
PX_KV = PX_Q + MLA_Q_RANK
PX_KR = PX_KV + MLA_KV_RANK
PX_KRR = PX_KR + LANE
PX_DQ = PX_KRR + LANE
PX_DQR = PX_DQ + DIFF_WIDTH
PX_DK = PX_DQR + DIFF_WIDTH
PX_DKR = PX_DK + DIFF_WIDTH
PX_DV = PX_DKR + DIFF_WIDTH
PX_G = PX_DV + DIFF_WIDTH
PX_N = PX_G + 2 * D_MODEL
CX_KV = 0
CX_KR = CX_KV + MLA_KV_RANK
CX_DK = CX_KR + LANE
CX_DV = CX_DK + DIFF_WIDTH
CX_N = CX_DV + DIFF_WIDTH

VMEM_LIMIT = 56 * 1024 * 1024


def _cparams(sem):
    return pltpu.CompilerParams(dimension_semantics=sem, vmem_limit_bytes=VMEM_LIMIT)


def _rot_cols(w):
    d = w.shape[-1]
    h = d // 2
    q = h // 2
    parts = []
    for s in (0, h):
        parts += [-w[..., s + q:s + h], w[..., s:s + q]]
    return jnp.concatenate(parts, axis=-1)


def _rope_tables(seq):
    t = np.arange(seq)
    row = (t // GRID_W).astype(np.float64)
    col = (t % GRID_W).astype(np.float64)

    def axial(d):
        h = d // 2
        inv = ROPE_THETA ** (-(np.arange(h // 2, dtype=np.float64) * 2.0 / h))
        ar, ac = row[:, None] * inv, col[:, None] * inv
        cos = np.concatenate([np.cos(ar), np.cos(ar), np.cos(ac), np.cos(ac)], -1)
        sin = np.concatenate([np.sin(ar), np.sin(ar), np.sin(ac), np.sin(ac)], -1)
        return cos, sin

    c32, s32 = axial(MLA_ROPE)
    c64, s64 = axial(DIFF_HD)
    cos_m = np.ones((seq, LANE))
    sin_m = np.zeros((seq, LANE))
    cos_m[:, MLA_NOPE:MLA_NOPE + MLA_ROPE] = c32
    sin_m[:, MLA_NOPE:MLA_NOPE + MLA_ROPE] = s32
    cos_d = np.concatenate([c64, c64], -1)
    sin_d = np.concatenate([s64, s64], -1)
    return tuple(jnp.asarray(a, F32) for a in (cos_m, sin_m, cos_d, sin_d))


def _rms(x, g):
    return x * lax.rsqrt(jnp.mean(x * x, axis=-1, keepdims=True) + EPS) * g


def _layer_norm(x, g, b):
    mu = jnp.mean(x, axis=-1, keepdims=True)
    xc = x - mu
    var = jnp.mean(xc * xc, axis=-1, keepdims=True)
    return xc * lax.rsqrt(var + EPS) * g + b


def _dot(a, b):
    return jnp.dot(a, b, preferred_element_type=F32)


def _dot_nt(a, b):
    return lax.dot_general(a, b, (((1,), (1,)), ((), ())), preferred_element_type=F32)


def _dot_tn(a, b):
    return lax.dot_general(a, b, (((0,), (0,)), ((), ())), preferred_element_type=F32)


def _ada_kernel(c_ref, w_ref, b_ref, o_ref):
    c = c_ref[...]
    s = (c / (1.0 + jnp.exp(-c))).astype(BF16)
    o_ref[...] = _dot(s, w_ref[...].astype(BF16)) + b_ref[...]


def _ada(cc, w_ada, b_ada):
    n = w_ada.shape[1]
    tn = 1024
    return pl.pallas_call(
        _ada_kernel,
        grid=(n // tn,),
        in_specs=[pl.BlockSpec((8, D_MODEL), lambda j: (0, 0)),
                  pl.BlockSpec((D_MODEL, tn), lambda j: (0, j)),
                  pl.BlockSpec((1, tn), lambda j: (0, j))],
        out_specs=pl.BlockSpec((8, tn), lambda j: (0, j)),
        out_shape=jax.ShapeDtypeStruct((8, n), F32),
        compiler_params=_cparams(("arbitrary",)),
        name="ada",
    )(cc, w_ada, b_ada.reshape(1, n))


def _proj_lat_kernel(x_ref, mod_ref, w_ref, gq_ref, wuq_ref, wuqr_ref, gkv_ref, wukv_ref,
                     cm_ref, sm_ref, cd_ref, sd_ref,
                     q_ref, k_ref, v_ref, dq_ref, dk_ref, dv_ref, g_ref):
    tm = x_ref.shape[1]
    sh = mod_ref[0, :, 0:D_MODEL]
    sc = mod_ref[0, :, D_MODEL:2 * D_MODEL]
    h = (x_ref[0] * (1.0 + sc) + sh).astype(BF16)

    def proj(a, n):
        return _dot(h, w_ref[:, a:a + n])

    lane = lax.broadcasted_iota(I32, (tm, LANE), 1)
    lo = lane < MLA_NOPE
    one0 = (lane == 0).astype(F32)
    cm, sm, cd, sd = cm_ref[...], sm_ref[...], cd_ref[...], sd_ref[...]

    cq = _rms(proj(PX_Q, MLA_Q_RANK), gq_ref[...]).astype(BF16)
    q = _dot(cq, wuq_ref[...])
    qr = _dot(cq, wuqr_ref[...])
    for hd in range(MLA_HEADS):
        sl = slice(hd * LANE, (hd + 1) * LANE)
        q_ref[0, :, sl] = ((q[:, sl] * cm + qr[:, sl] * sm) * (MLA_SCALE * LOG2E)).astype(BF16)

    ckv = _rms(proj(PX_KV, MLA_KV_RANK), gkv_ref[...]).astype(BF16)
    kv = _dot(ckv, wukv_ref[...])
    kr = proj(PX_KR, LANE) * cm + proj(PX_KRR, LANE) * sm
    for hd in range(MLA_HEADS):
        sl = slice(hd * LANE, (hd + 1) * LANE)
        kvh = kv[:, sl]
        k_ref[0, :, sl] = jnp.where(lo, kvh, kr).astype(BF16)
        v_ref[0, :, sl] = jnp.where(lo, one0, kvh).astype(BF16)

    dq = proj(PX_DQ, DIFF_WIDTH)
    dqr = proj(PX_DQR, DIFF_WIDTH)
    dk = proj(PX_DK, DIFF_WIDTH)
    dkr = proj(PX_DKR, DIFF_WIDTH)
    for hd in range(DIFF_HEADS):
        sl = slice(hd * LANE, (hd + 1) * LANE)
        qb = (dq[:, sl] * cd + dqr[:, sl] * sd) * (DIFF_SCALE * LOG2E)
        dq_ref[0, :, (2 * hd) * LANE:(2 * hd + 1) * LANE] = jnp.where(lo, qb, 0.0).astype(BF16)
        dq_ref[0, :, (2 * hd + 1) * LANE:(2 * hd + 2) * LANE] = jnp.where(lo, 0.0, qb).astype(BF16)
        dk_ref[0, :, sl] = (dk[:, sl] * cd + dkr[:, sl] * sd).astype(BF16)
    dv_ref[0] = proj(PX_DV, DIFF_WIDTH).astype(BF16)
    pg = proj(PX_G, 2 * D_MODEL)
    g_ref[0] = 1.0 / (1.0 + jnp.exp(-pg))


def _proj_ctx_kernel(x_ref, mod_ref, w_ref, gkv_ref, wukv_ref, k_ref, v_ref, dk_ref, dv_ref):
    tm = x_ref.shape[1]
    sh = mod_ref[0, :, 0:D_MODEL]
    sc = mod_ref[0, :, D_MODEL:2 * D_MODEL]
    h = (x_ref[0] * (1.0 + sc) + sh).astype(BF16)

    def proj(a, n):
        return _dot(h, w_ref[:, a:a + n])

    lane = lax.broadcasted_iota(I32, (tm, LANE), 1)
    lo = lane < MLA_NOPE
    one0 = (lane == 0).astype(F32)
    ckv = _rms(proj(CX_KV, MLA_KV_RANK), gkv_ref[...]).astype(BF16)
    kv = _dot(ckv, wukv_ref[...])
    kr = proj(CX_KR, LANE)
    for hd in range(MLA_HEADS):
        sl = slice(hd * LANE, (hd + 1) * LANE)
        kvh = kv[:, sl]
        k_ref[0, :, sl] = jnp.where(lo, kvh, kr).astype(BF16)
        v_ref[0, :, sl] = jnp.where(lo, one0, kvh).astype(BF16)
    dk_ref[0] = proj(CX_DK, DIFF_WIDTH).astype(BF16)
    dv_ref[0] = proj(CX_DV, DIFF_WIDTH).astype(BF16)


def _full(shape):
    nd = len(shape)
    return pl.BlockSpec(shape, lambda *_: (0,) * nd)


def _proj_lat(x, mod3, wx, gq, wuq, wuqr, gkv, wukv, tabs, tm):
    b, s, _ = x.shape
    row = lambda w: pl.BlockSpec((1, tm, w), lambda bi, ti: (bi, ti, 0))
    tab = pl.BlockSpec((tm, LANE), lambda bi, ti: (ti, 0))
    outs = [(MLA_W, BF16), (MLA_W, BF16), (MLA_W, BF16), (2 * DIFF_WIDTH, BF16), (DIFF_WIDTH, BF16),
            (DIFF_WIDTH, BF16), (2 * D_MODEL, F32)]
    return pl.pallas_call(
        _proj_lat_kernel,
        grid=(b, s // tm),
        in_specs=[row(D_MODEL),
                  pl.BlockSpec((1, 1, 6 * D_MODEL), lambda bi, ti: (bi, 0, 0)),
                  _full(wx.shape), _full(gq.shape), _full(wuq.shape), _full(wuqr.shape),
                  _full(gkv.shape), _full(wukv.shape), tab, tab, tab, tab],
        out_specs=[row(w) for w, _ in outs],
        out_shape=[jax.ShapeDtypeStruct((b, s, w), dt) for w, dt in outs],
        compiler_params=_cparams(("arbitrary", "arbitrary")),
        name="proj_lat",
    )(x, mod3, wx, gq, wuq, wuqr, gkv, wukv, *tabs)


def _proj_ctx(ctx, mod3, wc, gkv, wukv):
    b, s, _ = ctx.shape
    row = lambda w: pl.BlockSpec((1, s, w), lambda bi: (bi, 0, 0))
    outs = [MLA_W, MLA_W, DIFF_WIDTH, DIFF_WIDTH]
    return pl.pallas_call(
        _proj_ctx_kernel,
        grid=(b,),
        in_specs=[row(D_MODEL),
                  pl.BlockSpec((1, 1, 6 * D_MODEL), lambda bi: (b, 0, 0)),
                  _full(wc.shape), _full(gkv.shape), _full(wukv.shape)],
        out_specs=[row(w) for w in outs],
        out_shape=[jax.ShapeDtypeStruct((b, s, w), BF16) for w in outs],
        compiler_params=_cparams(("arbitrary",)),
        name="proj_ctx",
    )(ctx, mod3, wc, gkv, wukv)


def _flash_t(q_t, kc_ref, vc_ref, kl_ref, vl_ref, p_ref, g_ref, m_ref, acc_ref, tk, ps_ref=None, l_ref=None):
    n = kl_ref.shape[1] // tk

    def chunk(ref, i):
        return ref[0, pl.ds(pl.multiple_of(i * tk, tk), tk), :]

    def probs(i, stab):
        p = jnp.exp2(_dot(chunk(kl_ref, i), q_t) - stab)
        p_ref[i % 2] = p.astype(BF16)
        if ps_ref is not None:
            ps_ref[i % 2] = jnp.sum(p, axis=0, keepdims=True)

    def scores_exp(i):
        stab = m_ref[...]
        s_t = _dot(chunk(kl_ref, i), q_t)
        p = jnp.exp2(s_t - stab)
        p_ref[i % 2] = p.astype(BF16)
        if ps_ref is not None:
            ps_ref[i % 2] = jnp.sum(p, axis=0, keepdims=True)
        cmax = jnp.max(s_t, axis=0, keepdims=True)
        m_new = jnp.maximum(stab, cmax)
        g_ref[(i + 1) % 2] = jnp.exp2(stab - m_new)
        m_ref[...] = m_new

        @pl.when(jnp.max(cmax - stab) > LAGGED_MAX_HEADROOM)
        def _():
            probs(i, m_new)
            g_ref[i % 2] = g_ref[i % 2] * jnp.exp2(stab - m_new)
            g_ref[(i + 1) % 2] = jnp.ones_like(stab)

    def values(i):
        g = g_ref[i % 2]
        acc_ref[...] = g * acc_ref[...] + _dot_tn(chunk(vl_ref, i), p_ref[i % 2])
        if l_ref is not None:
            l_ref[...] = g * l_ref[...] + ps_ref[i % 2]

    s_t = _dot(kc_ref[0], q_t)
    m0 = jnp.max(s_t, axis=0, keepdims=True)
    p = jnp.exp2(s_t - m0)
    m_ref[...] = m0
    g_ref[0] = jnp.ones_like(m0)
    if l_ref is not None:
        l_ref[...] = jnp.sum(p, axis=0, keepdims=True)
    acc_ref[...] = _dot_tn(vc_ref[0], p.astype(BF16))

    scores_exp(0)

    def body(i, carry):
        values(i - 1)
        scores_exp(i)
        return carry

    lax.fori_loop(1, n, body, 0)
    values(n - 1)


def _attn_scratch(tk, nq, with_l):
    v = pltpu.VMEM
    shapes = [v((2, tk, nq), BF16), v((2, 1, nq), F32), v((1, nq), F32), v((LANE, nq), F32)]
    if with_l:
        shapes += [v((2, 1, nq), F32), v((1, nq), F32)]
    return shapes


def _mla_attn_kernel(q_ref, kc_ref, vc_ref, kl_ref, vl_ref, o_ref, p_ref, g_ref, m_ref, acc_ref, *, tk):
    _flash_t(q_ref[0].T, kc_ref, vc_ref, kl_ref, vl_ref, p_ref, g_ref, m_ref, acc_ref, tk)
    acc = acc_ref[...]
    o_ref[0] = (acc / acc[0:1, :]).T.astype(BF16)


def _diff_attn_kernel(lam_ref, gs_ref, q1_ref, q2_ref, kc_ref, vc_ref, kl_ref, vl_ref, o_ref,
                      p_ref, g_ref, m_ref, acc_ref, ps_ref, l_ref, *, tk):
    tq = q1_ref.shape[1]
    q_t = jnp.concatenate([q1_ref[0].T, q2_ref[0].T], axis=1)
    _flash_t(q_t, kc_ref, vc_ref, kl_ref, vl_ref, p_ref, g_ref, m_ref, acc_ref, tk, ps_ref, l_ref)
    o = acc_ref[...] / l_ref[...]
    dl = lam_ref[...]
    lam = (jnp.exp(jnp.sum(dl[0:1] * dl[1:2], axis=-1, keepdims=True))
           - jnp.exp(jnp.sum(dl[2:3] * dl[3:4], axis=-1, keepdims=True)) + LAMBDA_INIT)
    w = o[:, :tq] - lam * o[:, tq:]
    w = w * lax.rsqrt(jnp.mean(w * w, axis=0, keepdims=True) + EPS) * gs_ref[...] * (1.0 - LAMBDA_INIT)
    o_ref[0] = w.T.astype(BF16)


def _mla_attn(q, kc, vc, kl, vl, tq, tk):
    b, s, _ = q.shape
    sc = kc.shape[1]
    assert s % tk == 0 and s % tq == 0
    qs = pl.BlockSpec((1, tq, LANE), lambda bi, hi, qi: (bi, qi, hi))
    cs = pl.BlockSpec((1, sc, LANE), lambda bi, hi, qi: (bi, 0, hi))
    ls = pl.BlockSpec((1, s, LANE), lambda bi, hi, qi: (bi, 0, hi))
    return pl.pallas_call(
        functools.partial(_mla_attn_kernel, tk=tk),
        grid=(b, MLA_HEADS, s // tq),
        in_specs=[qs, cs, cs, ls, ls],
        out_specs=qs,
        out_shape=jax.ShapeDtypeStruct((b, s, MLA_W), BF16),
        scratch_shapes=_attn_scratch(tk, tq, False),
        compiler_params=_cparams(("arbitrary", "arbitrary", "arbitrary")),
        name="mla_attn",
    )(q, kc, vc, kl, vl)


def _diff_attn(lam4, g_subln, dq, kc, vc, kl, vl, tq, tk):
    b, s, _ = dq.shape
    sc = kc.shape[1]
    q1 = pl.BlockSpec((1, tq, LANE), lambda bi, hi, qi: (bi, qi, 2 * hi))
    q2 = pl.BlockSpec((1, tq, LANE), lambda bi, hi, qi: (bi, qi, 2 * hi + 1))
    cs = pl.BlockSpec((1, sc, LANE), lambda bi, hi, qi: (bi, 0, hi))
    ls = pl.BlockSpec((1, s, LANE), lambda bi, hi, qi: (bi, 0, hi))
    return pl.pallas_call(
        functools.partial(_diff_attn_kernel, tk=tk),
        grid=(b, DIFF_HEADS, s // tq),
        in_specs=[_full(lam4.shape), _full(g_subln.shape), q1, q2, cs, cs, ls, ls],
        out_specs=pl.BlockSpec((1, tq, LANE), lambda bi, hi, qi: (bi, qi, hi)),
        out_shape=jax.ShapeDtypeStruct((b, s, DIFF_WIDTH), BF16),
        scratch_shapes=_attn_scratch(tk, 2 * tq, True),
        compiler_params=_cparams(("arbitrary", "arbitrary", "arbitrary")),
        name="diff_attn",
    )(lam4, g_subln, dq, dq, kc, vc, kl, vl)


def _mix_kernel(om_ref, od_ref, g_ref, x_ref, mod_ref, wom_ref, wod_ref, wout_ref, lng_ref, lnb_ref,
                wrh_ref, wrl_ref, x1_ref, h2_ref, lg_ref):
    gm = g_ref[0, :, 0:D_MODEL]
    gd = g_ref[0, :, D_MODEL:2 * D_MODEL]
    y = gm * _dot(om_ref[0], wom_ref[...]) + gd * _dot(od_ref[0], wod_ref[...])
    z = _dot(y.astype(BF16), wout_ref[...])
    g1 = mod_ref[0, :, 2 * D_MODEL:3 * D_MODEL]
    sh2 = mod_ref[0, :, 3 * D_MODEL:4 * D_MODEL]
    sc2 = mod_ref[0, :, 4 * D_MODEL:5 * D_MODEL]
    x1 = _layer_norm(DEEPNORM_ALPHA * x_ref[0] + g1 * z, lng_ref[...], lnb_ref[...])
    x1_ref[0] = x1
    h2 = x1 * (1.0 + sc2) + sh2
    hi = h2.astype(BF16)
    lo = (h2 - hi.astype(F32)).astype(BF16)
    h2_ref[0] = hi
    lg_ref[0] = _dot_nt(wrh_ref[...], hi) + (_dot_nt(wrh_ref[...], lo) + _dot_nt(wrl_ref[...], hi))


def _mix(om, od, gates, x, mod3, wom, wod, wout, lng, lnb, wrh, wrl, tm):
    b, s, _ = x.shape
    row = lambda w: pl.BlockSpec((1, tm, w), lambda bi, ti: (bi, ti, 0))
    return pl.pallas_call(
        _mix_kernel,
        grid=(b, s // tm),
        in_specs=[row(MLA_W), row(DIFF_WIDTH), row(2 * D_MODEL), row(D_MODEL),
                  pl.BlockSpec((1, 1, 6 * D_MODEL), lambda bi, ti: (bi, 0, 0)),
                  _full(wom.shape), _full(wod.shape), _full(wout.shape), _full(lng.shape), _full(lnb.shape),
                  _full(wrh.shape), _full(wrl.shape)],
        out_specs=[row(D_MODEL), row(D_MODEL),
                   pl.BlockSpec((1, N_EXPERTS, tm), lambda bi, ti: (bi, 0, ti))],
        out_shape=[jax.ShapeDtypeStruct((b, s, D_MODEL), F32), jax.ShapeDtypeStruct((b, s, D_MODEL), BF16),
                   jax.ShapeDtypeStruct((b, N_EXPERTS, s), F32)],
        compiler_params=_cparams(("arbitrary", "arbitrary")),
        name="mix",
    )(om, od, gates, x, mod3, wom, wod, wout, lng, lnb, wrh, wrl)


def _route_kernel(lg_ref, br_ref, pos_ref, gate_ref, st_ref, *, cap, tile):
    e, s = lg_ref.shape[1], lg_ref.shape[2]
    lg = lg_ref[0] + br_ref[...]
    ex = jnp.exp(lg - jnp.max(lg, axis=0, keepdims=True))
    aff = ex / jnp.sum(ex, axis=0, keepdims=True)
    gate_ref[0] = aff
    bits = pltpu.bitcast(aff, I32)

    def count(mask):
        return jnp.sum(mask.astype(I32), axis=1, keepdims=True)

    def search(i, thr):
        cand = thr | (jnp.int32(1) << (30 - i))
        return jnp.where(count(bits >= cand) >= cap, cand, thr)

    thr = lax.fori_loop(0, 31, search, jnp.zeros((e, 1), I32))
    gt = bits > thr
    eq = bits == thr
    need = cap - count(gt)

    r = lax.broadcasted_iota(I32, (LANE, LANE), 0)
    c = lax.broadcasted_iota(I32, (LANE, LANE), 1)
    tri = (r < c).astype(BF16)
    lane = lax.broadcasted_iota(I32, (e, LANE), 1)

    def prefix(mask_fn, emit):
        carry = jnp.zeros((e, 1), F32)
        for j in range(s // LANE):
            m = mask_fn(j)
            emit(j, carry + _dot(m.astype(BF16), tri), m, carry)
            carry = carry + jnp.sum(m.astype(F32), axis=1, keepdims=True)

    def eq_blk(j):
        return eq[:, j * LANE:(j + 1) * LANE]

    def emit_sel(j, rank, m, carry):
        sel = gt[:, j * LANE:(j + 1) * LANE] | (m & (rank < need.astype(F32)))
        pos_ref[0, :, j * LANE:(j + 1) * LANE] = sel.astype(I32)

    prefix(eq_blk, emit_sel)

    starts = [jnp.zeros((e, LANE), I32)]

    def sel_blk(j):
        return pos_ref[0, :, j * LANE:(j + 1) * LANE] > 0

    def emit_pos(j, rank, m, carry):
        if (j * LANE) % tile == 0:
            starts[0] = jnp.where(lane == (j * LANE) // tile, carry.astype(I32), starts[0])
        pos_ref[0, :, j * LANE:(j + 1) * LANE] = jnp.where(m, rank.astype(I32), -1)

    prefix(sel_blk, emit_pos)
    st_ref[0] = jnp.where(lane == s // tile, cap, starts[0])


def _route(lgt, b_router, cap, tile):
    b, e, s = lgt.shape
    blk = pl.BlockSpec((1, e, s), lambda bi: (bi, 0, 0))
    return pl.pallas_call(
        functools.partial(_route_kernel, cap=cap, tile=tile),
        grid=(b,),
        in_specs=[blk, _full((e, 1))],
        out_specs=[blk, blk, pl.BlockSpec((1, e, LANE), lambda bi: (bi, 0, 0))],
        out_shape=[jax.ShapeDtypeStruct((b, e, s), I32), jax.ShapeDtypeStruct((b, e, s), F32),
                   jax.ShapeDtypeStruct((b, e, LANE), I32)],
        compiler_params=_cparams(("arbitrary",)),
        name="route",
    )(lgt, b_router.reshape(e, 1))


def _ffn_kernel(st_ref, h_ref, pos_ref, gate_ref, w1_ref, w3_ref, w2_ref, ye_ref, xe_ref, gs_ref, *, chunk):
    bi, ei, ti = pl.program_id(0), pl.program_id(1), pl.program_id(2)
    cap, tile = xe_ref.shape[0], h_ref.shape[1]
    s0 = st_ref[bi, ei, ti]
    s1 = st_ref[bi, ei, ti + 1]

    @pl.when(ti == 0)
    def _():
        xe_ref[...] = jnp.zeros_like(xe_ref)
        gs_ref[...] = jnp.zeros_like(gs_ref)

    pos = pos_ref[0, 0]
    gate = gate_ref[0, 0]
    slot = lax.broadcasted_iota(I32, (chunk, tile), 0)
    for c in range(cap // chunk):
        @pl.when((s0 < (c + 1) * chunk) & (s1 > c * chunk))
        def _():
            hit = slot == (pos - c * chunk)
            rows = slice(c * chunk, (c + 1) * chunk)
            xe_ref[rows, :] += _dot(hit.astype(BF16), h_ref[0])
            gs_ref[rows, :] += jnp.sum(jnp.where(hit, gate, 0.0), axis=1, keepdims=True)

    @pl.when(ti == pl.num_programs(2) - 1)
    def _():
        xe = xe_ref[...].astype(BF16)
        a = _dot(xe, w1_ref[0])
        hid = ((a / (1.0 + jnp.exp(-a))) * _dot(xe, w3_ref[0])).astype(BF16)
        ye_ref[0, 0] = (_dot(hid, w2_ref[0]) * gs_ref[...]).astype(BF16)


def _ffn(starts, h2, pos4, gate4, w1, w3, w2, cap, tile, chunk):
    b, s, d = h2.shape
    e = w1.shape[0]
    lane_blk = pl.BlockSpec((1, 1, 1, tile), lambda bi, ei, ti, st: (bi, ei, 0, ti))
    wspec = lambda shp: pl.BlockSpec((1,) + shp, lambda bi, ei, ti, st: (ei, 0, 0))
    gs = pltpu.PrefetchScalarGridSpec(
        num_scalar_prefetch=1,
        grid=(b, e, s // tile),
        in_specs=[pl.BlockSpec((1, tile, d), lambda bi, ei, ti, st: (bi, ti, 0)),
                  lane_blk, lane_blk, wspec(w1.shape[1:]), wspec(w3.shape[1:]), wspec(w2.shape[1:])],
        out_specs=pl.BlockSpec((1, 1, cap, d), lambda bi, ei, ti, st: (bi, ei, 0, 0)),
        scratch_shapes=[pltpu.VMEM((cap, d), F32), pltpu.VMEM((cap, 1), F32)],
    )
    return pl.pallas_call(
        functools.partial(_ffn_kernel, chunk=chunk),
        grid_spec=gs,
        out_shape=jax.ShapeDtypeStruct((b, e, cap, d), BF16),
        compiler_params=_cparams(("arbitrary", "arbitrary", "arbitrary")),
        name="ffn",
    )(starts, h2, pos4, gate4, w1, w3, w2)


def _combine_kernel(st_ref, ye_ref, pos_ref, x1_ref, mod_ref, lng_ref, lnb_ref, o_ref, acc_ref, *, chunk):
    bi, ti, ei = pl.program_id(0), pl.program_id(1), pl.program_id(2)
    cap, tile = ye_ref.shape[2], x1_ref.shape[1]
    s0 = st_ref[bi, ei, ti]
    s1 = st_ref[bi, ei, ti + 1]

    @pl.when(ei == 0)
    def _():
        acc_ref[...] = jnp.zeros_like(acc_ref)

    pos = pos_ref[0, 0]
    slot = lax.broadcasted_iota(I32, (chunk, tile), 0)
    for c in range(cap // chunk):
        @pl.when((s0 < (c + 1) * chunk) & (s1 > c * chunk))
        def _():
            hit = (slot == (pos - c * chunk)).astype(BF16)
            acc_ref[...] += _dot_tn(hit, ye_ref[0, 0, c * chunk:(c + 1) * chunk, :])

    @pl.when(ei == pl.num_programs(2) - 1)
    def _():
        g2 = mod_ref[0, :, 5 * D_MODEL:6 * D_MODEL]
        o_ref[0] = _layer_norm(DEEPNORM_ALPHA * x1_ref[0] + g2 * acc_ref[...], lng_ref[...], lnb_ref[...])


def _combine(starts, ye, pos4, x1, mod3, lng, lnb, tile, chunk):
    b, s, d = x1.shape
    e, cap = ye.shape[1], ye.shape[2]
    gs = pltpu.PrefetchScalarGridSpec(
        num_scalar_prefetch=1,
        grid=(b, s // tile, e),
        in_specs=[pl.BlockSpec((1, 1, cap, d), lambda bi, ti, ei, st: (bi, ei, 0, 0)),
                  pl.BlockSpec((1, 1, 1, tile), lambda bi, ti, ei, st: (bi, ei, 0, ti)),
                  pl.BlockSpec((1, tile, d), lambda bi, ti, ei, st: (bi, ti, 0)),
                  pl.BlockSpec((1, 1, 6 * D_MODEL), lambda bi, ti, ei, st: (bi, 0, 0)),
                  pl.BlockSpec((1, d), lambda bi, ti, ei, st: (0, 0)),
                  pl.BlockSpec((1, d), lambda bi, ti, ei, st: (0, 0))],
        out_specs=pl.BlockSpec((1, tile, d), lambda bi, ti, ei, st: (bi, ti, 0)),
        scratch_shapes=[pltpu.VMEM((tile, d), F32)],
    )
    return pl.pallas_call(
        functools.partial(_combine_kernel, chunk=chunk),
        grid_spec=gs,
        out_shape=jax.ShapeDtypeStruct((b, s, d), F32),
        compiler_params=_cparams(("arbitrary", "arbitrary", "arbitrary")),
        name="combine",
    )(starts, ye, pos4, x1, mod3, lng, lnb)


def _head_pad(w, n_heads, width, offset=0):
    k = w.shape[0]
    w3 = w.reshape(k, n_heads, width)
    out = jnp.zeros((k, n_heads, LANE), w.dtype).at[:, :, offset:offset + width].set(w3)
    return out.reshape(k, n_heads * LANE)


def _prep_weights(w_in, w_uq, w_o_mla):
    kr = w_in[:, MLA_KR_OFF:DIFF_Q_OFF]
    dqw = w_in[:, DIFF_Q_OFF:DIFF_K_OFF]
    dkw = w_in[:, DIFF_K_OFF:DIFF_V_OFF]
    dvw = w_in[:, DIFF_V_OFF:GATE_OFF]
    k = w_in.shape[0]

    def rot_groups(w):
        return _rot_cols(w.reshape(k, -1, DIFF_HD)).reshape(k, -1)

    kr_pad = _head_pad(kr, 1, MLA_ROPE, MLA_NOPE)
    krr_pad = _head_pad(_rot_cols(kr), 1, MLA_ROPE, MLA_NOPE)
    wx = jnp.concatenate([w_in[:, MLA_Q_OFF:MLA_KR_OFF], kr_pad, krr_pad, dqw, rot_groups(dqw),
                          dkw, rot_groups(dkw), dvw, w_in[:, GATE_OFF:]], axis=1).astype(BF16)
    wc = jnp.concatenate([w_in[:, MLA_KV_OFF:MLA_KR_OFF], kr_pad, dkw, dvw], axis=1).astype(BF16)
    hd = MLA_NOPE + MLA_ROPE
    uq3 = w_uq.reshape(w_uq.shape[0], MLA_HEADS, hd)
    uq_rot = jnp.concatenate([jnp.zeros_like(uq3[..., :MLA_NOPE]), _rot_cols(uq3[..., MLA_NOPE:])], -1)
    wuq = _head_pad(w_uq, MLA_HEADS, hd).astype(BF16)
    wuqr = _head_pad(uq_rot.reshape(w_uq.shape[0], -1), MLA_HEADS, hd).astype(BF16)
    wom = jnp.zeros((MLA_HEADS, LANE, D_MODEL), w_o_mla.dtype).at[:, MLA_NOPE:, :].set(
        w_o_mla.reshape(MLA_HEADS, MLA_V, D_MODEL)).reshape(MLA_W, D_MODEL).astype(BF16)
    return wx, wc, wuq, wuqr, wom


def kernel(x, c, ctx, c_ctx, w_ada, b_ada, w_in, mla_g_q, mla_w_uq, mla_g_kv, mla_w_ukv, mla_w_o, diff_lambda,
           diff_g_subln, diff_w_o, w_out, ln1_g, ln1_b, moe_w_router, moe_b_router, moe_w1, moe_w3, moe_w2,
           ln2_g, ln2_b):
    b, s, d = x.shape
    assert d == D_MODEL and w_ada.shape[0] == DEPTH == 1 and b <= 4
    cap = EC_CAPACITY * s // N_EXPERTS
    tile = min(1024, s)
    gather_chunk = min(128, cap)
    scatter_chunk = min(256, cap)
    tm = min(512, s)
    tq = min(2048, s)
    tk = min(1024, s)

    cc = jnp.zeros((8, d), F32).at[:b].set(c).at[b].set(c_ctx)
    mod3 = _ada(cc, w_ada[0], b_ada[0]).reshape(8, 1, 6 * d)

    wx, wc, wuq, wuqr, wom = _prep_weights(w_in[0], mla_w_uq[0], mla_w_o[0])
    gq = mla_g_q[0].reshape(1, -1)
    gkv = mla_g_kv[0].reshape(1, -1)
    wukv = mla_w_ukv[0].astype(BF16)
    tabs = _rope_tables(s)

    q, k, v, dq, dk, dv, gates = _proj_lat(x, mod3, wx, gq, wuq, wuqr, gkv, wukv, tabs, tm)
    kc, vc, dkc, dvc = _proj_ctx(ctx, mod3, wc, gkv, wukv)

    o_mla = _mla_attn(q, kc, vc, k, v, tq, tk)
    o_diff = _diff_attn(diff_lambda[0], diff_g_subln[0].reshape(-1, 1), dq, dkc, dvc, dk, dv, tq // 2, tk)

    wr = moe_w_router[0].T
    wrh = wr.astype(BF16)
    wrl = (wr - wrh.astype(F32)).astype(BF16)
    x1, h2, lgt = _mix(o_mla, o_diff, gates, x, mod3, wom, diff_w_o[0].astype(BF16), w_out[0].astype(BF16),
                       ln1_g[0].reshape(1, -1), ln1_b[0].reshape(1, -1), wrh, wrl, tm)

    pos, gate, starts = _route(lgt, moe_b_router[0], cap, tile)
    pos4 = pos.reshape(b, N_EXPERTS, 1, s)
    gate4 = gate.reshape(b, N_EXPERTS, 1, s)
    ye = _ffn(starts, h2, pos4, gate4, moe_w1[0].astype(BF16), moe_w3[0].astype(BF16), moe_w2[0].astype(BF16),
              cap, tile, gather_chunk)
    return _combine(starts, ye, pos4, x1, mod3, ln2_g[0].reshape(1, -1), ln2_b[0].reshape(1, -1), tile,
                    scatter_chunk)
```

```python
import functools
import math

import numpy as np
import jax
import jax.numpy as jnp
from jax import lax
from jax.experimental import pallas as pl
from jax.experimental.pallas import tpu as pltpu

F32 = jnp.float32
BF16 = jnp.bfloat16
I32 = jnp.int32

D_MODEL = 1024
DEPTH = 1
GRID_W = 64
ROPE_THETA = 10000.0
EPS = 1e-6
MLA_HEADS = 8
MLA_Q_RANK = 256
MLA_KV_RANK = 128
MLA_NOPE = 64
MLA_ROPE = 32
MLA_V = 64
DIFF_HEADS = 4
DIFF_HD = 64
N_EXPERTS = 16
EXPERT_FF = 1024
EC_CAPACITY = 2

MLA_Q_OFF = 0
MLA_KV_OFF = MLA_Q_OFF + MLA_Q_RANK
MLA_KR_OFF = MLA_KV_OFF + MLA_KV_RANK
DIFF_Q_OFF = MLA_KR_OFF + MLA_ROPE
DIFF_K_OFF = DIFF_Q_OFF + DIFF_HEADS * 2 * DIFF_HD
DIFF_V_OFF = DIFF_K_OFF + DIFF_HEADS * 2 * DIFF_HD
GATE_OFF = DIFF_V_OFF + DIFF_HEADS * 2 * DIFF_HD
N_IN = GATE_OFF + 2 * D_MODEL
DIFF_WIDTH = DIFF_HEADS * 2 * DIFF_HD
MLA_SCALE = (MLA_NOPE + MLA_ROPE) ** -0.5
DIFF_SCALE = DIFF_HD ** -0.5
DEEPNORM_ALPHA = (2 * DEPTH) ** 0.25
LAMBDA_INIT = 0.8 - 0.6 * math.exp(-0.3 * 0)
LOG2E = 1.4426950408889634
LAGGED_MAX_HEADROOM = 100.0

LANE = 128
MLA_W = MLA_HEADS * LANE
PX_Q = 0
PX_KV = PX_Q + MLA_Q_RANK
PX_KR = PX_KV + MLA_KV_RANK
PX_KRR = PX_KR + LANE
PX_DQ = PX_KRR + LANE
PX_DQR = PX_DQ + DIFF_WIDTH
PX_DK = PX_DQR + DIFF_WIDTH
PX_DKR = PX_DK + DIFF_WIDTH
PX_DV = PX_DKR + DIFF_WIDTH
PX_G = PX_DV + DIFF_WIDTH
PX_N = PX_G + 2 * D_MODEL
CX_KV = 0
CX_KR = CX_KV + MLA_KV_RANK
CX_DK = CX_KR + LANE
CX_DV = CX_DK + DIFF_WIDTH
CX_N = CX_DV + DIFF_WIDTH

VMEM_LIMIT = 56 * 1024 * 1024


def _cparams(sem):
    return pltpu.CompilerParams(dimension_semantics=sem, vmem_limit_bytes=VMEM_LIMIT)


def _rot_cols(w):
    d = w.shape[-1]
    h = d // 2
    q = h // 2
    parts = []
    for s in (0, h):
        parts += [-w[..., s + q:s + h], w[..., s:s + q]]
    return jnp.concatenate(parts, axis=-1)


def _rope_tables(seq):
    t = np.arange(seq)
    row = (t // GRID_W).astype(np.float64)
    col = (t % GRID_W).astype(np.float64)

    def axial(d):
        h = d // 2
        inv = ROPE_THETA ** (-(np.arange(h // 2, dtype=np.float64) * 2.0 / h))
        ar, ac = row[:, None] * inv, col[:, None] * inv
        cos = np.concatenate([np.cos(ar), np.cos(ar), np.cos(ac), np.cos(ac)], -1)
        sin = np.concatenate([np.sin(ar), np.sin(ar), np.sin(ac), np.sin(ac)], -1)
        return cos, sin

    c32, s32 = axial(MLA_ROPE)
    c64, s64 = axial(DIFF_HD)
    cos_m = np.ones((seq, LANE))
    sin_m = np.zeros((seq, LANE))
    cos_m[:, MLA_NOPE:MLA_NOPE + MLA_ROPE] = c32
    sin_m[:, MLA_NOPE:MLA_NOPE + MLA_ROPE] = s32
    cos_d = np.concatenate([c64, c64], -1)
    sin_d = np.concatenate([s64, s64], -1)
    return tuple(jnp.asarray(a, F32) for a in (cos_m, sin_m, cos_d, sin_d))


def _rms(x, g):
    return x * lax.rsqrt(jnp.mean(x * x, axis=-1, keepdims=True) + EPS) * g


def _layer_norm(x, g, b):
    mu = jnp.mean(x, axis=-1, keepdims=True)
    xc = x - mu
    var = jnp.mean(xc * xc, axis=-1, keepdims=True)
    return xc * lax.rsqrt(var + EPS) * g + b


def _dot(a, b):
    return jnp.dot(a, b, preferred_element_type=F32)


def _dot_nt(a, b):
    return lax.dot_general(a, b, (((1,), (1,)), ((), ())), preferred_element_type=F32)


def _dot_tn(a, b):
    return lax.dot_general(a, b, (((0,), (0,)), ((), ())), preferred_element_type=F32)


def _ada_kernel(c_ref, w_ref, b_ref, o_ref):
    c = c_ref[...]
    s = (c / (1.0 + jnp.exp(-c))).astype(BF16)
    o_ref[...] = _dot(s, w_ref[...].astype(BF16)) + b_ref[...]


def _ada(cc, w_ada, b_ada):
    n = w_ada.shape[1]
    tn = 1024
    return pl.pallas_call(
        _ada_kernel,
        grid=(n // tn,),
        in_specs=[pl.BlockSpec((8, D_MODEL), lambda j: (0, 0)),
                  pl.BlockSpec((D_MODEL, tn), lambda j: (0, j)),
                  pl.BlockSpec((1, tn), lambda j: (0, j))],
        out_specs=pl.BlockSpec((8, tn), lambda j: (0, j)),
        out_shape=jax.ShapeDtypeStruct((8, n), F32),
        compiler_params=_cparams(("arbitrary",)),
        name="ada",
    )(cc, w_ada, b_ada.reshape(1, n))


def _proj_lat_kernel(x_ref, mod_ref, w_ref, gq_ref, wuq_ref, wuqr_ref, gkv_ref, wukv_ref,
                     cm_ref, sm_ref, cd_ref, sd_ref,
                     q_ref, k_ref, v_ref, dq_ref, dk_ref, dv_ref, g_ref):
    tm = x_ref.shape[1]
    sh = mod_ref[0, :, 0:D_MODEL]
    sc = mod_ref[0, :, D_MODEL:2 * D_MODEL]
    h = (x_ref[0] * (1.0 + sc) + sh).astype(BF16)

    def proj(a, n):
        return _dot(h, w_ref[:, a:a + n])

    lane = lax.broadcasted_iota(I32, (tm, LANE), 1)
    lo = lane < MLA_NOPE
    one0 = (lane == 0).astype(F32)
    cm, sm, cd, sd = cm_ref[...], sm_ref[...], cd_ref[...], sd_ref[...]

    cq = _rms(proj(PX_Q, MLA_Q_RANK), gq_ref[...]).astype(BF16)
    q = _dot(cq, wuq_ref[...])
    qr = _dot(cq, wuqr_ref[...])
    for hd in range(MLA_HEADS):
        sl = slice(hd * LANE, (hd + 1) * LANE)
        q_ref[0, :, sl] = ((q[:, sl] * cm + qr[:, sl] * sm) * (MLA_SCALE * LOG2E)).astype(BF16)

    ckv = _rms(proj(PX_KV, MLA_KV_RANK), gkv_ref[...]).astype(BF16)
    kv = _dot(ckv, wukv_ref[...])
    kr = proj(PX_KR, LANE) * cm + proj(PX_KRR, LANE) * sm
    for hd in range(MLA_HEADS):
        sl = slice(hd * LANE, (hd + 1) * LANE)
        kvh = kv[:, sl]
        k_ref[0, :, sl] = jnp.where(lo, kvh, kr).astype(BF16)
        v_ref[0, :, sl] = jnp.where(lo, one0, kvh).astype(BF16)

    dq = proj(PX_DQ, DIFF_WIDTH)
    dqr = proj(PX_DQR, DIFF_WIDTH)
    dk = proj(PX_DK, DIFF_WIDTH)
    dkr = proj(PX_DKR, DIFF_WIDTH)
    for hd in range(DIFF_HEADS):
        sl = slice(hd * LANE, (hd + 1) * LANE)
        qb = (dq[:, sl] * cd + dqr[:, sl] * sd) * (DIFF_SCALE * LOG2E)
        dq_ref[0, :, (2 * hd) * LANE:(2 * hd + 1) * LANE] = jnp.where(lo, qb, 0.0).astype(BF16)
        dq_ref[0, :, (2 * hd + 1) * LANE:(2 * hd + 2) * LANE] = jnp.where(lo, 0.0, qb).astype(BF16)
        dk_ref[0, :, sl] = (dk[:, sl] * cd + dkr[:, sl] * sd).astype(BF16)
    dv_ref[0] = proj(PX_DV, DIFF_WIDTH).astype(BF16)
    pg = proj(PX_G, 2 * D_MODEL)
    g_ref[0] = (1.0 / (1.0 + jnp.exp(-pg))).astype(BF16)


def _proj_ctx_kernel(x_ref, mod_ref, w_ref, gkv_ref, wukv_ref, k_ref, v_ref, dk_ref, dv_ref):
    tm = x_ref.shape[1]
    sh = mod_ref[0, :, 0:D_MODEL]
    sc = mod_ref[0, :, D_MODEL:2 * D_MODEL]
    h = (x_ref[0] * (1.0 + sc) + sh).astype(BF16)

    def proj(a, n):
        return _dot(h, w_ref[:, a:a + n])

    lane = lax.broadcasted_iota(I32, (tm, LANE), 1)
    lo = lane < MLA_NOPE
    one0 = (lane == 0).astype(F32)
    ckv = _rms(proj(CX_KV, MLA_KV_RANK), gkv_ref[...]).astype(BF16)
    kv = _dot(ckv, wukv_ref[...])
    kr = proj(CX_KR, LANE)
    for hd in range(MLA_HEADS):
        sl = slice(hd * LANE, (hd + 1) * LANE)
        kvh = kv[:, sl]
        k_ref[0, :, sl] = jnp.where(lo, kvh, kr).astype(BF16)
        v_ref[0, :, sl] = jnp.where(lo, one0, kvh).astype(BF16)
    dk_ref[0] = proj(CX_DK, DIFF_WIDTH).astype(BF16)
    dv_ref[0] = proj(CX_DV, DIFF_WIDTH).astype(BF16)


def _full(shape):
    nd = len(shape)
    return pl.BlockSpec(shape, lambda *_: (0,) * nd)


def _proj_lat(x, mod3, wx, gq, wuq, wuqr, gkv, wukv, tabs, tm):
    b, s, _ = x.shape
    row = lambda w: pl.BlockSpec((1, tm, w), lambda bi, ti: (bi, ti, 0))
    tab = pl.BlockSpec((tm, LANE), lambda bi, ti: (ti, 0))
    outs = [(MLA_W, BF16), (MLA_W, BF16), (MLA_W, BF16), (2 * DIFF_WIDTH, BF16), (DIFF_WIDTH, BF16),
            (DIFF_WIDTH, BF16), (2 * D_MODEL, BF16)]
    return pl.pallas_call(
        _proj_lat_kernel,
        grid=(b, s // tm),
        in_specs=[row(D_MODEL),
                  pl.BlockSpec((1, 1, 6 * D_MODEL), lambda bi, ti: (bi, 0, 0)),
                  _full(wx.shape), _full(gq.shape), _full(wuq.shape), _full(wuqr.shape),
                  _full(gkv.shape), _full(wukv.shape), tab, tab, tab, tab],
        out_specs=[row(w) for w, _ in outs],
        out_shape=[jax.ShapeDtypeStruct((b, s, w), dt) for w, dt in outs],
        compiler_params=_cparams(("arbitrary", "arbitrary")),
        name="proj_lat",
    )(x, mod3, wx, gq, wuq, wuqr, gkv, wukv, *tabs)


def _proj_ctx(ctx, mod3, wc, gkv, wukv):
    b, s, _ = ctx.shape
    row = lambda w: pl.BlockSpec((1, s, w), lambda bi: (bi, 0, 0))
    outs = [MLA_W, MLA_W, DIFF_WIDTH, DIFF_WIDTH]
    return pl.pallas_call(
        _proj_ctx_kernel,
        grid=(b,),
        in_specs=[row(D_MODEL),
                  pl.BlockSpec((1, 1, 6 * D_MODEL), lambda bi: (b, 0, 0)),
                  _full(wc.shape), _full(gkv.shape), _full(wukv.shape)],
        out_specs=[row(w) for w in outs],
        out_shape=[jax.ShapeDtypeStruct((b, s, w), BF16) for w in outs],
        compiler_params=_cparams(("arbitrary",)),
        name="proj_ctx",
    )(ctx, mod3, wc, gkv, wukv)


def _flash_t(q_t, kc_ref, vc_ref, kl_ref, vl_ref, p_ref, g_ref, m_ref, x_ref, acc_ref, tk, ps_ref=None, l_ref=None):
    n = kl_ref.shape[1] // tk

    def chunk(ref, i):
        return ref[0, pl.ds(pl.multiple_of(i * tk, tk), tk), :]

    def scores_exp(i):
        stab = m_ref[...]
        s_t = _dot(chunk(kl_ref, i), q_t)
        p = jnp.exp2(s_t - stab)
        p_ref[i % 2] = p.astype(BF16)
        if ps_ref is not None:
            ps_ref[i % 2] = jnp.sum(p, axis=0, keepdims=True)
        cmax = jnp.max(s_t, axis=0, keepdims=True)
        m_new = jnp.maximum(stab, cmax)
        g_ref[(i + 1) % 2] = jnp.exp2(stab - m_new)
        m_ref[...] = m_new
        x_ref[...] = jnp.maximum(x_ref[...], cmax - stab)

    def values(i):
        g = g_ref[i % 2]
        acc_ref[...] = g * acc_ref[...] + _dot_tn(chunk(vl_ref, i), p_ref[i % 2])
        if l_ref is not None:
            l_ref[...] = g * l_ref[...] + ps_ref[i % 2]

    def exact_chunk(k, v, m, first):
        p = jnp.exp2(_dot(k, q_t) - m)
        pv = _dot_tn(v, p.astype(BF16))
        acc_ref[...] = pv if first else acc_ref[...] + pv
        if l_ref is not None:
            ps = jnp.sum(p, axis=0, keepdims=True)
            l_ref[...] = ps if first else l_ref[...] + ps

    m0 = jnp.max(_dot(kc_ref[0], q_t), axis=0, keepdims=True)
    m_ref[...] = m0
    g_ref[0] = jnp.ones_like(m0)
    x_ref[...] = jnp.zeros_like(m0)
    exact_chunk(kc_ref[0], vc_ref[0], m0, True)

    scores_exp(0)

    def body(i, carry):
        values(i - 1)
        scores_exp(i)
        return carry

    lax.fori_loop(1, n, body, 0)
    values(n - 1)

    @pl.when(jnp.max(x_ref[...]) > LAGGED_MAX_HEADROOM)
    def _():
        m = m_ref[...]
        exact_chunk(kc_ref[0], vc_ref[0], m, True)

        def redo(i, carry):
            exact_chunk(chunk(kl_ref, i), chunk(vl_ref, i), m, False)
            return carry

        lax.fori_loop(0, n, redo, 0)


def _attn_scratch(tk, nq, with_l):
    v = pltpu.VMEM
    shapes = [v((2, tk, nq), BF16), v((2, 1, nq), F32), v((1, nq), F32), v((1, nq), F32), v((LANE, nq), F32)]
    if with_l:
        shapes += [v((2, 1, nq), F32), v((1, nq), F32)]
    return shapes


def _mla_attn_kernel(q_ref, kc_ref, vc_ref, kl_ref, vl_ref, o_ref, p_ref, g_ref, m_ref, x_ref, acc_ref, *, tk):
    _flash_t(q_ref[0].T, kc_ref, vc_ref, kl_ref, vl_ref, p_ref, g_ref, m_ref, x_ref, acc_ref, tk)
    acc = acc_ref[...]
    o_ref[0] = (acc / acc[0:1, :]).T.astype(BF16)


def _diff_attn_kernel(lam_ref, gs_ref, q1_ref, q2_ref, kc_ref, vc_ref, kl_ref, vl_ref, o_ref,
                      p_ref, g_ref, m_ref, x_ref, acc_ref, ps_ref, l_ref, *, tk):
    tq = q1_ref.shape[1]
    q_t = jnp.concatenate([q1_ref[0].T, q2_ref[0].T], axis=1)
    _flash_t(q_t, kc_ref, vc_ref, kl_ref, vl_ref, p_ref, g_ref, m_ref, x_ref, acc_ref, tk, ps_ref, l_ref)
    o = acc_ref[...] / l_ref[...]
    dl = lam_ref[...]
    lam = (jnp.exp(jnp.sum(dl[0:1] * dl[1:2], axis=-1, keepdims=True))
           - jnp.exp(jnp.sum(dl[2:3] * dl[3:4], axis=-1, keepdims=True)) + LAMBDA_INIT)
    w = o[:, :tq] - lam * o[:, tq:]
    w = w * lax.rsqrt(jnp.mean(w * w, axis=0, keepdims=True) + EPS) * gs_ref[...] * (1.0 - LAMBDA_INIT)
    o_ref[0] = w.T.astype(BF16)


def _mla_attn(q, kc, vc, kl, vl, tq, tk):
    b, s, _ = q.shape
    sc = kc.shape[1]
    assert s % tk == 0 and s % tq == 0
    qs = pl.BlockSpec((1, tq, LANE), lambda bi, hi, qi: (bi, qi, hi))
    cs = pl.BlockSpec((1, sc, LANE), lambda bi, hi, qi: (bi, 0, hi))
    ls = pl.BlockSpec((1, s, LANE), lambda bi, hi, qi: (bi, 0, hi))
    return pl.pallas_call(
        functools.partial(_mla_attn_kernel, tk=tk),
        grid=(b, MLA_HEADS, s // tq),
        in_specs=[qs, cs, cs, ls, ls],
        out_specs=qs,
        out_shape=jax.ShapeDtypeStruct((b, s, MLA_W), BF16),
        scratch_shapes=_attn_scratch(tk, tq, False),
        compiler_params=_cparams(("arbitrary", "arbitrary", "arbitrary")),
        name="mla_attn",
    )(q, kc, vc, kl, vl)


def _diff_attn(lam4, g_subln, dq, kc, vc, kl, vl, tq, tk):
    b, s, _ = dq.shape
    sc = kc.shape[1]
    q1 = pl.BlockSpec((1, tq, LANE), lambda bi, hi, qi: (bi, qi, 2 * hi))
    q2 = pl.BlockSpec((1, tq, LANE), lambda bi, hi, qi: (bi, qi, 2 * hi + 1))
    cs = pl.BlockSpec((1, sc, LANE), lambda bi, hi, qi: (bi, 0, hi))
    ls = pl.BlockSpec((1, s, LANE), lambda bi, hi, qi: (bi, 0, hi))
    return pl.pallas_call(
        functools.partial(_diff_attn_kernel, tk=tk),
        grid=(b, DIFF_HEADS, s // tq),
        in_specs=[_full(lam4.shape), _full(g_subln.shape), q1, q2, cs, cs, ls, ls],
        out_specs=pl.BlockSpec((1, tq, LANE), lambda bi, hi, qi: (bi, qi, hi)),
        out_shape=jax.ShapeDtypeStruct((b, s, DIFF_WIDTH), BF16),
        scratch_shapes=_attn_scratch(tk, 2 * tq, True),
        compiler_params=_cparams(("arbitrary", "arbitrary", "arbitrary")),
        name="diff_attn",
    )(lam4, g_subln, dq, dq, kc, vc, kl, vl)


def _mix_kernel(om_ref, od_ref, g_ref, x_ref, mod_ref, wom_ref, wod_ref, wout_ref, lng_ref, lnb_ref,
                wrh_ref, wrl_ref, x1_ref, h2_ref, lg_ref):
    gm = g_ref[0, :, 0:D_MODEL]
    gd = g_ref[0, :, D_MODEL:2 * D_MODEL]
    y = gm * _dot(om_ref[0], wom_ref[...]) + gd * _dot(od_ref[0], wod_ref[...])
    z = _dot(y.astype(BF16), wout_ref[...])
    g1 = mod_ref[0, :, 2 * D_MODEL:3 * D_MODEL]
    sh2 = mod_ref[0, :, 3 * D_MODEL:4 * D_MODEL]
    sc2 = mod_ref[0, :, 4 * D_MODEL:5 * D_MODEL]
    x1 = _layer_norm(DEEPNORM_ALPHA * x_ref[0] + g1 * z, lng_ref[...], lnb_ref[...])
    x1_ref[0] = x1
    h2 = x1 * (1.0 + sc2) + sh2
    hi = h2.astype(BF16)
    lo = (h2 - hi.astype(F32)).astype(BF16)
    h2_ref[0] = hi
    lg_ref[0] = _dot_nt(wrh_ref[...], hi) + (_dot_nt(wrh_ref[...], lo) + _dot_nt(wrl_ref[...], hi))


def _mix(om, od, gates, x, mod3, wom, wod, wout, lng, lnb, wrh, wrl, tm):
    b, s, _ = x.shape
    row = lambda w: pl.BlockSpec((1, tm, w), lambda bi, ti: (bi, ti, 0))
    return pl.pallas_call(
        _mix_kernel,
        grid=(b, s // tm),
        in_specs=[row(MLA_W), row(DIFF_WIDTH), row(2 * D_MODEL), row(D_MODEL),
                  pl.BlockSpec((1, 1, 6 * D_MODEL), lambda bi, ti: (bi, 0, 0)),
                  _full(wom.shape), _full(wod.shape), _full(wout.shape), _full(lng.shape), _full(lnb.shape),
                  _full(wrh.shape), _full(wrl.shape)],
        out_specs=[row(D_MODEL), row(D_MODEL),
                   pl.BlockSpec((1, N_EXPERTS, tm), lambda bi, ti: (bi, 0, ti))],
        out_shape=[jax.ShapeDtypeStruct((b, s, D_MODEL), F32), jax.ShapeDtypeStruct((b, s, D_MODEL), BF16),
                   jax.ShapeDtypeStruct((b, N_EXPERTS, s), F32)],
        compiler_params=_cparams(("arbitrary", "arbitrary")),
        name="mix",
    )(om, od, gates, x, mod3, wom, wod, wout, lng, lnb, wrh, wrl)


def _route_kernel(lg_ref, br_ref, pos_ref, gate_ref, st_ref, *, cap, tile):
    e, s = lg_ref.shape[1], lg_ref.shape[2]
    lg = lg_ref[0] + br_ref[...]
    ex = jnp.exp(lg - jnp.max(lg, axis=0, keepdims=True))
    aff = ex / jnp.sum(ex, axis=0, keepdims=True)
    gate_ref[0] = aff
    bits = pltpu.bitcast(aff, I32)

    def count(mask):
        return jnp.sum(mask.astype(I32), axis=1, keepdims=True)

    def search(i, thr):
        cand = thr | (jnp.int32(1) << (30 - i))
        return jnp.where(count(bits >= cand) >= cap, cand, thr)

    thr = lax.fori_loop(0, 31, search, jnp.zeros((e, 1), I32))
    gt = bits > thr
    eq = bits == thr
    need = cap - count(gt)

    r = lax.broadcasted_iota(I32, (LANE, LANE), 0)
    c = lax.broadcasted_iota(I32, (LANE, LANE), 1)
    tri = (r < c).astype(BF16)
    lane = lax.broadcasted_iota(I32, (e, LANE), 1)

    def prefix(mask_fn, emit):
        carry = jnp.zeros((e, 1), F32)
        for j in range(s // LANE):
            m = mask_fn(j)
            emit(j, carry + _dot(m.astype(BF16), tri), m, carry)
            carry = carry + jnp.sum(m.astype(F32), axis=1, keepdims=True)

    def eq_blk(j):
        return eq[:, j * LANE:(j + 1) * LANE]

    def emit_sel(j, rank, m, carry):
        sel = gt[:, j * LANE:(j + 1) * LANE] | (m & (rank < need.astype(F32)))
        pos_ref[0, :, j * LANE:(j + 1) * LANE] = sel.astype(I32)

    prefix(eq_blk, emit_sel)

    starts = [jnp.zeros((e, LANE), I32)]

    def sel_blk(j):
        return pos_ref[0, :, j * LANE:(j + 1) * LANE] > 0

    def emit_pos(j, rank, m, carry):
        if (j * LANE) % tile == 0:
            starts[0] = jnp.where(lane == (j * LANE) // tile, carry.astype(I32), starts[0])
        pos_ref[0, :, j * LANE:(j + 1) * LANE] = jnp.where(m, rank.astype(I32), -1)

    prefix(sel_blk, emit_pos)
    st_ref[0] = jnp.where(lane == s // tile, cap, starts[0])


def _route(lgt, b_router, cap, tile):
    b, e, s = lgt.shape
    blk = pl.BlockSpec((1, e, s), lambda bi: (bi, 0, 0))
    return pl.pallas_call(
        functools.partial(_route_kernel, cap=cap, tile=tile),
        grid=(b,),
        in_specs=[blk, _full((e, 1))],
        out_specs=[blk, blk, pl.BlockSpec((1, e, LANE), lambda bi: (bi, 0, 0))],
        out_shape=[jax.ShapeDtypeStruct((b, e, s), I32), jax.ShapeDtypeStruct((b, e, s), F32),
                   jax.ShapeDtypeStruct((b, e, LANE), I32)],
        compiler_params=_cparams(("arbitrary",)),
        name="route",
    )(lgt, b_router.reshape(e, 1))


def _ffn_kernel(st_ref, h_ref, pos_ref, gate_ref, w1_ref, w3_ref, w2_ref, ye_ref, xe_ref, gs_ref, *, chunk, sub):
    bi, ei, ti = pl.program_id(0), pl.program_id(1), pl.program_id(2)
    cap, tile = xe_ref.shape[0], h_ref.shape[1]
    nsub = tile // sub
    bounds = [st_ref[bi, ei, ti * nsub + u] for u in range(nsub + 1)]

    @pl.when(ti == 0)
    def _():
        xe_ref[...] = jnp.zeros_like(xe_ref)
        gs_ref[...] = jnp.zeros_like(gs_ref)

    slot = lax.broadcasted_iota(I32, (chunk, sub), 0)
    for c in range(cap // chunk):
        lo, hi = c * chunk, (c + 1) * chunk
        rows = slice(lo, hi)

        @pl.when((bounds[0] < hi) & (bounds[nsub] > lo))
        def _():
            for u in range(nsub):
                @pl.when((bounds[u] < hi) & (bounds[u + 1] > lo))
                def _():
                    cols = slice(u * sub, (u + 1) * sub)
                    hit = slot == (pos_ref[0, 0, :, cols] - lo)
                    xe_ref[rows, :] += _dot(hit.astype(BF16), h_ref[0, cols, :])
                    gs_ref[rows, :] += jnp.sum(jnp.where(hit, gate_ref[0, 0, :, cols], 0.0), axis=1,
                                               keepdims=True)

    @pl.when(ti == pl.num_programs(2) - 1)
    def _():
        xe = xe_ref[...].astype(BF16)
        a = _dot(xe, w1_ref[0])
        hid = ((a / (1.0 + jnp.exp(-a))) * _dot(xe, w3_ref[0])).astype(BF16)
        ye_ref[0, 0] = (_dot(hid, w2_ref[0]) * gs_ref[...]).astype(BF16)


def _ffn(starts, h2, pos4, gate4, w1, w3, w2, cap, tile, chunk, sub):
    b, s, d = h2.shape
    e = w1.shape[0]
    lane_blk = pl.BlockSpec((1, 1, 1, tile), lambda bi, ei, ti, st: (bi, ei, 0, ti))
    wspec = lambda shp: pl.BlockSpec((1,) + shp, lambda bi, ei, ti, st: (ei, 0, 0))
    gs = pltpu.PrefetchScalarGridSpec(
        num_scalar_prefetch=1,
        grid=(b, e, s // tile),
        in_specs=[pl.BlockSpec((1, tile, d), lambda bi, ei, ti, st: (bi, ti, 0)),
                  lane_blk, lane_blk, wspec(w1.shape[1:]), wspec(w3.shape[1:]), wspec(w2.shape[1:])],
        out_specs=pl.BlockSpec((1, 1, cap, d), lambda bi, ei, ti, st: (bi, ei, 0, 0)),
        scratch_shapes=[pltpu.VMEM((cap, d), F32), pltpu.VMEM((cap, 1), F32)],
    )
    return pl.pallas_call(
        functools.partial(_ffn_kernel, chunk=chunk, sub=sub),
        grid_spec=gs,
        out_shape=jax.ShapeDtypeStruct((b, e, cap, d), BF16),
        compiler_params=_cparams(("arbitrary", "arbitrary", "arbitrary")),
        name="ffn",
    )(starts, h2, pos4, gate4, w1, w3, w2)


def _combine_kernel(st_ref, ye_ref, pos_ref, x1_ref, mod_ref, lng_ref, lnb_ref, o_ref, acc_ref, *, chunk, sub):
    bi, ti, ei = pl.program_id(0), pl.program_id(1), pl.program_id(2)
    cap, tile = ye_ref.shape[2], x1_ref.shape[1]
    nsub = tile // sub
    bounds = [st_ref[bi, ei, ti * nsub + u] for u in range(nsub + 1)]

    @pl.when(ei == 0)
    def _():
        acc_ref[...] = jnp.zeros_like(acc_ref)

    slot = lax.broadcasted_iota(I32, (chunk, sub), 0)
    for c in range(cap // chunk):
        lo, hi = c * chunk, (c + 1) * chunk

        @pl.when((bounds[0] < hi) & (bounds[nsub] > lo))
        def _():
            for u in range(nsub):
                @pl.when((bounds[u] < hi) & (bounds[u + 1] > lo))
                def _():
                    cols = slice(u * sub, (u + 1) * sub)
                    hit = (slot == (pos_ref[0, 0, :, cols] - lo)).astype(BF16)
                    acc_ref[cols, :] += _dot_tn(hit, ye_ref[0, 0, lo:hi, :])

    @pl.when(ei == pl.num_programs(2) - 1)
    def _():
        g2 = mod_ref[0, :, 5 * D_MODEL:6 * D_MODEL]
        o_ref[0] = _layer_norm(DEEPNORM_ALPHA * x1_ref[0] + g2 * acc_ref[...], lng_ref[...], lnb_ref[...])


def _combine(starts, ye, pos4, x1, mod3, lng, lnb, tile, chunk, sub):
    b, s, d = x1.shape
    e, cap = ye.shape[1], ye.shape[2]
    gs = pltpu.PrefetchScalarGridSpec(
        num_scalar_prefetch=1,
        grid=(b, s // tile, e),
        in_specs=[pl.BlockSpec((1, 1, cap, d), lambda bi, ti, ei, st: (bi, ei, 0, 0)),
                  pl.BlockSpec((1, 1, 1, tile), lambda bi, ti, ei, st: (bi, ei, 0, ti)),
                  pl.BlockSpec((1, tile, d), lambda bi, ti, ei, st: (bi, ti, 0)),
                  pl.BlockSpec((1, 1, 6 * D_MODEL), lambda bi, ti, ei, st: (bi, 0, 0)),
                  pl.BlockSpec((1, d), lambda bi, ti, ei, st: (0, 0)),
                  pl.BlockSpec((1, d), lambda bi, ti, ei, st: (0, 0))],
        out_specs=pl.BlockSpec((1, tile, d), lambda bi, ti, ei, st: (bi, ti, 0)),
        scratch_shapes=[pltpu.VMEM((tile, d), F32)],
    )
    return pl.pallas_call(
        functools.partial(_combine_kernel, chunk=chunk, sub=sub),
        grid_spec=gs,
        out_shape=jax.ShapeDtypeStruct((b, s, d), F32),
        compiler_params=_cparams(("arbitrary", "arbitrary", "arbitrary")),
        name="combine",
    )(starts, ye, pos4, x1, mod3, lng, lnb)


def _head_pad(w, n_heads, width, offset=0):
    k = w.shape[0]
    w3 = w.reshape(k, n_heads, width)
    out = jnp.zeros((k, n_heads, LANE), w.dtype).at[:, :, offset:offset + width].set(w3)
    return out.reshape(k, n_heads * LANE)


def _prep_weights(w_in, w_uq, w_o_mla):
    kr = w_in[:, MLA_KR_OFF:DIFF_Q_OFF]
    dqw = w_in[:, DIFF_Q_OFF:DIFF_K_OFF]
    dkw = w_in[:, DIFF_K_OFF:DIFF_V_OFF]
    dvw = w_in[:, DIFF_V_OFF:GATE_OFF]
    k = w_in.shape[0]

    def rot_groups(w):
        return _rot_cols(w.reshape(k, -1, DIFF_HD)).reshape(k, -1)

    kr_pad = _head_pad(kr, 1, MLA_ROPE, MLA_NOPE)
    krr_pad = _head_pad(_rot_cols(kr), 1, MLA_ROPE, MLA_NOPE)
    wx = jnp.concatenate([w_in[:, MLA_Q_OFF:MLA_KR_OFF], kr_pad, krr_pad, dqw, rot_groups(dqw),
                          dkw, rot_groups(dkw), dvw, w_in[:, GATE_OFF:]], axis=1).astype(BF16)
    wc = jnp.concatenate([w_in[:, MLA_KV_OFF:MLA_KR_OFF], kr_pad, dkw, dvw], axis=1).astype(BF16)
    hd = MLA_NOPE + MLA_ROPE
    uq3 = w_uq.reshape(w_uq.shape[0], MLA_HEADS, hd)
    uq_rot = jnp.concatenate([jnp.zeros_like(uq3[..., :MLA_NOPE]), _rot_cols(uq3[..., MLA_NOPE:])], -1)
    wuq = _head_pad(w_uq, MLA_HEADS, hd).astype(BF16)
    wuqr = _head_pad(uq_rot.reshape(w_uq.shape[0], -1), MLA_HEADS, hd).astype(BF16)
    wom = jnp.zeros((MLA_HEADS, LANE, D_MODEL), w_o_mla.dtype).at[:, MLA_NOPE:, :].set(
        w_o_mla.reshape(MLA_HEADS, MLA_V, D_MODEL)).reshape(MLA_W, D_MODEL).astype(BF16)
    return wx, wc, wuq, wuqr, wom


def kernel(x, c, ctx, c_ctx, w_ada, b_ada, w_in, mla_g_q, mla_w_uq, mla_g_kv, mla_w_ukv, mla_w_o, diff_lambda,
           diff_g_subln, diff_w_o, w_out, ln1_g, ln1_b, moe_w_router, moe_b_router, moe_w1, moe_w3, moe_w2,
           ln2_g, ln2_b):
    b, s, d = x.shape
    assert d == D_MODEL and w_ada.shape[0] == DEPTH == 1 and b <= 4
    cap = EC_CAPACITY * s // N_EXPERTS
    tile = min(1024, s)
    sub = tile
    gather_chunk = min(128, cap)
    scatter_chunk = min(256, cap)
    tm = min(512, s)
    tq = min(2048, s)
    tk = min(2048, s)

    cc = jnp.zeros((8, d), F32).at[:b].set(c).at[b].set(c_ctx)
    mod3 = _ada(cc, w_ada[0], b_ada[0]).reshape(8, 1, 6 * d)

    wx, wc, wuq, wuqr, wom = _prep_weights(w_in[0], mla_w_uq[0], mla_w_o[0])
    gq = mla_g_q[0].reshape(1, -1)
    gkv = mla_g_kv[0].reshape(1, -1)
    wukv = mla_w_ukv[0].astype(BF16)
    tabs = _rope_tables(s)

    q, k, v, dq, dk, dv, gates = _proj_lat(x, mod3, wx, gq, wuq, wuqr, gkv, wukv, tabs, tm)
    kc, vc, dkc, dvc = _proj_ctx(ctx, mod3, wc, gkv, wukv)

    o_mla = _mla_attn(q, kc, vc, k, v, tq, tk)
    o_diff = _diff_attn(diff_lambda[0], diff_g_subln[0].reshape(-1, 1), dq, dkc, dvc, dk, dv, tq // 2, tk)

    wr = moe_w_router[0].T
    wrh = wr.astype(BF16)
    wrl = (wr - wrh.astype(F32)).astype(BF16)
    x1, h2, lgt = _mix(o_mla, o_diff, gates, x, mod3, wom, diff_w_o[0].astype(BF16), w_out[0].astype(BF16),
                       ln1_g[0].reshape(1, -1), ln1_b[0].reshape(1, -1), wrh, wrl, tm)

    pos, gate, starts = _route(lgt, moe_b_router[0], cap, sub)
    pos4 = pos.reshape(b, N_EXPERTS, 1, s)
    gate4 = gate.reshape(b, N_EXPERTS, 1, s)
    ye = _ffn(starts, h2, pos4, gate4, moe_w1[0].astype(BF16), moe_w3[0].astype(BF16), moe_w2[0].astype(BF16),
              cap, tile, gather_chunk, sub)
    return _combine(starts, ye, pos4, x1, mod3, ln2_g[0].reshape(1, -1), ln2_b[0].reshape(1, -1), tile,
                    scatter_chunk, sub)
```

```python
import functools
import math

import numpy as np
import jax
import jax.numpy as jnp
from jax import lax
from jax.experimental import pallas as pl
from jax.experimental.pallas import tpu as pltpu
from jax.experimental.pallas import tpu_sc as plsc

F32 = jnp.float32
BF16 = jnp.bfloat16
I32 = jnp.int32
U32 = jnp.uint32

D_MODEL = 1024
DEPTH = 1
GRID_W = 64
ROPE_THETA = 10000.0
EPS = 1e-6
MLA_HEADS = 8
MLA_Q_RANK = 256
MLA_KV_RANK = 128
MLA_NOPE = 64
MLA_ROPE = 32
MLA_V = 64
DIFF_HEADS = 4
DIFF_HD = 64
N_EXPERTS = 16
EXPERT_FF = 1024
EC_CAPACITY = 2

MLA_Q_OFF = 0
MLA_KV_OFF = MLA_Q_OFF + MLA_Q_RANK
MLA_KR_OFF = MLA_KV_OFF + MLA_KV_RANK
DIFF_Q_OFF = MLA_KR_OFF + MLA_ROPE
DIFF_K_OFF = DIFF_Q_OFF + DIFF_HEADS * 2 * DIFF_HD
DIFF_V_OFF = DIFF_K_OFF + DIFF_HEADS * 2 * DIFF_HD
GATE_OFF = DIFF_V_OFF + DIFF_HEADS * 2 * DIFF_HD
N_IN = GATE_OFF + 2 * D_MODEL
DIFF_WIDTH = DIFF_HEADS * 2 * DIFF_HD
MLA_SCALE = (MLA_NOPE + MLA_ROPE) ** -0.5
DIFF_SCALE = DIFF_HD ** -0.5
DEEPNORM_ALPHA = (2 * DEPTH) ** 0.25
LAMBDA_INIT = 0.8 - 0.6 * math.exp(-0.3 * 0)
LOG2E = 1.4426950408889634
LAGGED_MAX_HEADROOM = 100.0

LANE = 128
MLA_W = MLA_HEADS * LANE
PX_Q = 0
PX_KV = PX_Q + MLA_Q_RANK
PX_KR = PX_KV + MLA_KV_RANK
PX_KRR = PX_KR + LANE
PX_DQ = PX_KRR + LANE
PX_DQR = PX_DQ + DIFF_WIDTH
PX_DK = PX_DQR + DIFF_WIDTH
PX_DKR = PX_DK + DIFF_WIDTH
PX_DV = PX_DKR + DIFF_WIDTH
PX_G = PX_DV + DIFF_WIDTH
PX_N = PX_G + 2 * D_MODEL
CX_KV = 0
CX_KR = CX_KV + MLA_KV_RANK
CX_DK = CX_KR + LANE
CX_DV = CX_DK + DIFF_WIDTH
CX_N = CX_DV + DIFF_WIDTH

VMEM_LIMIT = 56 * 1024 * 1024


def _cparams(sem):
    return pltpu.CompilerParams(dimension_semantics=sem, vmem_limit_bytes=VMEM_LIMIT)


def _rot_cols(w):
    d = w.shape[-1]
    h = d // 2
    q = h // 2
    parts = []
    for s in (0, h):
        parts += [-w[..., s + q:s + h], w[..., s:s + q]]
    return jnp.concatenate(parts, axis=-1)


def _rope_tables(seq):
    t = np.arange(seq)
    row = (t // GRID_W).astype(np.float64)
    col = (t % GRID_W).astype(np.float64)

    def axial(d):
        h = d // 2
        inv = ROPE_THETA ** (-(np.arange(h // 2, dtype=np.float64) * 2.0 / h))
        ar, ac = row[:, None] * inv, col[:, None] * inv
        cos = np.concatenate([np.cos(ar), np.cos(ar), np.cos(ac), np.cos(ac)], -1)
        sin = np.concatenate([np.sin(ar), np.sin(ar), np.sin(ac), np.sin(ac)], -1)
        return cos, sin

    c32, s32 = axial(MLA_ROPE)
    c64, s64 = axial(DIFF_HD)
    cos_m = np.ones((seq, LANE))
    sin_m = np.zeros((seq, LANE))
    cos_m[:, MLA_NOPE:MLA_NOPE + MLA_ROPE] = c32
    sin_m[:, MLA_NOPE:MLA_NOPE + MLA_ROPE] = s32
    cos_d = np.concatenate([c64, c64], -1)
    sin_d = np.concatenate([s64, s64], -1)
    return tuple(jnp.asarray(a, F32) for a in (cos_m, sin_m, cos_d, sin_d))


def _rms(x, g):
    return x * lax.rsqrt(jnp.mean(x * x, axis=-1, keepdims=True) + EPS) * g


def _layer_norm(x, g, b):
    mu = jnp.mean(x, axis=-1, keepdims=True)
    xc = x - mu
    var = jnp.mean(xc * xc, axis=-1, keepdims=True)
    return xc * lax.rsqrt(var + EPS) * g + b


def _dot(a, b):
    return jnp.dot(a, b, preferred_element_type=F32)


def _dot_nt(a, b):
    return lax.dot_general(a, b, (((1,), (1,)), ((), ())), preferred_element_type=F32)


def _dot_tn(a, b):
    return lax.dot_general(a, b, (((0,), (0,)), ((), ())), preferred_element_type=F32)


def _ada_kernel(c_ref, w_ref, b_ref, o_ref):
    c = c_ref[...]
    s = (c / (1.0 + jnp.exp(-c))).astype(BF16)
    o_ref[...] = _dot(s, w_ref[...].astype(BF16)) + b_ref[...]


def _ada(cc, w_ada, b_ada):
    n = w_ada.shape[1]
    tn = 1024
    return pl.pallas_call(
        _ada_kernel,
        grid=(n // tn,),
        in_specs=[pl.BlockSpec((8, D_MODEL), lambda j: (0, 0)),
                  pl.BlockSpec((D_MODEL, tn), lambda j: (0, j)),
                  pl.BlockSpec((1, tn), lambda j: (0, j))],
        out_specs=pl.BlockSpec((8, tn), lambda j: (0, j)),
        out_shape=jax.ShapeDtypeStruct((8, n), F32),
        compiler_params=_cparams(("arbitrary",)),
        name="ada",
    )(cc, w_ada, b_ada.reshape(1, n))


def _proj_lat_kernel(x_ref, mod_ref, w_ref, gq_ref, wuq_ref, wuqr_ref, gkv_ref, wukv_ref,
                     cm_ref, sm_ref, cd_ref, sd_ref,
                     q_ref, k_ref, v_ref, dq_ref, dk_ref, dv_ref, g_ref):
    tm = x_ref.shape[1]
    sh = mod_ref[0, :, 0:D_MODEL]
    sc = mod_ref[0, :, D_MODEL:2 * D_MODEL]
    h = (x_ref[0] * (1.0 + sc) + sh).astype(BF16)

    def proj(a, n):
        return _dot(h, w_ref[:, a:a + n])

    lane = lax.broadcasted_iota(I32, (tm, LANE), 1)
    lo = lane < MLA_NOPE
    one0 = (lane == 0).astype(F32)
    cm, sm, cd, sd = cm_ref[...], sm_ref[...], cd_ref[...], sd_ref[...]

    cq = _rms(proj(PX_Q, MLA_Q_RANK), gq_ref[...]).astype(BF16)
    q = _dot(cq, wuq_ref[...])
    qr = _dot(cq, wuqr_ref[...])
    for hd in range(MLA_HEADS):
        sl = slice(hd * LANE, (hd + 1) * LANE)
        q_ref[0, :, sl] = ((q[:, sl] * cm + qr[:, sl] * sm) * (MLA_SCALE * LOG2E)).astype(BF16)

    ckv = _rms(proj(PX_KV, MLA_KV_RANK), gkv_ref[...]).astype(BF16)
    kv = _dot(ckv, wukv_ref[...])
    kr = proj(PX_KR, LANE) * cm + proj(PX_KRR, LANE) * sm
    for hd in range(MLA_HEADS):
        sl = slice(hd * LANE, (hd + 1) * LANE)
        kvh = kv[:, sl]
        k_ref[0, :, sl] = jnp.where(lo, kvh, kr).astype(BF16)
        v_ref[0, :, sl] = jnp.where(lo, one0, kvh).astype(BF16)

    dq = proj(PX_DQ, DIFF_WIDTH)
    dqr = proj(PX_DQR, DIFF_WIDTH)
    dk = proj(PX_DK, DIFF_WIDTH)
    dkr = proj(PX_DKR, DIFF_WIDTH)
    for hd in range(DIFF_HEADS):
        sl = slice(hd * LANE, (hd + 1) * LANE)
        qb = (dq[:, sl] * cd + dqr[:, sl] * sd) * (DIFF_SCALE * LOG2E)
        dq_ref[0, :, (2 * hd) * LANE:(2 * hd + 1) * LANE] = jnp.where(lo, qb, 0.0).astype(BF16)
        dq_ref[0, :, (2 * hd + 1) * LANE:(2 * hd + 2) * LANE] = jnp.where(lo, 0.0, qb).astype(BF16)
        dk_ref[0, :, sl] = (dk[:, sl] * cd + dkr[:, sl] * sd).astype(BF16)
    dv_ref[0] = proj(PX_DV, DIFF_WIDTH).astype(BF16)
    pg = proj(PX_G, 2 * D_MODEL)
    g_ref[0] = (1.0 / (1.0 + jnp.exp(-pg))).astype(BF16)


def _proj_ctx_kernel(x_ref, mod_ref, w_ref, gkv_ref, wukv_ref, k_ref, v_ref, dk_ref, dv_ref):
    tm = x_ref.shape[1]
    sh = mod_ref[0, :, 0:D_MODEL]
    sc = mod_ref[0, :, D_MODEL:2 * D_MODEL]
    h = (x_ref[0] * (1.0 + sc) + sh).astype(BF16)

    def proj(a, n):
        return _dot(h, w_ref[:, a:a + n])

    lane = lax.broadcasted_iota(I32, (tm, LANE), 1)
    lo = lane < MLA_NOPE
    one0 = (lane == 0).astype(F32)
    ckv = _rms(proj(CX_KV, MLA_KV_RANK), gkv_ref[...]).astype(BF16)
    kv = _dot(ckv, wukv_ref[...])
    kr = proj(CX_KR, LANE)
    for hd in range(MLA_HEADS):
        sl = slice(hd * LANE, (hd + 1) * LANE)
        kvh = kv[:, sl]
        k_ref[0, :, sl] = jnp.where(lo, kvh, kr).astype(BF16)
        v_ref[0, :, sl] = jnp.where(lo, one0, kvh).astype(BF16)
    dk_ref[0] = proj(CX_DK, DIFF_WIDTH).astype(BF16)
    dv_ref[0] = proj(CX_DV, DIFF_WIDTH).astype(BF16)


def _full(shape):
    nd = len(shape)
    return pl.BlockSpec(shape, lambda *_: (0,) * nd)


def _proj_lat(x, mod3, wx, gq, wuq, wuqr, gkv, wukv, tabs, tm):
    b, s, _ = x.shape
    row = lambda w: pl.BlockSpec((1, tm, w), lambda bi, ti: (bi, ti, 0))
    tab = pl.BlockSpec((tm, LANE), lambda bi, ti: (ti, 0))
    outs = [(MLA_W, BF16), (MLA_W, BF16), (MLA_W, BF16), (2 * DIFF_WIDTH, BF16), (DIFF_WIDTH, BF16),
            (DIFF_WIDTH, BF16), (2 * D_MODEL, BF16)]
    return pl.pallas_call(
        _proj_lat_kernel,
        grid=(b, s // tm),
        in_specs=[row(D_MODEL),
                  pl.BlockSpec((1, 1, 6 * D_MODEL), lambda bi, ti: (bi, 0, 0)),
                  _full(wx.shape), _full(gq.shape), _full(wuq.shape), _full(wuqr.shape),
                  _full(gkv.shape), _full(wukv.shape), tab, tab, tab, tab],
        out_specs=[row(w) for w, _ in outs],
        out_shape=[jax.ShapeDtypeStruct((b, s, w), dt) for w, dt in outs],
        compiler_params=_cparams(("arbitrary", "arbitrary")),
        name="proj_lat",
    )(x, mod3, wx, gq, wuq, wuqr, gkv, wukv, *tabs)


def _proj_ctx(ctx, mod3, wc, gkv, wukv):
    b, s, _ = ctx.shape
    row = lambda w: pl.BlockSpec((1, s, w), lambda bi: (bi, 0, 0))
    outs = [MLA_W, MLA_W, DIFF_WIDTH, DIFF_WIDTH]
    return pl.pallas_call(
        _proj_ctx_kernel,
        grid=(b,),
        in_specs=[row(D_MODEL),
                  pl.BlockSpec((1, 1, 6 * D_MODEL), lambda bi: (b, 0, 0)),
                  _full(wc.shape), _full(gkv.shape), _full(wukv.shape)],
        out_specs=[row(w) for w in outs],
        out_shape=[jax.ShapeDtypeStruct((b, s, w), BF16) for w in outs],
        compiler_params=_cparams(("arbitrary",)),
        name="proj_ctx",
    )(ctx, mod3, wc, gkv, wukv)


def _flash_t(q_t, kc_ref, vc_ref, kl_ref, vl_ref, p_ref, g_ref, m_ref, x_ref, acc_ref, tk, ps_ref=None, l_ref=None):
    n = kl_ref.shape[1] // tk

    def chunk(ref, i):
        return ref[0, pl.ds(pl.multiple_of(i * tk, tk), tk), :]

    def scores_exp(i):
        stab = m_ref[...]
        s_t = _dot(chunk(kl_ref, i), q_t)
        p = jnp.exp2(s_t - stab)
        p_ref[i % 2] = p.astype(BF16)
        if ps_ref is not None:
            ps_ref[i % 2] = jnp.sum(p, axis=0, keepdims=True)
        cmax = jnp.max(s_t, axis=0, keepdims=True)
        m_new = jnp.maximum(stab, cmax)
        g_ref[(i + 1) % 2] = jnp.exp2(stab - m_new)
        m_ref[...] = m_new
        x_ref[...] = jnp.maximum(x_ref[...], cmax - stab)

    def values(i):
        g = g_ref[i % 2]
        acc_ref[...] = g * acc_ref[...] + _dot_tn(chunk(vl_ref, i), p_ref[i % 2])
        if l_ref is not None:
            l_ref[...] = g * l_ref[...] + ps_ref[i % 2]

    def exact_chunk(k, v, m, first):
        p = jnp.exp2(_dot(k, q_t) - m)
        pv = _dot_tn(v, p.astype(BF16))
        acc_ref[...] = pv if first else acc_ref[...] + pv
        if l_ref is not None:
            ps = jnp.sum(p, axis=0, keepdims=True)
            l_ref[...] = ps if first else l_ref[...] + ps

    m0 = jnp.max(_dot(kc_ref[0], q_t), axis=0, keepdims=True)
    m_ref[...] = m0
    g_ref[0] = jnp.ones_like(m0)
    x_ref[...] = jnp.zeros_like(m0)
    exact_chunk(kc_ref[0], vc_ref[0], m0, True)

    scores_exp(0)

    def body(i, carry):
        values(i - 1)
        scores_exp(i)
        return carry

    lax.fori_loop(1, n, body, 0)
    values(n - 1)

    @pl.when(jnp.max(x_ref[...]) > LAGGED_MAX_HEADROOM)
    def _():
        m = m_ref[...]
        exact_chunk(kc_ref[0], vc_ref[0], m, True)

        def redo(i, carry):
            exact_chunk(chunk(kl_ref, i), chunk(vl_ref, i), m, False)
            return carry

        lax.fori_loop(0, n, redo, 0)


def _attn_scratch(tk, nq, with_l):
    v = pltpu.VMEM
    shapes = [v((2, tk, nq), BF16), v((2, 1, nq), F32), v((1, nq), F32), v((1, nq), F32), v((LANE, nq), F32)]
    if with_l:
        shapes += [v((2, 1, nq), F32), v((1, nq), F32)]
    return shapes


def _mla_attn_kernel(q_ref, kc_ref, vc_ref, kl_ref, vl_ref, o_ref, p_ref, g_ref, m_ref, x_ref, acc_ref, *, tk):
    _flash_t(q_ref[0].T, kc_ref, vc_ref, kl_ref, vl_ref, p_ref, g_ref, m_ref, x_ref, acc_ref, tk)
    acc = acc_ref[...]
    o_ref[0] = (acc / acc[0:1, :]).T.astype(BF16)


def _diff_attn_kernel(lam_ref, gs_ref, q1_ref, q2_ref, kc_ref, vc_ref, kl_ref, vl_ref, o_ref,
                      p_ref, g_ref, m_ref, x_ref, acc_ref, ps_ref, l_ref, *, tk):
    tq = q1_ref.shape[1]
    q_t = jnp.concatenate([q1_ref[0].T, q2_ref[0].T], axis=1)
    _flash_t(q_t, kc_ref, vc_ref, kl_ref, vl_ref, p_ref, g_ref, m_ref, x_ref, acc_ref, tk, ps_ref, l_ref)
    o = acc_ref[...] / l_ref[...]
    dl = lam_ref[...]
    lam = (jnp.exp(jnp.sum(dl[0:1] * dl[1:2], axis=-1, keepdims=True))
           - jnp.exp(jnp.sum(dl[2:3] * dl[3:4], axis=-1, keepdims=True)) + LAMBDA_INIT)
    w = o[:, :tq] - lam * o[:, tq:]
    w = w * lax.rsqrt(jnp.mean(w * w, axis=0, keepdims=True) + EPS) * gs_ref[...] * (1.0 - LAMBDA_INIT)
    o_ref[0] = w.T.astype(BF16)


def _mla_attn(q, kc, vc, kl, vl, tq, tk):
    b, s, _ = q.shape
    sc = kc.shape[1]
    assert s % tk == 0 and s % tq == 0
    qs = pl.BlockSpec((1, tq, LANE), lambda bi, hi, qi: (bi, qi, hi))
    cs = pl.BlockSpec((1, sc, LANE), lambda bi, hi, qi: (bi, 0, hi))
    ls = pl.BlockSpec((1, s, LANE), lambda bi, hi, qi: (bi, 0, hi))
    return pl.pallas_call(
        functools.partial(_mla_attn_kernel, tk=tk),
        grid=(b, MLA_HEADS, s // tq),
        in_specs=[qs, cs, cs, ls, ls],
        out_specs=qs,
        out_shape=jax.ShapeDtypeStruct((b, s, MLA_W), BF16),
        scratch_shapes=_attn_scratch(tk, tq, False),
        compiler_params=_cparams(("arbitrary", "arbitrary", "arbitrary")),
        name="mla_attn",
    )(q, kc, vc, kl, vl)


def _diff_attn(lam4, g_subln, dq, kc, vc, kl, vl, tq, tk):
    b, s, _ = dq.shape
    sc = kc.shape[1]
    q1 = pl.BlockSpec((1, tq, LANE), lambda bi, hi, qi: (bi, qi, 2 * hi))
    q2 = pl.BlockSpec((1, tq, LANE), lambda bi, hi, qi: (bi, qi, 2 * hi + 1))
    cs = pl.BlockSpec((1, sc, LANE), lambda bi, hi, qi: (bi, 0, hi))
    ls = pl.BlockSpec((1, s, LANE), lambda bi, hi, qi: (bi, 0, hi))
    return pl.pallas_call(
        functools.partial(_diff_attn_kernel, tk=tk),
        grid=(b, DIFF_HEADS, s // tq),
        in_specs=[_full(lam4.shape), _full(g_subln.shape), q1, q2, cs, cs, ls, ls],
        out_specs=pl.BlockSpec((1, tq, LANE), lambda bi, hi, qi: (bi, qi, hi)),
        out_shape=jax.ShapeDtypeStruct((b, s, DIFF_WIDTH), BF16),
        scratch_shapes=_attn_scratch(tk, 2 * tq, True),
        compiler_params=_cparams(("arbitrary", "arbitrary", "arbitrary")),
        name="diff_attn",
    )(lam4, g_subln, dq, dq, kc, vc, kl, vl)


def _mix_kernel(om_ref, od_ref, g_ref, x_ref, mod_ref, wom_ref, wod_ref, wout_ref, lng_ref, lnb_ref,
                wrh_ref, wrl_ref, x1_ref, ha_ref, hb_ref, lg_ref):
    gm = g_ref[0, :, 0:D_MODEL]
    gd = g_ref[0, :, D_MODEL:2 * D_MODEL]
    y = gm * _dot(om_ref[0], wom_ref[...]) + gd * _dot(od_ref[0], wod_ref[...])
    z = _dot(y.astype(BF16), wout_ref[...])
    g1 = mod_ref[0, :, 2 * D_MODEL:3 * D_MODEL]
    sh2 = mod_ref[0, :, 3 * D_MODEL:4 * D_MODEL]
    sc2 = mod_ref[0, :, 4 * D_MODEL:5 * D_MODEL]
    x1 = _layer_norm(DEEPNORM_ALPHA * x_ref[0] + g1 * z, lng_ref[...], lnb_ref[...])
    x1_ref[0] = x1
    h2 = x1 * (1.0 + sc2) + sh2
    hi = h2.astype(BF16)
    lo = (h2 - hi.astype(F32)).astype(BF16)
    bits = pltpu.bitcast(hi.astype(F32), U32)
    q4 = D_MODEL // 4
    ha_ref[0] = bits[:, 0:q4] | (bits[:, q4:2 * q4] >> 16)
    hb_ref[0] = bits[:, 2 * q4:3 * q4] | (bits[:, 3 * q4:4 * q4] >> 16)
    lg_ref[0] = _dot_nt(wrh_ref[...], hi) + (_dot_nt(wrh_ref[...], lo) + _dot_nt(wrl_ref[...], hi))


def _mix(om, od, gates, x, mod3, wom, wod, wout, lng, lnb, wrh, wrl, tm):
    b, s, _ = x.shape
    row = lambda w: pl.BlockSpec((1, tm, w), lambda bi, ti: (bi, ti, 0))
    return pl.pallas_call(
        _mix_kernel,
        grid=(b, s // tm),
        in_specs=[row(MLA_W), row(DIFF_WIDTH), row(2 * D_MODEL), row(D_MODEL),
                  pl.BlockSpec((1, 1, 6 * D_MODEL), lambda bi, ti: (bi, 0, 0)),
                  _full(wom.shape), _full(wod.shape), _full(wout.shape), _full(lng.shape), _full(lnb.shape),
                  _full(wrh.shape), _full(wrl.shape)],
        out_specs=[row(D_MODEL), row(D_MODEL // 4), row(D_MODEL // 4),
                   pl.BlockSpec((1, N_EXPERTS, tm), lambda bi, ti: (bi, 0, ti))],
        out_shape=[jax.ShapeDtypeStruct((b, s, D_MODEL), F32), jax.ShapeDtypeStruct((b, s, D_MODEL // 4), U32),
                   jax.ShapeDtypeStruct((b, s, D_MODEL // 4), U32), jax.ShapeDtypeStruct((b, N_EXPERTS, s), F32)],
        compiler_params=_cparams(("arbitrary", "arbitrary")),
        name="mix",
    )(om, od, gates, x, mod3, wom, wod, wout, lng, lnb, wrh, wrl)


def _route_kernel(lg_ref, br_ref, pos_ref, gate_ref, st_ref, *, cap, tile):
    e, s = lg_ref.shape[1], lg_ref.shape[2]
    lg = lg_ref[0] + br_ref[...]
    ex = jnp.exp(lg - jnp.max(lg, axis=0, keepdims=True))
    aff = ex / jnp.sum(ex, axis=0, keepdims=True)
    gate_ref[0] = aff
    bits = pltpu.bitcast(aff, I32)

    def count(mask):
        return jnp.sum(mask.astype(I32), axis=1, keepdims=True)

    def search(i, thr):
        cand = thr | (jnp.int32(1) << (30 - i))
        return jnp.where(count(bits >= cand) >= cap, cand, thr)

    thr = lax.fori_loop(0, 31, search, jnp.zeros((e, 1), I32))
    gt = bits > thr
    eq = bits == thr
    need = cap - count(gt)

    r = lax.broadcasted_iota(I32, (LANE, LANE), 0)
    c = lax.broadcasted_iota(I32, (LANE, LANE), 1)
    tri = (r < c).astype(BF16)
    lane = lax.broadcasted_iota(I32, (e, LANE), 1)

    def prefix(mask_fn, emit):
        carry = jnp.zeros((e, 1), F32)
        for j in range(s // LANE):
            m = mask_fn(j)
            emit(j, carry + _dot(m.astype(BF16), tri), m, carry)
            carry = carry + jnp.sum(m.astype(F32), axis=1, keepdims=True)

    def eq_blk(j):
        return eq[:, j * LANE:(j + 1) * LANE]

    def emit_sel(j, rank, m, carry):
        sel = gt[:, j * LANE:(j + 1) * LANE] | (m & (rank < need.astype(F32)))
        pos_ref[0, :, j * LANE:(j + 1) * LANE] = sel.astype(I32)

    prefix(eq_blk, emit_sel)

    starts = [jnp.zeros((e, LANE), I32)]

    def sel_blk(j):
        return pos_ref[0, :, j * LANE:(j + 1) * LANE] > 0

    def emit_pos(j, rank, m, carry):
        if (j * LANE) % tile == 0:
            starts[0] = jnp.where(lane == (j * LANE) // tile, carry.astype(I32), starts[0])
        pos_ref[0, :, j * LANE:(j + 1) * LANE] = jnp.where(m, rank.astype(I32), -1)

    prefix(sel_blk, emit_pos)
    st_ref[0] = jnp.where(lane == s // tile, cap, starts[0])


def _route(lgt, b_router, cap, tile):
    b, e, s = lgt.shape
    blk = pl.BlockSpec((1, e, s), lambda bi: (bi, 0, 0))
    return pl.pallas_call(
        functools.partial(_route_kernel, cap=cap, tile=tile),
        grid=(b,),
        in_specs=[blk, _full((e, 1))],
        out_specs=[blk, blk, pl.BlockSpec((1, e, LANE), lambda bi: (bi, 0, 0))],
        out_shape=[jax.ShapeDtypeStruct((b, e, s), I32), jax.ShapeDtypeStruct((b, e, s), F32),
                   jax.ShapeDtypeStruct((b, e, LANE), I32)],
        compiler_params=_cparams(("arbitrary",)),
        name="route",
    )(lgt, b_router.reshape(e, 1))


SLOT_COLS = 8


def _slots_kernel(st_ref, pos_ref, gate_ref, o_ref, acc_ref, *, chunk, tile):
    bi, ei = pl.program_id(0), pl.program_id(1)
    cap, s = acc_ref.shape[0], pos_ref.shape[3]
    acc_ref[...] = jnp.zeros_like(acc_ref)
    slot = lax.broadcasted_iota(I32, (chunk, tile), 0)
    row = lax.broadcasted_iota(I32, (SLOT_COLS, tile), 0)
    for u in range(s // tile):
        b0, b1 = st_ref[bi, ei, u], st_ref[bi, ei, u + 1]
        cols = slice(u * tile, (u + 1) * tile)

        @pl.when(b1 > b0)
        def _():
            t = lax.broadcasted_iota(I32, (SLOT_COLS, tile), 1) + u * tile
            g = jnp.broadcast_to(gate_ref[0, 0, :, cols], (SLOT_COLS, tile))
            g1 = g.astype(BF16).astype(F32)
            g2 = (g - g1).astype(BF16).astype(F32)
            g3 = g - g1 - g2
            rec = jnp.where(row == 0, (t >> 7).astype(F32),
                            jnp.where(row == 1, (t & 127).astype(F32),
                                      jnp.where(row == 2, g1, jnp.where(row == 3, g2,
                                                                        jnp.where(row == 4, g3, 0.0)))))
            rec = rec.astype(BF16)
            pos = pos_ref[0, 0, :, cols]
            for c in range(cap // chunk):
                lo, hi = c * chunk, (c + 1) * chunk

                @pl.when((b0 < hi) & (b1 > lo))
                def _():
                    hit = (slot == (pos - lo)).astype(BF16)
                    acc_ref[lo:hi, :] += _dot_nt(hit, rec)

    o_ref[0, 0] = acc_ref[...]


def _slots(starts, pos4, gate4, cap, tile, chunk):
    b, e, _, s = pos4.shape
    blk = pl.BlockSpec((1, 1, 1, s), lambda bi, ei, st: (bi, ei, 0, 0))
    gs = pltpu.PrefetchScalarGridSpec(
        num_scalar_prefetch=1,
        grid=(b, e),
        in_specs=[blk, blk],
        out_specs=pl.BlockSpec((1, 1, cap, SLOT_COLS), lambda bi, ei, st: (bi, ei, 0, 0)),
        scratch_shapes=[pltpu.VMEM((cap, SLOT_COLS), F32)],
    )
    return pl.pallas_call(
        functools.partial(_slots_kernel, chunk=chunk, tile=tile),
        grid_spec=gs,
        out_shape=jax.ShapeDtypeStruct((b, e, cap, SLOT_COLS), F32),
        compiler_params=_cparams(("arbitrary", "arbitrary")),
        name="slots",
    )(starts, pos4, gate4)


SC_GATHER_WINDOW = 128


def _sc_gather(table, idx):
    n = idx.shape[0]
    w = table.shape[1]
    mesh = plsc.VectorSubcoreMesh(core_axis_name="core", subcore_axis_name="subcore")

    @pl.kernel(out_type=jax.ShapeDtypeStruct((n, w), table.dtype), mesh=mesh)
    def gather(t_hbm, i_hbm, o_hbm):
        def body(i_vmem, o_vmem):
            pltpu.sync_copy(t_hbm.at[i_vmem.at[0]], o_vmem)

        pltpu.emit_pipeline(
            body,
            grid=(n // SC_GATHER_WINDOW,),
            in_specs=[pl.BlockSpec((1, SC_GATHER_WINDOW), index_map=lambda i: (0, i))],
            out_specs=[pl.BlockSpec((SC_GATHER_WINDOW, w), index_map=lambda i: (i, 0))],
            core_axis_name=("core", "subcore"),
            dimension_semantics=(pltpu.PARALLEL,),
        )(i_hbm, o_hbm)

    return gather(table, idx.reshape(1, n))


def _ffn_kernel(xa_ref, xb_ref, sl_ref, w1_ref, w3_ref, w2_ref, ye_ref):
    def halves(words):
        return pltpu.bitcast(words & jnp.uint32(0xFFFF0000), F32), pltpu.bitcast(words << 16, F32)

    a_hi, a_lo = halves(xa_ref[...])
    b_hi, b_lo = halves(xb_ref[...])
    xe = jnp.concatenate([a_hi, a_lo, b_hi, b_lo], axis=1).astype(BF16)
    sl = sl_ref[0, 0]
    gate = sl[:, 2:3] + sl[:, 3:4] + sl[:, 4:5]
    a = _dot(xe, w1_ref[0])
    hid = ((a / (1.0 + jnp.exp(-a))) * _dot(xe, w3_ref[0])).astype(BF16)
    ye_ref[0, 0] = (_dot(hid, w2_ref[0]) * gate).astype(BF16)


def _ffn(xa, xb, slots, w1, w3, w2):
    b, e, cap, _ = slots.shape
    d = w2.shape[2]
    xspec = pl.BlockSpec((cap, xa.shape[1]), lambda bi, ei: (bi * e + ei, 0))
    wspec = lambda shp: pl.BlockSpec((1,) + shp, lambda bi, ei: (ei, 0, 0))
    return pl.pallas_call(
        _ffn_kernel,
        grid=(b, e),
        in_specs=[xspec, xspec, pl.BlockSpec((1, 1, cap, SLOT_COLS), lambda bi, ei: (bi, ei, 0, 0)),
                  wspec(w1.shape[1:]), wspec(w3.shape[1:]), wspec(w2.shape[1:])],
        out_specs=pl.BlockSpec((1, 1, cap, d), lambda bi, ei: (bi, ei, 0, 0)),
        out_shape=jax.ShapeDtypeStruct((b, e, cap, d), BF16),
        compiler_params=_cparams(("arbitrary", "arbitrary")),
        name="ffn",
    )(xa, xb, slots, w1, w3, w2)


def _combine_kernel(st_ref, ye_ref, pos_ref, x1_ref, mod_ref, lng_ref, lnb_ref, o_ref, acc_ref, *, chunk, sub):
    bi, ti, ei = pl.program_id(0), pl.program_id(1), pl.program_id(2)
    cap, tile = ye_ref.shape[2], x1_ref.shape[1]
    nsub = tile // sub
    bounds = [st_ref[bi, ei, ti * nsub + u] for u in range(nsub + 1)]

    @pl.when(ei == 0)
    def _():
        acc_ref[...] = jnp.zeros_like(acc_ref)

    slot = lax.broadcasted_iota(I32, (chunk, sub), 0)
    for c in range(cap // chunk):
        lo, hi = c * chunk, (c + 1) * chunk

        @pl.when((bounds[0] < hi) & (bounds[nsub] > lo))
        def _():
            for u in range(nsub):
                @pl.when((bounds[u] < hi) & (bounds[u + 1] > lo))
                def _():
                    cols = slice(u * sub, (u + 1) * sub)
                    hit = (slot == (pos_ref[0, 0, :, cols] - lo)).astype(BF16)
                    acc_ref[cols, :] += _dot_tn(hit, ye_ref[0, 0, lo:hi, :])

    @pl.when(ei == pl.num_programs(2) - 1)
    def _():
        g2 = mod_ref[0, :, 5 * D_MODEL:6 * D_MODEL]
        o_ref[0] = _layer_norm(DEEPNORM_ALPHA * x1_ref[0] + g2 * acc_ref[...], lng_ref[...], lnb_ref[...])


def _combine(starts, ye, pos4, x1, mod3, lng, lnb, tile, chunk, sub):
    b, s, d = x1.shape
    e, cap = ye.shape[1], ye.shape[2]
    gs = pltpu.PrefetchScalarGridSpec(
        num_scalar_prefetch=1,
        grid=(b, s // tile, e),
        in_specs=[pl.BlockSpec((1, 1, cap, d), lambda bi, ti, ei, st: (bi, ei, 0, 0)),
                  pl.BlockSpec((1, 1, 1, tile), lambda bi, ti, ei, st: (bi, ei, 0, ti)),
                  pl.BlockSpec((1, tile, d), lambda bi, ti, ei, st: (bi, ti, 0)),
                  pl.BlockSpec((1, 1, 6 * D_MODEL), lambda bi, ti, ei, st: (bi, 0, 0)),
                  pl.BlockSpec((1, d), lambda bi, ti, ei, st: (0, 0)),
                  pl.BlockSpec((1, d), lambda bi, ti, ei, st: (0, 0))],
        out_specs=pl.BlockSpec((1, tile, d), lambda bi, ti, ei, st: (bi, ti, 0)),
        scratch_shapes=[pltpu.VMEM((tile, d), F32)],
    )
    return pl.pallas_call(
        functools.partial(_combine_kernel, chunk=chunk, sub=sub),
        grid_spec=gs,
        out_shape=jax.ShapeDtypeStruct((b, s, d), F32),
        compiler_params=_cparams(("arbitrary", "arbitrary", "arbitrary")),
        name="combine",
    )(starts, ye, pos4, x1, mod3, lng, lnb)


def _head_pad(w, n_heads, width, offset=0):
    k = w.shape[0]
    w3 = w.reshape(k, n_heads, width)
    out = jnp.zeros((k, n_heads, LANE), w.dtype).at[:, :, offset:offset + width].set(w3)
    return out.reshape(k, n_heads * LANE)


def _prep_weights(w_in, w_uq, w_o_mla):
    kr = w_in[:, MLA_KR_OFF:DIFF_Q_OFF]
    dqw = w_in[:, DIFF_Q_OFF:DIFF_K_OFF]
    dkw = w_in[:, DIFF_K_OFF:DIFF_V_OFF]
    dvw = w_in[:, DIFF_V_OFF:GATE_OFF]
    k = w_in.shape[0]

    def rot_groups(w):
        return _rot_cols(w.reshape(k, -1, DIFF_HD)).reshape(k, -1)

    kr_pad = _head_pad(kr, 1, MLA_ROPE, MLA_NOPE)
    krr_pad = _head_pad(_rot_cols(kr), 1, MLA_ROPE, MLA_NOPE)
    wx = jnp.concatenate([w_in[:, MLA_Q_OFF:MLA_KR_OFF], kr_pad, krr_pad, dqw, rot_groups(dqw),
                          dkw, rot_groups(dkw), dvw, w_in[:, GATE_OFF:]], axis=1).astype(BF16)
    wc = jnp.concatenate([w_in[:, MLA_KV_OFF:MLA_KR_OFF], kr_pad, dkw, dvw], axis=1).astype(BF16)
    hd = MLA_NOPE + MLA_ROPE
    uq3 = w_uq.reshape(w_uq.shape[0], MLA_HEADS, hd)
    uq_rot = jnp.concatenate([jnp.zeros_like(uq3[..., :MLA_NOPE]), _rot_cols(uq3[..., MLA_NOPE:])], -1)
    wuq = _head_pad(w_uq, MLA_HEADS, hd).astype(BF16)
    wuqr = _head_pad(uq_rot.reshape(w_uq.shape[0], -1), MLA_HEADS, hd).astype(BF16)
    wom = jnp.zeros((MLA_HEADS, LANE, D_MODEL), w_o_mla.dtype).at[:, MLA_NOPE:, :].set(
        w_o_mla.reshape(MLA_HEADS, MLA_V, D_MODEL)).reshape(MLA_W, D_MODEL).astype(BF16)
    return wx, wc, wuq, wuqr, wom


def kernel(x, c, ctx, c_ctx, w_ada, b_ada, w_in, mla_g_q, mla_w_uq, mla_g_kv, mla_w_ukv, mla_w_o, diff_lambda,
           diff_g_subln, diff_w_o, w_out, ln1_g, ln1_b, moe_w_router, moe_b_router, moe_w1, moe_w3, moe_w2,
           ln2_g, ln2_b):
    b, s, d = x.shape
    assert d == D_MODEL and w_ada.shape[0] == DEPTH == 1 and b <= 4
    cap = EC_CAPACITY * s // N_EXPERTS
    tile = min(1024, s)
    sub = tile
    gather_chunk = min(128, cap)
    scatter_chunk = min(256, cap)
    tm = min(512, s)
    tq = min(2048, s)
    tk = min(2048, s)

    cc = jnp.zeros((8, d), F32).at[:b].set(c).at[b].set(c_ctx)
    mod3 = _ada(cc, w_ada[0], b_ada[0]).reshape(8, 1, 6 * d)

    wx, wc, wuq, wuqr, wom = _prep_weights(w_in[0], mla_w_uq[0], mla_w_o[0])
    gq = mla_g_q[0].reshape(1, -1)
    gkv = mla_g_kv[0].reshape(1, -1)
    wukv = mla_w_ukv[0].astype(BF16)
    tabs = _rope_tables(s)

    q, k, v, dq, dk, dv, gates = _proj_lat(x, mod3, wx, gq, wuq, wuqr, gkv, wukv, tabs, tm)
    kc, vc, dkc, dvc = _proj_ctx(ctx, mod3, wc, gkv, wukv)

    o_mla = _mla_attn(q, kc, vc, k, v, tq, tk)
    o_diff = _diff_attn(diff_lambda[0], diff_g_subln[0].reshape(-1, 1), dq, dkc, dvc, dk, dv, tq // 2, tk)

    wr = moe_w_router[0].T
    wrh = wr.astype(BF16)
    wrl = (wr - wrh.astype(F32)).astype(BF16)
    x1, ha, hb, lgt = _mix(o_mla, o_diff, gates, x, mod3, wom, diff_w_o[0].astype(BF16), w_out[0].astype(BF16),
                       ln1_g[0].reshape(1, -1), ln1_b[0].reshape(1, -1), wrh, wrl, tm)

    pos, gate, starts = _route(lgt, moe_b_router[0], cap, sub)
    pos4 = pos.reshape(b, N_EXPERTS, 1, s)
    gate4 = gate.reshape(b, N_EXPERTS, 1, s)
    slots = _slots(starts, pos4, gate4, cap, tile, gather_chunk)
    row0 = (jnp.arange(b, dtype=I32) * s)[:, None, None]
    idx = (slots[..., 0].astype(I32) * 128 + slots[..., 1].astype(I32) + row0).reshape(-1)
    xa = _sc_gather(ha.reshape(b * s, -1), idx)
    xb = _sc_gather(hb.reshape(b * s, -1), idx)
    ye = _ffn(xa, xb, slots, moe_w1[0].astype(BF16), moe_w3[0].astype(BF16), moe_w2[0].astype(BF16))
    return _combine(starts, ye, pos4, x1, mod3, ln2_g[0].reshape(1, -1), ln2_b[0].reshape(1, -1), tile,
                    scatter_chunk, sub)
```

```python
import functools
import math

import numpy as np
import jax
import jax.numpy as jnp
from jax import lax
from jax.experimental import pallas as pl
from jax.experimental.pallas import tpu as pltpu
from jax.experimental.pallas import tpu_sc as plsc

F32 = jnp.float32
BF16 = jnp.bfloat16
I32 = jnp.int32
U32 = jnp.uint32

D_MODEL = 1024
DEPTH = 1
GRID_W = 64
ROPE_THETA = 10000.0
EPS = 1e-6
MLA_HEADS = 8
MLA_Q_RANK = 256
MLA_KV_RANK = 128
MLA_NOPE = 64
MLA_ROPE = 32
MLA_V = 64
DIFF_HEADS = 4
DIFF_HD = 64
N_EXPERTS = 16
EXPERT_FF = 1024
EC_CAPACITY = 2

MLA_Q_OFF = 0
MLA_KV_OFF = MLA_Q_OFF + MLA_Q_RANK
MLA_KR_OFF = MLA_KV_OFF + MLA_KV_RANK
DIFF_Q_OFF = MLA_KR_OFF + MLA_ROPE
DIFF_K_OFF = DIFF_Q_OFF + DIFF_HEADS * 2 * DIFF_HD
DIFF_V_OFF = DIFF_K_OFF + DIFF_HEADS * 2 * DIFF_HD
GATE_OFF = DIFF_V_OFF + DIFF_HEADS * 2 * DIFF_HD
N_IN = GATE_OFF + 2 * D_MODEL
DIFF_WIDTH = DIFF_HEADS * 2 * DIFF_HD
MLA_SCALE = (MLA_NOPE + MLA_ROPE) ** -0.5
DIFF_SCALE = DIFF_HD ** -0.5
DEEPNORM_ALPHA = (2 * DEPTH) ** 0.25
LAMBDA_INIT = 0.8 - 0.6 * math.exp(-0.3 * 0)
LOG2E = 1.4426950408889634
LAGGED_MAX_HEADROOM = 100.0

LANE = 128
MLA_W = MLA_HEADS * LANE
PX_Q = 0
PX_KV = PX_Q + MLA_Q_RANK
PX_KR = PX_KV + MLA_KV_RANK
PX_DQ = PX_KR + LANE
PX_DK = PX_DQ + DIFF_WIDTH
PX_DV = PX_DK + DIFF_WIDTH
PX_G = PX_DV + DIFF_WIDTH
PX_N = PX_G + 2 * D_MODEL
CX_KV = 0
CX_KR = CX_KV + MLA_KV_RANK
CX_DK = CX_KR + LANE
CX_DV = CX_DK + DIFF_WIDTH
CX_N = CX_DV + DIFF_WIDTH

VMEM_LIMIT = 56 * 1024 * 1024


def _cparams(sem):
    return pltpu.CompilerParams(dimension_semantics=sem, vmem_limit_bytes=VMEM_LIMIT)


def _rope(x, cos, sin_lo, sin_hi, quarter):
    return x * cos + pltpu.roll(x, LANE - quarter, 1) * sin_lo + pltpu.roll(x, quarter, 1) * sin_hi


def _rope_tables(seq):
    t = np.arange(seq)
    row = (t // GRID_W).astype(np.float64)
    col = (t % GRID_W).astype(np.float64)

    def axial(d):
        h = d // 2
        inv = ROPE_THETA ** (-(np.arange(h // 2, dtype=np.float64) * 2.0 / h))
        ar, ac = row[:, None] * inv, col[:, None] * inv
        cos = np.concatenate([np.cos(ar), np.cos(ar), np.cos(ac), np.cos(ac)], -1)
        sin = np.concatenate([np.sin(ar), np.sin(ar), np.sin(ac), np.sin(ac)], -1)
        return cos, sin

    def split(sin, d):
        first = (np.arange(sin.shape[1]) % (d // 2)) < d // 4
        return np.where(first, -sin, 0.0), np.where(first, 0.0, sin)

    c32, s32 = axial(MLA_ROPE)
    c64, s64 = axial(DIFF_HD)
    cos_m = np.ones((seq, LANE))
    sin_m = np.zeros((seq, LANE))
    cos_m[:, MLA_NOPE:MLA_NOPE + MLA_ROPE] = c32
    sin_m[:, MLA_NOPE:MLA_NOPE + MLA_ROPE] = s32
    cos_d = np.concatenate([c64, c64], -1)
    sin_d = np.concatenate([s64, s64], -1)
    tabs = (cos_m, *split(sin_m, MLA_ROPE), cos_d, *split(sin_d, DIFF_HD))
    return tuple(jnp.asarray(a, F32) for a in tabs)


def _rms(x, g):
    return x * lax.rsqrt(jnp.mean(x * x, axis=-1, keepdims=True) + EPS) * g


def _layer_norm(x, g, b):
    mu = jnp.mean(x, axis=-1, keepdims=True)
    xc = x - mu
    var = jnp.mean(xc * xc, axis=-1, keepdims=True)
    return xc * lax.rsqrt(var + EPS) * g + b


def _dot(a, b):
    return jnp.dot(a, b, preferred_element_type=F32)


def _dot_nt(a, b):
    return lax.dot_general(a, b, (((1,), (1,)), ((), ())), preferred_element_type=F32)


def _dot_tn(a, b):
    return lax.dot_general(a, b, (((0,), (0,)), ((), ())), preferred_element_type=F32)


def _ada_kernel(c_ref, w_ref, b_ref, o_ref):
    c = c_ref[...]
    s = (c / (1.0 + jnp.exp(-c))).astype(BF16)
    o_ref[...] = _dot(s, w_ref[...].astype(BF16)) + b_ref[...]


def _ada(cc, w_ada, b_ada):
    n = w_ada.shape[1]
    tn = 1024
    return pl.pallas_call(
        _ada_kernel,
        grid=(n // tn,),
        in_specs=[pl.BlockSpec((8, D_MODEL), lambda j: (0, 0)),
                  pl.BlockSpec((D_MODEL, tn), lambda j: (0, j)),
                  pl.BlockSpec((1, tn), lambda j: (0, j))],
        out_specs=pl.BlockSpec((8, tn), lambda j: (0, j)),
        out_shape=jax.ShapeDtypeStruct((8, n), F32),
        compiler_params=_cparams(("arbitrary",)),
        name="ada",
    )(cc, w_ada, b_ada.reshape(1, n))


def _proj_lat_kernel(x_ref, mod_ref, w_ref, gq_ref, wuq_ref, gkv_ref, wukv_ref,
                     cm_ref, slm_ref, shm_ref, cd_ref, sld_ref, shd_ref,
                     q_ref, k_ref, v_ref, dq_ref, dk_ref, dv_ref, g_ref):
    tm = x_ref.shape[1]
    sh = mod_ref[0, :, 0:D_MODEL]
    sc = mod_ref[0, :, D_MODEL:2 * D_MODEL]
    h = (x_ref[0] * (1.0 + sc) + sh).astype(BF16)

    def proj(a, n):
        return _dot(h, w_ref[:, a:a + n])

    lane = lax.broadcasted_iota(I32, (tm, LANE), 1)
    lo = lane < MLA_NOPE
    one0 = (lane == 0).astype(F32)
    def rope_m(t):
        return _rope(t, cm_ref[...], slm_ref[...], shm_ref[...], MLA_ROPE // 4)

    def rope_d(t):
        return _rope(t, cd_ref[...], sld_ref[...], shd_ref[...], DIFF_HD // 4)

    cq = _rms(proj(PX_Q, MLA_Q_RANK), gq_ref[...]).astype(BF16)
    q = _dot(cq, wuq_ref[...])
    for hd in range(MLA_HEADS):
        sl = slice(hd * LANE, (hd + 1) * LANE)
        q_ref[0, :, sl] = (rope_m(q[:, sl]) * (MLA_SCALE * LOG2E)).astype(BF16)

    ckv = _rms(proj(PX_KV, MLA_KV_RANK), gkv_ref[...]).astype(BF16)
    kv = _dot(ckv, wukv_ref[...])
    kr = rope_m(proj(PX_KR, LANE))
    for hd in range(MLA_HEADS):
        sl = slice(hd * LANE, (hd + 1) * LANE)
        kvh = kv[:, sl]
        k_ref[0, :, sl] = jnp.where(lo, kvh, kr).astype(BF16)
        v_ref[0, :, sl] = jnp.where(lo, one0, kvh).astype(BF16)

    dq = proj(PX_DQ, DIFF_WIDTH)
    dk = proj(PX_DK, DIFF_WIDTH)
    for hd in range(DIFF_HEADS):
        sl = slice(hd * LANE, (hd + 1) * LANE)
        qb = rope_d(dq[:, sl]) * (DIFF_SCALE * LOG2E)
        dq_ref[0, :, (2 * hd) * LANE:(2 * hd + 1) * LANE] = jnp.where(lo, qb, 0.0).astype(BF16)
        dq_ref[0, :, (2 * hd + 1) * LANE:(2 * hd + 2) * LANE] = jnp.where(lo, 0.0, qb).astype(BF16)
        dk_ref[0, :, sl] = rope_d(dk[:, sl]).astype(BF16)
    dv_ref[0] = proj(PX_DV, DIFF_WIDTH).astype(BF16)
    pg = proj(PX_G, 2 * D_MODEL)
    g_ref[0] = (1.0 / (1.0 + jnp.exp(-pg))).astype(BF16)


def _proj_ctx_kernel(x_ref, mod_ref, w_ref, gkv_ref, wukv_ref, k_ref, v_ref, dk_ref, dv_ref):
    tm = x_ref.shape[1]
    sh = mod_ref[0, :, 0:D_MODEL]
    sc = mod_ref[0, :, D_MODEL:2 * D_MODEL]
    h = (x_ref[0] * (1.0 + sc) + sh).astype(BF16)

    def proj(a, n):
        return _dot(h, w_ref[:, a:a + n])

    lane = lax.broadcasted_iota(I32, (tm, LANE), 1)
    lo = lane < MLA_NOPE
    one0 = (lane == 0).astype(F32)
    ckv = _rms(proj(CX_KV, MLA_KV_RANK), gkv_ref[...]).astype(BF16)
    kv = _dot(ckv, wukv_ref[...])
    kr = proj(CX_KR, LANE)
    for hd in range(MLA_HEADS):
        sl = slice(hd * LANE, (hd + 1) * LANE)
        kvh = kv[:, sl]
        k_ref[0, :, sl] = jnp.where(lo, kvh, kr).astype(BF16)
        v_ref[0, :, sl] = jnp.where(lo, one0, kvh).astype(BF16)
    dk_ref[0] = proj(CX_DK, DIFF_WIDTH).astype(BF16)
    dv_ref[0] = proj(CX_DV, DIFF_WIDTH).astype(BF16)


def _full(shape):
    nd = len(shape)
    return pl.BlockSpec(shape, lambda *_: (0,) * nd)


def _proj_lat(x, mod3, wx, gq, wuq, gkv, wukv, tabs, tm):
    b, s, _ = x.shape
    row = lambda w: pl.BlockSpec((1, tm, w), lambda bi, ti: (bi, ti, 0))
    tab = pl.BlockSpec((tm, LANE), lambda bi, ti: (ti, 0))
    outs = [(MLA_W, BF16), (MLA_W, BF16), (MLA_W, BF16), (2 * DIFF_WIDTH, BF16), (DIFF_WIDTH, BF16),
            (DIFF_WIDTH, BF16), (2 * D_MODEL, BF16)]
    return pl.pallas_call(
        _proj_lat_kernel,
        grid=(b, s // tm),
        in_specs=[row(D_MODEL),
                  pl.BlockSpec((1, 1, 6 * D_MODEL), lambda bi, ti: (bi, 0, 0)),
                  _full(wx.shape), _full(gq.shape), _full(wuq.shape),
                  _full(gkv.shape), _full(wukv.shape)] + [tab] * len(tabs),
        out_specs=[row(w) for w, _ in outs],
        out_shape=[jax.ShapeDtypeStruct((b, s, w), dt) for w, dt in outs],
        compiler_params=_cparams(("arbitrary", "arbitrary")),
        name="proj_lat",
    )(x, mod3, wx, gq, wuq, gkv, wukv, *tabs)


def _proj_ctx(ctx, mod3, wc, gkv, wukv):
    b, s, _ = ctx.shape
    row = lambda w: pl.BlockSpec((1, s, w), lambda bi: (bi, 0, 0))
    outs = [MLA_W, MLA_W, DIFF_WIDTH, DIFF_WIDTH]
    return pl.pallas_call(
        _proj_ctx_kernel,
        grid=(b,),
        in_specs=[row(D_MODEL),
                  pl.BlockSpec((1, 1, 6 * D_MODEL), lambda bi: (b, 0, 0)),
                  _full(wc.shape), _full(gkv.shape), _full(wukv.shape)],
        out_specs=[row(w) for w in outs],
        out_shape=[jax.ShapeDtypeStruct((b, s, w), BF16) for w in outs],
        compiler_params=_cparams(("arbitrary",)),
        name="proj_ctx",
    )(ctx, mod3, wc, gkv, wukv)


ONES_ROWS = 16


def _flash_t(q_t, kc_ref, vc_ref, kl_ref, vl_ref, p_ref, g_ref, m_ref, x_ref, acc_ref, tk, ones_rows):
    n = kl_ref.shape[1] // tk

    def chunk(ref, i):
        return ref[0, pl.ds(pl.multiple_of(i * tk, tk), tk), :]

    def pv(v, p):
        if ones_rows:
            return _dot(jnp.concatenate([v.T, jnp.ones((ONES_ROWS, v.shape[0]), BF16)], axis=0), p)
        return _dot_tn(v, p)

    def scores_exp(i):
        stab = m_ref[...]
        s_t = _dot(chunk(kl_ref, i), q_t)
        p_ref[i % 2] = jnp.exp2(s_t - stab).astype(BF16)
        cmax = jnp.max(s_t, axis=0, keepdims=True)
        m_new = jnp.maximum(stab, cmax)
        g_ref[(i + 1) % 2] = jnp.exp2(stab - m_new)
        m_ref[...] = m_new
        x_ref[...] = jnp.maximum(x_ref[...], cmax - stab)

    def values(i):
        acc_ref[...] = g_ref[i % 2] * acc_ref[...] + pv(chunk(vl_ref, i), p_ref[i % 2])

    def exact_chunk(k, v, m, first):
        out = pv(v, jnp.exp2(_dot(k, q_t) - m).astype(BF16))
        acc_ref[...] = out if first else acc_ref[...] + out

    m0 = jnp.max(_dot(kc_ref[0], q_t), axis=0, keepdims=True)
    m_ref[...] = m0
    g_ref[0] = jnp.ones_like(m0)
    x_ref[...] = jnp.zeros_like(m0)
    exact_chunk(kc_ref[0], vc_ref[0], m0, True)

    scores_exp(0)

    def body(i, carry):
        values(i - 1)
        scores_exp(i)
        return carry

    lax.fori_loop(1, n, body, 0)
    values(n - 1)

    @pl.when(jnp.max(x_ref[...]) > LAGGED_MAX_HEADROOM)
    def _():
        m = m_ref[...]
        exact_chunk(kc_ref[0], vc_ref[0], m, True)

        def redo(i, carry):
            exact_chunk(chunk(kl_ref, i), chunk(vl_ref, i), m, False)
            return carry

        lax.fori_loop(0, n, redo, 0)


def _attn_scratch(tk, nq, ones_rows):
    v = pltpu.VMEM
    rows = LANE + (ONES_ROWS if ones_rows else 0)
    return [v((2, tk, nq), BF16), v((2, 1, nq), F32), v((1, nq), F32), v((1, nq), F32), v((rows, nq), F32)]


def _mla_attn_kernel(q_ref, kc_ref, vc_ref, kl_ref, vl_ref, o_ref, p_ref, g_ref, m_ref, x_ref, acc_ref, *, tk):
    _flash_t(q_ref[0].T, kc_ref, vc_ref, kl_ref, vl_ref, p_ref, g_ref, m_ref, x_ref, acc_ref, tk, False)
    acc = acc_ref[...]
    o_ref[0] = (acc / acc[0:1, :]).T.astype(BF16)


def _diff_attn_kernel(lam_ref, gs_ref, q1_ref, q2_ref, kc_ref, vc_ref, kl_ref, vl_ref, o_ref,
                      p_ref, g_ref, m_ref, x_ref, acc_ref, *, tk):
    tq = q1_ref.shape[1]
    q_t = jnp.concatenate([q1_ref[0].T, q2_ref[0].T], axis=1)
    _flash_t(q_t, kc_ref, vc_ref, kl_ref, vl_ref, p_ref, g_ref, m_ref, x_ref, acc_ref, tk, True)
    o = acc_ref[0:LANE, :] / acc_ref[LANE:LANE + 1, :]
    dl = lam_ref[...]
    lam = (jnp.exp(jnp.sum(dl[0:1] * dl[1:2], axis=-1, keepdims=True))
           - jnp.exp(jnp.sum(dl[2:3] * dl[3:4], axis=-1, keepdims=True)) + LAMBDA_INIT)
    w = o[:, :tq] - lam * o[:, tq:]
    w = w * lax.rsqrt(jnp.mean(w * w, axis=0, keepdims=True) + EPS) * gs_ref[...] * (1.0 - LAMBDA_INIT)
    o_ref[0] = w.T.astype(BF16)


def _mla_attn(q, kc, vc, kl, vl, tq, tk):
    b, s, _ = q.shape
    sc = kc.shape[1]
    assert s % tk == 0 and s % tq == 0
    qs = pl.BlockSpec((1, tq, LANE), lambda bi, hi, qi: (bi, qi, hi))
    cs = pl.BlockSpec((1, sc, LANE), lambda bi, hi, qi: (bi, 0, hi))
    ls = pl.BlockSpec((1, s, LANE), lambda bi, hi, qi: (bi, 0, hi))
    return pl.pallas_call(
        functools.partial(_mla_attn_kernel, tk=tk),
        grid=(b, MLA_HEADS, s // tq),
        in_specs=[qs, cs, cs, ls, ls],
        out_specs=qs,
        out_shape=jax.ShapeDtypeStruct((b, s, MLA_W), BF16),
        scratch_shapes=_attn_scratch(tk, tq, False),
        compiler_params=_cparams(("arbitrary", "arbitrary", "arbitrary")),
        name="mla_attn",
    )(q, kc, vc, kl, vl)


def _diff_attn(lam4, g_subln, dq, kc, vc, kl, vl, tq, tk):
    b, s, _ = dq.shape
    sc = kc.shape[1]
    q1 = pl.BlockSpec((1, tq, LANE), lambda bi, hi, qi: (bi, qi, 2 * hi))
    q2 = pl.BlockSpec((1, tq, LANE), lambda bi, hi, qi: (bi, qi, 2 * hi + 1))
    cs = pl.BlockSpec((1, sc, LANE), lambda bi, hi, qi: (bi, 0, hi))
    ls = pl.BlockSpec((1, s, LANE), lambda bi, hi, qi: (bi, 0, hi))
    return pl.pallas_call(
        functools.partial(_diff_attn_kernel, tk=tk),
        grid=(b, DIFF_HEADS, s // tq),
        in_specs=[_full(lam4.shape), _full(g_subln.shape), q1, q2, cs, cs, ls, ls],
        out_specs=pl.BlockSpec((1, tq, LANE), lambda bi, hi, qi: (bi, qi, hi)),
        out_shape=jax.ShapeDtypeStruct((b, s, DIFF_WIDTH), BF16),
        scratch_shapes=_attn_scratch(tk, 2 * tq, True),
        compiler_params=_cparams(("arbitrary", "arbitrary", "arbitrary")),
        name="diff_attn",
    )(lam4, g_subln, dq, dq, kc, vc, kl, vl)


def _mix_kernel(om_ref, od_ref, g_ref, x_ref, mod_ref, wom_ref, wod_ref, wout_ref, lng_ref, lnb_ref,
                wrh_ref, wrl_ref, x1_ref, ha_ref, hb_ref, lg_ref):
    gm = g_ref[0, :, 0:D_MODEL]
    gd = g_ref[0, :, D_MODEL:2 * D_MODEL]
    y = gm * _dot(om_ref[0], wom_ref[...]) + gd * _dot(od_ref[0], wod_ref[...])
    z = _dot(y.astype(BF16), wout_ref[...])
    g1 = mod_ref[0, :, 2 * D_MODEL:3 * D_MODEL]
    sh2 = mod_ref[0, :, 3 * D_MODEL:4 * D_MODEL]
    sc2 = mod_ref[0, :, 4 * D_MODEL:5 * D_MODEL]
    x1 = _layer_norm(DEEPNORM_ALPHA * x_ref[0] + g1 * z, lng_ref[...], lnb_ref[...])
    x1_ref[0] = x1
    h2 = x1 * (1.0 + sc2) + sh2
    hi = h2.astype(BF16)
    lo = (h2 - hi.astype(F32)).astype(BF16)
    bits = pltpu.bitcast(hi.astype(F32), U32)
    q4 = D_MODEL // 4
    ha_ref[0] = bits[:, 0:q4] | (bits[:, q4:2 * q4] >> 16)
    hb_ref[0] = bits[:, 2 * q4:3 * q4] | (bits[:, 3 * q4:4 * q4] >> 16)
    lg_ref[0] = _dot_nt(wrh_ref[...], hi) + (_dot_nt(wrh_ref[...], lo) + _dot_nt(wrl_ref[...], hi))


def _mix(om, od, gates, x, mod3, wom, wod, wout, lng, lnb, wrh, wrl, tm):
    b, s, _ = x.shape
    row = lambda w: pl.BlockSpec((1, tm, w), lambda bi, ti: (bi, ti, 0))
    return pl.pallas_call(
        _mix_kernel,
        grid=(b, s // tm),
        in_specs=[row(MLA_W), row(DIFF_WIDTH), row(2 * D_MODEL), row(D_MODEL),
                  pl.BlockSpec((1, 1, 6 * D_MODEL), lambda bi, ti: (bi, 0, 0)),
                  _full(wom.shape), _full(wod.shape), _full(wout.shape), _full(lng.shape), _full(lnb.shape),
                  _full(wrh.shape), _full(wrl.shape)],
        out_specs=[row(D_MODEL), row(D_MODEL // 4), row(D_MODEL // 4),
                   pl.BlockSpec((1, N_EXPERTS, tm), lambda bi, ti: (bi, 0, ti))],
        out_shape=[jax.ShapeDtypeStruct((b, s, D_MODEL), F32), jax.ShapeDtypeStruct((b, s, D_MODEL // 4), U32),
                   jax.ShapeDtypeStruct((b, s, D_MODEL // 4), U32), jax.ShapeDtypeStruct((b, N_EXPERTS, s), F32)],
        compiler_params=_cparams(("arbitrary", "arbitrary")),
        name="mix",
    )(om, od, gates, x, mod3, wom, wod, wout, lng, lnb, wrh, wrl)


def _route_kernel(lg_ref, br_ref, pos_ref, gate_ref, st_ref, *, cap, tile):
    e, s = lg_ref.shape[1], lg_ref.shape[2]
    lg = lg_ref[0] + br_ref[...]
    ex = jnp.exp(lg - jnp.max(lg, axis=0, keepdims=True))
    aff = ex / jnp.sum(ex, axis=0, keepdims=True)
    gate_ref[0] = aff
    bits = pltpu.bitcast(aff, I32)

    def count(mask):
        return jnp.sum(mask.astype(I32), axis=1, keepdims=True)

    def search(i, thr):
        cand = thr | (jnp.int32(1) << (30 - i))
        return jnp.where(count(bits >= cand) >= cap, cand, thr)

    thr = lax.fori_loop(0, 31, search, jnp.zeros((e, 1), I32))
    gt = bits > thr
    eq = bits == thr
    need = cap - count(gt)

    r = lax.broadcasted_iota(I32, (LANE, LANE), 0)
    c = lax.broadcasted_iota(I32, (LANE, LANE), 1)
    tri = (r < c).astype(BF16)
    lane = lax.broadcasted_iota(I32, (e, LANE), 1)

    def prefix(mask_fn, emit):
        carry = jnp.zeros((e, 1), F32)
        for j in range(s // LANE):
            m = mask_fn(j)
            emit(j, carry + _dot(m.astype(BF16), tri), m, carry)
            carry = carry + jnp.sum(m.astype(F32), axis=1, keepdims=True)

    def eq_blk(j):
        return eq[:, j * LANE:(j + 1) * LANE]

    def emit_sel(j, rank, m, carry):
        sel = gt[:, j * LANE:(j + 1) * LANE] | (m & (rank < need.astype(F32)))
        pos_ref[0, :, j * LANE:(j + 1) * LANE] = sel.astype(I32)

    prefix(eq_blk, emit_sel)

    starts = [jnp.zeros((e, LANE), I32)]

    def sel_blk(j):
        return pos_ref[0, :, j * LANE:(j + 1) * LANE] > 0

    def emit_pos(j, rank, m, carry):
        if (j * LANE) % tile == 0:
            starts[0] = jnp.where(lane == (j * LANE) // tile, carry.astype(I32), starts[0])
        pos_ref[0, :, j * LANE:(j + 1) * LANE] = jnp.where(m, rank.astype(I32), -1)

    prefix(sel_blk, emit_pos)
    st_ref[0] = jnp.where(lane == s // tile, cap, starts[0])


def _route(lgt, b_router, cap, tile):
    b, e, s = lgt.shape
    blk = pl.BlockSpec((1, e, s), lambda bi: (bi, 0, 0))
    return pl.pallas_call(
        functools.partial(_route_kernel, cap=cap, tile=tile),
        grid=(b,),
        in_specs=[blk, _full((e, 1))],
        out_specs=[blk, blk, pl.BlockSpec((1, e, LANE), lambda bi: (bi, 0, 0))],
        out_shape=[jax.ShapeDtypeStruct((b, e, s), I32), jax.ShapeDtypeStruct((b, e, s), F32),
                   jax.ShapeDtypeStruct((b, e, LANE), I32)],
        compiler_params=_cparams(("arbitrary",)),
        name="route",
    )(lgt, b_router.reshape(e, 1))


SLOT_COLS = 8


def _slots_kernel(st_ref, pos_ref, gate_ref, o_ref, acc_ref, *, chunk, tile):
    bi, ei = pl.program_id(0), pl.program_id(1)
    cap, s = acc_ref.shape[0], pos_ref.shape[3]
    acc_ref[...] = jnp.zeros_like(acc_ref)
    slot = lax.broadcasted_iota(I32, (chunk, tile), 0)
    row = lax.broadcasted_iota(I32, (SLOT_COLS, tile), 0)
    for u in range(s // tile):
        b0, b1 = st_ref[bi, ei, u], st_ref[bi, ei, u + 1]
        cols = slice(u * tile, (u + 1) * tile)

        @pl.when(b1 > b0)
        def _():
            t = lax.broadcasted_iota(I32, (SLOT_COLS, tile), 1) + u * tile
            g = jnp.broadcast_to(gate_ref[0, 0, :, cols], (SLOT_COLS, tile))
            g1 = g.astype(BF16).astype(F32)
            g2 = (g - g1).astype(BF16).astype(F32)
            g3 = g - g1 - g2
            rec = jnp.where(row == 0, (t >> 7).astype(F32),
                            jnp.where(row == 1, (t & 127).astype(F32),
                                      jnp.where(row == 2, g1, jnp.where(row == 3, g2,
                                                                        jnp.where(row == 4, g3, 0.0)))))
            rec = rec.astype(BF16)
            pos = pos_ref[0, 0, :, cols]
            for c in range(cap // chunk):
                lo, hi = c * chunk, (c + 1) * chunk

                @pl.when((b0 < hi) & (b1 > lo))
                def _():
                    hit = (slot == (pos - lo)).astype(BF16)
                    acc_ref[lo:hi, :] += _dot_nt(hit, rec)

    o_ref[0, 0] = acc_ref[...]


def _slots(starts, pos4, gate4, cap, tile, chunk):
    b, e, _, s = pos4.shape
    blk = pl.BlockSpec((1, 1, 1, s), lambda bi, ei, st: (bi, ei, 0, 0))
    gs = pltpu.PrefetchScalarGridSpec(
        num_scalar_prefetch=1,
        grid=(b, e),
        in_specs=[blk, blk],
        out_specs=pl.BlockSpec((1, 1, cap, SLOT_COLS), lambda bi, ei, st: (bi, ei, 0, 0)),
        scratch_shapes=[pltpu.VMEM((cap, SLOT_COLS), F32)],
    )
    return pl.pallas_call(
        functools.partial(_slots_kernel, chunk=chunk, tile=tile),
        grid_spec=gs,
        out_shape=jax.ShapeDtypeStruct((b, e, cap, SLOT_COLS), F32),
        compiler_params=_cparams(("arbitrary", "arbitrary")),
        name="slots",
    )(starts, pos4, gate4)


SC_GATHER_WINDOW = 128


def _sc_gather(table, idx):
    n = idx.shape[0]
    w = table.shape[1]
    mesh = plsc.VectorSubcoreMesh(core_axis_name="core", subcore_axis_name="subcore")

    @pl.kernel(out_type=jax.ShapeDtypeStruct((n, w), table.dtype), mesh=mesh)
    def gather(t_hbm, i_hbm, o_hbm):
        def body(i_vmem, o_vmem):
            pltpu.sync_copy(t_hbm.at[i_vmem.at[0]], o_vmem)

        pltpu.emit_pipeline(
            body,
            grid=(n // SC_GATHER_WINDOW,),
            in_specs=[pl.BlockSpec((1, SC_GATHER_WINDOW), index_map=lambda i: (0, i))],
            out_specs=[pl.BlockSpec((SC_GATHER_WINDOW, w), index_map=lambda i: (i, 0))],
            core_axis_name=("core", "subcore"),
            dimension_semantics=(pltpu.PARALLEL,),
        )(i_hbm, o_hbm)

    return gather(table, idx.reshape(1, n))


def _ffn_kernel(xa_ref, xb_ref, sl_ref, w1_ref, w3_ref, w2_ref, ye_ref):
    def halves(words):
        return pltpu.bitcast(words & jnp.uint32(0xFFFF0000), F32), pltpu.bitcast(words << 16, F32)

    a_hi, a_lo = halves(xa_ref[...])
    b_hi, b_lo = halves(xb_ref[...])
    xe = jnp.concatenate([a_hi, a_lo, b_hi, b_lo], axis=1).astype(BF16)
    sl = sl_ref[0, 0]
    gate = sl[:, 2:3] + sl[:, 3:4] + sl[:, 4:5]
    a = _dot(xe, w1_ref[0])
    hid = ((a / (1.0 + jnp.exp(-a))) * _dot(xe, w3_ref[0])).astype(BF16)
    ye_ref[0, 0] = (_dot(hid, w2_ref[0]) * gate).astype(BF16)


def _ffn(xa, xb, slots, w1, w3, w2):
    b, e, cap, _ = slots.shape
    d = w2.shape[2]
    xspec = pl.BlockSpec((cap, xa.shape[1]), lambda bi, ei: (bi * e + ei, 0))
    wspec = lambda shp: pl.BlockSpec((1,) + shp, lambda bi, ei: (ei, 0, 0))
    return pl.pallas_call(
        _ffn_kernel,
        grid=(b, e),
        in_specs=[xspec, xspec, pl.BlockSpec((1, 1, cap, SLOT_COLS), lambda bi, ei: (bi, ei, 0, 0)),
                  wspec(w1.shape[1:]), wspec(w3.shape[1:]), wspec(w2.shape[1:])],
        out_specs=pl.BlockSpec((1, 1, cap, d), lambda bi, ei: (bi, ei, 0, 0)),
        out_shape=jax.ShapeDtypeStruct((b, e, cap, d), BF16),
        compiler_params=_cparams(("arbitrary", "arbitrary")),
        name="ffn",
    )(xa, xb, slots, w1, w3, w2)


def _combine_kernel(st_ref, ye_ref, pos_ref, x1_ref, mod_ref, lng_ref, lnb_ref, o_ref, acc_ref, *, chunk, sub):
    bi, ti, ei = pl.program_id(0), pl.program_id(1), pl.program_id(2)
    cap, tile = ye_ref.shape[2], x1_ref.shape[1]
    nsub = tile // sub
    bounds = [st_ref[bi, ei, ti * nsub + u] for u in range(nsub + 1)]

    @pl.when(ei == 0)
    def _():
        acc_ref[...] = jnp.zeros_like(acc_ref)

    slot = lax.broadcasted_iota(I32, (chunk, sub), 0)
    for c in range(cap // chunk):
        lo, hi = c * chunk, (c + 1) * chunk

        @pl.when((bounds[0] < hi) & (bounds[nsub] > lo))
        def _():
            for u in range(nsub):
                @pl.when((bounds[u] < hi) & (bounds[u + 1] > lo))
                def _():
                    cols = slice(u * sub, (u + 1) * sub)
                    hit = (slot == (pos_ref[0, 0, :, cols] - lo)).astype(BF16)
                    acc_ref[cols, :] += _dot_tn(hit, ye_ref[0, 0, lo:hi, :])

    @pl.when(ei == pl.num_programs(2) - 1)
    def _():
        g2 = mod_ref[0, :, 5 * D_MODEL:6 * D_MODEL]
        o_ref[0] = _layer_norm(DEEPNORM_ALPHA * x1_ref[0] + g2 * acc_ref[...], lng_ref[...], lnb_ref[...])


def _combine(starts, ye, pos4, x1, mod3, lng, lnb, tile, chunk, sub):
    b, s, d = x1.shape
    e, cap = ye.shape[1], ye.shape[2]
    gs = pltpu.PrefetchScalarGridSpec(
        num_scalar_prefetch=1,
        grid=(b, s // tile, e),
        in_specs=[pl.BlockSpec((1, 1, cap, d), lambda bi, ti, ei, st: (bi, ei, 0, 0)),
                  pl.BlockSpec((1, 1, 1, tile), lambda bi, ti, ei, st: (bi, ei, 0, ti)),
                  pl.BlockSpec((1, tile, d), lambda bi, ti, ei, st: (bi, ti, 0)),
                  pl.BlockSpec((1, 1, 6 * D_MODEL), lambda bi, ti, ei, st: (bi, 0, 0)),
                  pl.BlockSpec((1, d), lambda bi, ti, ei, st: (0, 0)),
                  pl.BlockSpec((1, d), lambda bi, ti, ei, st: (0, 0))],
        out_specs=pl.BlockSpec((1, tile, d), lambda bi, ti, ei, st: (bi, ti, 0)),
        scratch_shapes=[pltpu.VMEM((tile, d), F32)],
    )
    return pl.pallas_call(
        functools.partial(_combine_kernel, chunk=chunk, sub=sub),
        grid_spec=gs,
        out_shape=jax.ShapeDtypeStruct((b, s, d), F32),
        compiler_params=_cparams(("arbitrary", "arbitrary", "arbitrary")),
        name="combine",
    )(starts, ye, pos4, x1, mod3, lng, lnb)


def _head_pad(w, n_heads, width, offset=0):
    k = w.shape[0]
    w3 = w.reshape(k, n_heads, width)
    out = jnp.zeros((k, n_heads, LANE), w.dtype).at[:, :, offset:offset + width].set(w3)
    return out.reshape(k, n_heads * LANE)


def _prep_weights(w_in, w_uq, w_o_mla):
    kr = w_in[:, MLA_KR_OFF:DIFF_Q_OFF]
    dqw = w_in[:, DIFF_Q_OFF:DIFF_K_OFF]
    dkw = w_in[:, DIFF_K_OFF:DIFF_V_OFF]
    dvw = w_in[:, DIFF_V_OFF:GATE_OFF]
    kr_pad = _head_pad(kr, 1, MLA_ROPE, MLA_NOPE)
    wx = jnp.concatenate([w_in[:, MLA_Q_OFF:MLA_KR_OFF], kr_pad, dqw, dkw, dvw, w_in[:, GATE_OFF:]],
                         axis=1).astype(BF16)
    wc = jnp.concatenate([w_in[:, MLA_KV_OFF:MLA_KR_OFF], kr_pad, dkw, dvw], axis=1).astype(BF16)
    wuq = _head_pad(w_uq, MLA_HEADS, MLA_NOPE + MLA_ROPE).astype(BF16)
    wom = jnp.zeros((MLA_HEADS, LANE, D_MODEL), w_o_mla.dtype).at[:, MLA_NOPE:, :].set(
        w_o_mla.reshape(MLA_HEADS, MLA_V, D_MODEL)).reshape(MLA_W, D_MODEL).astype(BF16)
    return wx, wc, wuq, wom


def kernel(x, c, ctx, c_ctx, w_ada, b_ada, w_in, mla_g_q, mla_w_uq, mla_g_kv, mla_w_ukv, mla_w_o, diff_lambda,
           diff_g_subln, diff_w_o, w_out, ln1_g, ln1_b, moe_w_router, moe_b_router, moe_w1, moe_w3, moe_w2,
           ln2_g, ln2_b):
    b, s, d = x.shape
    assert d == D_MODEL and w_ada.shape[0] == DEPTH == 1 and b <= 4
    cap = EC_CAPACITY * s // N_EXPERTS
    tile = min(1024, s)
    sub = tile
    gather_chunk = min(128, cap)
    scatter_chunk = min(256, cap)
    tm = min(512, s)
    tq = min(2048, s)
    tk = min(2048, s)

    cc = jnp.zeros((8, d), F32).at[:b].set(c).at[b].set(c_ctx)
    mod3 = _ada(cc, w_ada[0], b_ada[0]).reshape(8, 1, 6 * d)

    wx, wc, wuq, wom = _prep_weights(w_in[0], mla_w_uq[0], mla_w_o[0])
    gq = mla_g_q[0].reshape(1, -1)
    gkv = mla_g_kv[0].reshape(1, -1)
    wukv = mla_w_ukv[0].astype(BF16)
    tabs = _rope_tables(s)

    q, k, v, dq, dk, dv, gates = _proj_lat(x, mod3, wx, gq, wuq, gkv, wukv, tabs, tm)
    kc, vc, dkc, dvc = _proj_ctx(ctx, mod3, wc, gkv, wukv)

    o_mla = _mla_attn(q, kc, vc, k, v, tq, tk)
    o_diff = _diff_attn(diff_lambda[0], diff_g_subln[0].reshape(-1, 1), dq, dkc, dvc, dk, dv, tq // 2, tk)

    wr = moe_w_router[0].T
    wrh = wr.astype(BF16)
    wrl = (wr - wrh.astype(F32)).astype(BF16)
    x1, ha, hb, lgt = _mix(o_mla, o_diff, gates, x, mod3, wom, diff_w_o[0].astype(BF16), w_out[0].astype(BF16),
                       ln1_g[0].reshape(1, -1), ln1_b[0].reshape(1, -1), wrh, wrl, tm)

    pos, gate, starts = _route(lgt, moe_b_router[0], cap, sub)
    pos4 = pos.reshape(b, N_EXPERTS, 1, s)
    gate4 = gate.reshape(b, N_EXPERTS, 1, s)
    slots = _slots(starts, pos4, gate4, cap, tile, gather_chunk)
    row0 = (jnp.arange(b, dtype=I32) * s)[:, None, None]
    idx = (slots[..., 0].astype(I32) * 128 + slots[..., 1].astype(I32) + row0).reshape(-1)
    xa = _sc_gather(ha.reshape(b * s, -1), idx)
    xb = _sc_gather(hb.reshape(b * s, -1), idx)
    ye = _ffn(xa, xb, slots, moe_w1[0].astype(BF16), moe_w3[0].astype(BF16), moe_w2[0].astype(BF16))
    return _combine(starts, ye, pos4, x1, mod3, ln2_g[0].reshape(1, -1), ln2_b[0].reshape(1, -1), tile,
                    scatter_chunk, sub)
```

```python
import functools
import math

import numpy as np
import jax
import jax.numpy as jnp
from jax import lax
from jax.experimental import pallas as pl
from jax.experimental.pallas import tpu as pltpu
from jax.experimental.pallas import tpu_sc as plsc

F32 = jnp.float32
BF16 = jnp.bfloat16
I32 = jnp.int32
U32 = jnp.uint32

D_MODEL = 1024
DEPTH = 1
GRID_W = 64
ROPE_THETA = 10000.0
EPS = 1e-6
MLA_HEADS = 8
MLA_Q_RANK = 256
MLA_KV_RANK = 128
MLA_NOPE = 64
MLA_ROPE = 32
MLA_V = 64
DIFF_HEADS = 4
DIFF_HD = 64
N_EXPERTS = 16
EXPERT_FF = 1024
EC_CAPACITY = 2

MLA_Q_OFF = 0
MLA_KV_OFF = MLA_Q_OFF + MLA_Q_RANK
MLA_KR_OFF = MLA_KV_OFF + MLA_KV_RANK
DIFF_Q_OFF = MLA_KR_OFF + MLA_ROPE
DIFF_K_OFF = DIFF_Q_OFF + DIFF_HEADS * 2 * DIFF_HD
DIFF_V_OFF = DIFF_K_OFF + DIFF_HEADS * 2 * DIFF_HD
GATE_OFF = DIFF_V_OFF + DIFF_HEADS * 2 * DIFF_HD
N_IN = GATE_OFF + 2 * D_MODEL
DIFF_WIDTH = DIFF_HEADS * 2 * DIFF_HD
MLA_SCALE = (MLA_NOPE + MLA_ROPE) ** -0.5
DIFF_SCALE = DIFF_HD ** -0.5
DEEPNORM_ALPHA = (2 * DEPTH) ** 0.25
LAMBDA_INIT = 0.8 - 0.6 * math.exp(-0.3 * 0)
LOG2E = 1.4426950408889634
LAGGED_MAX_HEADROOM = 100.0

LANE = 128
MLA_W = MLA_HEADS * LANE
PX_Q = 0
PX_KV = PX_Q + MLA_Q_RANK
PX_KR = PX_KV + MLA_KV_RANK
PX_DQ = PX_KR + LANE
PX_DK = PX_DQ + DIFF_WIDTH
PX_DV = PX_DK + DIFF_WIDTH
PX_G = PX_DV + DIFF_WIDTH
PX_N = PX_G + 2 * D_MODEL
CX_KV = 0
CX_KR = CX_KV + MLA_KV_RANK
CX_DK = CX_KR + LANE
CX_DV = CX_DK + DIFF_WIDTH
CX_N = CX_DV + DIFF_WIDTH

VMEM_LIMIT = 56 * 1024 * 1024


def _cparams(sem):
    return pltpu.CompilerParams(dimension_semantics=sem, vmem_limit_bytes=VMEM_LIMIT)


def _rope(x, cos, sin_lo, sin_hi, quarter):
    return x * cos + pltpu.roll(x, LANE - quarter, 1) * sin_lo + pltpu.roll(x, quarter, 1) * sin_hi


def _rope_tables(seq):
    t = np.arange(seq)
    row = (t // GRID_W).astype(np.float64)
    col = (t % GRID_W).astype(np.float64)

    def axial(d):
        h = d // 2
        inv = ROPE_THETA ** (-(np.arange(h // 2, dtype=np.float64) * 2.0 / h))
        ar, ac = row[:, None] * inv, col[:, None] * inv
        cos = np.concatenate([np.cos(ar), np.cos(ar), np.cos(ac), np.cos(ac)], -1)
        sin = np.concatenate([np.sin(ar), np.sin(ar), np.sin(ac), np.sin(ac)], -1)
        return cos, sin

    def split(sin, d):
        first = (np.arange(sin.shape[1]) % (d // 2)) < d // 4
        return np.where(first, -sin, 0.0), np.where(first, 0.0, sin)

    c32, s32 = axial(MLA_ROPE)
    c64, s64 = axial(DIFF_HD)
    cos_m = np.ones((seq, LANE))
    sin_m = np.zeros((seq, LANE))
    cos_m[:, MLA_NOPE:MLA_NOPE + MLA_ROPE] = c32
    sin_m[:, MLA_NOPE:MLA_NOPE + MLA_ROPE] = s32
    cos_d = np.concatenate([c64, c64], -1)
    sin_d = np.concatenate([s64, s64], -1)
    tabs = (cos_m, *split(sin_m, MLA_ROPE), cos_d, *split(sin_d, DIFF_HD))
    return tuple(jnp.asarray(a, F32) for a in tabs)


def _rms(x, g):
    return x * lax.rsqrt(jnp.mean(x * x, axis=-1, keepdims=True) + EPS) * g


def _layer_norm(x, g, b):
    mu = jnp.mean(x, axis=-1, keepdims=True)
    xc = x - mu
    var = jnp.mean(xc * xc, axis=-1, keepdims=True)
    return xc * lax.rsqrt(var + EPS) * g + b


def _dot(a, b):
    return jnp.dot(a, b, preferred_element_type=F32)


def _dot_nt(a, b):
    return lax.dot_general(a, b, (((1,), (1,)), ((), ())), preferred_element_type=F32)


def _dot_tn(a, b):
    return lax.dot_general(a, b, (((0,), (0,)), ((), ())), preferred_element_type=F32)


def _ada_kernel(c_ref, w_ref, b_ref, o_ref):
    c = c_ref[...]
    s = (c / (1.0 + jnp.exp(-c))).astype(BF16)
    o_ref[...] = _dot(s, w_ref[...].astype(BF16)) + b_ref[...]


def _ada(cc, w_ada, b_ada):
    n = w_ada.shape[1]
    tn = 1024
    return pl.pallas_call(
        _ada_kernel,
        grid=(n // tn,),
        in_specs=[pl.BlockSpec((8, D_MODEL), lambda j: (0, 0)),
                  pl.BlockSpec((D_MODEL, tn), lambda j: (0, j)),
                  pl.BlockSpec((1, tn), lambda j: (0, j))],
        out_specs=pl.BlockSpec((8, tn), lambda j: (0, j)),
        out_shape=jax.ShapeDtypeStruct((8, n), F32),
        compiler_params=_cparams(("arbitrary",)),
        name="ada",
    )(cc, w_ada, b_ada.reshape(1, n))


def _proj_lat_kernel(x_ref, mod_ref, w_ref, gq_ref, wuq_ref, gkv_ref, wukv_ref,
                     cm_ref, slm_ref, shm_ref, cd_ref, sld_ref, shd_ref,
                     q_ref, k_ref, v_ref, dq_ref, dk_ref, dv_ref, g_ref):
    tm = x_ref.shape[1]
    sh = mod_ref[0, :, 0:D_MODEL]
    sc = mod_ref[0, :, D_MODEL:2 * D_MODEL]
    h = (x_ref[0] * (1.0 + sc) + sh).astype(BF16)

    def proj(a, n):
        return _dot(h, w_ref[:, a:a + n])

    lane = lax.broadcasted_iota(I32, (tm, LANE), 1)
    lo = lane < MLA_NOPE

    def rope_m(t):
        return _rope(t, cm_ref[...], slm_ref[...], shm_ref[...], MLA_ROPE // 4)

    def rope_d(t):
        return _rope(t, cd_ref[...], sld_ref[...], shd_ref[...], DIFF_HD // 4)

    cq = _rms(proj(PX_Q, MLA_Q_RANK), gq_ref[...]).astype(BF16)
    q = _dot(cq, wuq_ref[...])
    for hd in range(MLA_HEADS):
        sl = slice(hd * LANE, (hd + 1) * LANE)
        q_ref[0, :, sl] = (rope_m(q[:, sl]) * (MLA_SCALE * LOG2E)).astype(BF16)

    ckv = _rms(proj(PX_KV, MLA_KV_RANK), gkv_ref[...]).astype(BF16)
    kv = _dot(ckv, wukv_ref[...])
    kr = rope_m(proj(PX_KR, LANE))
    for hd in range(MLA_HEADS):
        sl = slice(hd * LANE, (hd + 1) * LANE)
        kvh = kv[:, sl]
        k_ref[0, :, sl] = jnp.where(lo, kvh, kr).astype(BF16)
        v_ref[0, hd, 0] = _v_rows(kvh)

    dq = proj(PX_DQ, DIFF_WIDTH)
    dk = proj(PX_DK, DIFF_WIDTH)
    for hd in range(DIFF_HEADS):
        sl = slice(hd * LANE, (hd + 1) * LANE)
        qb = rope_d(dq[:, sl]) * (DIFF_SCALE * LOG2E)
        dq_ref[0, :, (2 * hd) * LANE:(2 * hd + 1) * LANE] = jnp.where(lo, qb, 0.0).astype(BF16)
        dq_ref[0, :, (2 * hd + 1) * LANE:(2 * hd + 2) * LANE] = jnp.where(lo, 0.0, qb).astype(BF16)
        dk_ref[0, :, sl] = rope_d(dk[:, sl]).astype(BF16)
    dv_ref[0] = proj(PX_DV, DIFF_WIDTH).astype(BF16)
    pg = proj(PX_G, 2 * D_MODEL)
    g_ref[0] = (1.0 / (1.0 + jnp.exp(-pg))).astype(BF16)


def _proj_ctx_kernel(x_ref, mod_ref, w_ref, gkv_ref, wukv_ref, k_ref, v_ref, dk_ref, dv_ref):
    tm = x_ref.shape[1]
    sh = mod_ref[0, :, 0:D_MODEL]
    sc = mod_ref[0, :, D_MODEL:2 * D_MODEL]
    h = (x_ref[0] * (1.0 + sc) + sh).astype(BF16)

    def proj(a, n):
        return _dot(h, w_ref[:, a:a + n])

    lane = lax.broadcasted_iota(I32, (tm, LANE), 1)
    lo = lane < MLA_NOPE
    ckv = _rms(proj(CX_KV, MLA_KV_RANK), gkv_ref[...]).astype(BF16)
    kv = _dot(ckv, wukv_ref[...])
    kr = proj(CX_KR, LANE)
    for hd in range(MLA_HEADS):
        sl = slice(hd * LANE, (hd + 1) * LANE)
        kvh = kv[:, sl]
        k_ref[0, :, sl] = jnp.where(lo, kvh, kr).astype(BF16)
        v_ref[0, hd] = _v_rows(kvh)
    dk_ref[0] = proj(CX_DK, DIFF_WIDTH).astype(BF16)
    dv_ref[0] = proj(CX_DV, DIFF_WIDTH).astype(BF16)


def _full(shape):
    nd = len(shape)
    return pl.BlockSpec(shape, lambda *_: (0,) * nd)


def _proj_lat(x, mod3, wx, gq, wuq, gkv, wukv, tabs, tm, tk):
    b, s, _ = x.shape
    row = lambda w: pl.BlockSpec((1, tm, w), lambda bi, ti: (bi, ti, 0))
    tab = pl.BlockSpec((tm, LANE), lambda bi, ti: (ti, 0))
    outs = [(MLA_W, BF16), (MLA_W, BF16), None, (2 * DIFF_WIDTH, BF16), (DIFF_WIDTH, BF16),
            (DIFF_WIDTH, BF16), (2 * D_MODEL, BF16)]
    per = tk // tm
    v_spec = pl.BlockSpec((1, MLA_HEADS, 1, MLA_VROWS, tm), lambda bi, ti: (bi, 0, ti // per, 0, ti % per))
    v_shape = jax.ShapeDtypeStruct((b, MLA_HEADS, s // tk, MLA_VROWS, tk), BF16)
    return pl.pallas_call(
        _proj_lat_kernel,
        grid=(b, s // tm),
        in_specs=[row(D_MODEL),
                  pl.BlockSpec((1, 1, 6 * D_MODEL), lambda bi, ti: (bi, 0, 0)),
                  _full(wx.shape), _full(gq.shape), _full(wuq.shape),
                  _full(gkv.shape), _full(wukv.shape)] + [tab] * len(tabs),
        out_specs=[v_spec if o is None else row(o[0]) for o in outs],
        out_shape=[v_shape if o is None else jax.ShapeDtypeStruct((b, s, o[0]), o[1]) for o in outs],
        compiler_params=_cparams(("arbitrary", "arbitrary")),
        name="proj_lat",
    )(x, mod3, wx, gq, wuq, gkv, wukv, *tabs)


def _proj_ctx(ctx, mod3, wc, gkv, wukv):
    b, s, _ = ctx.shape
    row = lambda w: pl.BlockSpec((1, s, w), lambda bi: (bi, 0, 0))
    outs = [MLA_W, None, DIFF_WIDTH, DIFF_WIDTH]
    v_spec = pl.BlockSpec((1, MLA_HEADS, MLA_VROWS, s), lambda bi: (bi, 0, 0, 0))
    v_shape = jax.ShapeDtypeStruct((b, MLA_HEADS, MLA_VROWS, s), BF16)
    return pl.pallas_call(
        _proj_ctx_kernel,
        grid=(b,),
        in_specs=[row(D_MODEL),
                  pl.BlockSpec((1, 1, 6 * D_MODEL), lambda bi: (b, 0, 0)),
                  _full(wc.shape), _full(gkv.shape), _full(wukv.shape)],
        out_specs=[v_spec if w is None else row(w) for w in outs],
        out_shape=[v_shape if w is None else jax.ShapeDtypeStruct((b, s, w), BF16) for w in outs],
        compiler_params=_cparams(("arbitrary",)),
        name="proj_ctx",
    )(ctx, mod3, wc, gkv, wukv)


ONES_ROWS = 16


MLA_VROWS = 80


def _v_rows(kvh):
    tail = (lax.broadcasted_iota(I32, (MLA_VROWS - MLA_V, kvh.shape[0]), 0) == 0).astype(F32)
    return jnp.concatenate([kvh.T[MLA_NOPE:, :], tail], axis=0).astype(BF16)


def _flash_t(q_t, kc_ref, vc, kl_ref, v_of, p_ref, g_ref, m_ref, x_ref, acc_ref, tk, ones_rows):
    n = kl_ref.shape[1] // tk

    def chunk(ref, i):
        return ref[0, pl.ds(pl.multiple_of(i * tk, tk), tk), :]

    def pv(v, p):
        if ones_rows:
            v = jnp.concatenate([v.T, jnp.ones((ONES_ROWS, v.shape[0]), BF16)], axis=0)
        return _dot(v, p)

    def scores_exp(i):
        stab = m_ref[...]
        s_t = _dot(chunk(kl_ref, i), q_t)
        p_ref[i % 2] = jnp.exp2(s_t - stab).astype(BF16)
        cmax = jnp.max(s_t, axis=0, keepdims=True)
        m_new = jnp.maximum(stab, cmax)
        g_ref[(i + 1) % 2] = jnp.exp2(stab - m_new)
        m_ref[...] = m_new
        x_ref[...] = jnp.maximum(x_ref[...], cmax - stab)

    def values(i):
        acc_ref[...] = g_ref[i % 2] * acc_ref[...] + pv(v_of(i), p_ref[i % 2])

    def exact_chunk(k, v, m, first):
        out = pv(v, jnp.exp2(_dot(k, q_t) - m).astype(BF16))
        acc_ref[...] = out if first else acc_ref[...] + out

    m0 = jnp.max(_dot(kc_ref[0], q_t), axis=0, keepdims=True)
    m_ref[...] = m0
    g_ref[0] = jnp.ones_like(m0)
    x_ref[...] = jnp.zeros_like(m0)
    exact_chunk(kc_ref[0], vc, m0, True)

    scores_exp(0)

    def body(i, carry):
        values(i - 1)
        scores_exp(i)
        return carry

    lax.fori_loop(1, n, body, 0)
    values(n - 1)

    @pl.when(jnp.max(x_ref[...]) > LAGGED_MAX_HEADROOM)
    def _():
        m = m_ref[...]
        exact_chunk(kc_ref[0], vc, m, True)

        def redo(i, carry):
            exact_chunk(chunk(kl_ref, i), v_of(i), m, False)
            return carry

        lax.fori_loop(0, n, redo, 0)


def _attn_scratch(tk, nq, rows):
    v = pltpu.VMEM
    return [v((2, tk, nq), BF16), v((2, 1, nq), F32), v((1, nq), F32), v((1, nq), F32), v((rows, nq), F32)]


def _mla_attn_kernel(q_ref, kc_ref, vc_ref, kl_ref, vl_ref, o_ref, p_ref, g_ref, m_ref, x_ref, acc_ref, *, tk):
    _flash_t(q_ref[0].T, kc_ref, vc_ref[0, 0], kl_ref, lambda i: vl_ref[0, 0, i], p_ref, g_ref, m_ref, x_ref,
             acc_ref, tk, False)
    acc = acc_ref[...]
    o = acc[0:MLA_V, :] / acc[MLA_V:MLA_V + 1, :]
    o_ref[0] = jnp.concatenate([o, jnp.zeros_like(o)], axis=0).T.astype(BF16)


def _diff_attn_kernel(lam_ref, gs_ref, q1_ref, q2_ref, kc_ref, vc_ref, kl_ref, vl_ref, o_ref,
                      p_ref, g_ref, m_ref, x_ref, acc_ref, *, tk):
    tq = q1_ref.shape[1]
    q_t = jnp.concatenate([q1_ref[0].T, q2_ref[0].T], axis=1)
    _flash_t(q_t, kc_ref, vc_ref[0], kl_ref,
             lambda i: vl_ref[0, pl.ds(pl.multiple_of(i * tk, tk), tk), :],
             p_ref, g_ref, m_ref, x_ref, acc_ref, tk, True)
    o = acc_ref[0:LANE, :] / acc_ref[LANE:LANE + 1, :]
    dl = lam_ref[...]
    lam = (jnp.exp(jnp.sum(dl[0:1] * dl[1:2], axis=-1, keepdims=True))
           - jnp.exp(jnp.sum(dl[2:3] * dl[3:4], axis=-1, keepdims=True)) + LAMBDA_INIT)
    w = o[:, :tq] - lam * o[:, tq:]
    w = w * lax.rsqrt(jnp.mean(w * w, axis=0, keepdims=True) + EPS) * gs_ref[...] * (1.0 - LAMBDA_INIT)
    o_ref[0] = w.T.astype(BF16)


def _mla_attn(q, kc, vc, kl, vl, tq, tk):
    b, s, _ = q.shape
    sc = kc.shape[1]
    assert s % tk == 0 and s % tq == 0
    qs = pl.BlockSpec((1, tq, LANE), lambda bi, hi, qi: (bi, qi, hi))
    cs = pl.BlockSpec((1, sc, LANE), lambda bi, hi, qi: (bi, 0, hi))
    ls = pl.BlockSpec((1, s, LANE), lambda bi, hi, qi: (bi, 0, hi))
    vcs = pl.BlockSpec((1, 1, MLA_VROWS, sc), lambda bi, hi, qi: (bi, hi, 0, 0))
    vls = pl.BlockSpec((1, 1, s // tk, MLA_VROWS, tk), lambda bi, hi, qi: (bi, hi, 0, 0, 0))
    return pl.pallas_call(
        functools.partial(_mla_attn_kernel, tk=tk),
        grid=(b, MLA_HEADS, s // tq),
        in_specs=[qs, cs, vcs, ls, vls],
        out_specs=qs,
        out_shape=jax.ShapeDtypeStruct((b, s, MLA_W), BF16),
        scratch_shapes=_attn_scratch(tk, tq, MLA_VROWS),
        compiler_params=_cparams(("arbitrary", "arbitrary", "arbitrary")),
        name="mla_attn",
    )(q, kc, vc, kl, vl)


def _diff_attn(lam4, g_subln, dq, kc, vc, kl, vl, tq, tk):
    b, s, _ = dq.shape
    sc = kc.shape[1]
    q1 = pl.BlockSpec((1, tq, LANE), lambda bi, hi, qi: (bi, qi, 2 * hi))
    q2 = pl.BlockSpec((1, tq, LANE), lambda bi, hi, qi: (bi, qi, 2 * hi + 1))
    cs = pl.BlockSpec((1, sc, LANE), lambda bi, hi, qi: (bi, 0, hi))
    ls = pl.BlockSpec((1, s, LANE), lambda bi, hi, qi: (bi, 0, hi))
    return pl.pallas_call(
        functools.partial(_diff_attn_kernel, tk=tk),
        grid=(b, DIFF_HEADS, s // tq),
        in_specs=[_full(lam4.shape), _full(g_subln.shape), q1, q2, cs, cs, ls, ls],
        out_specs=pl.BlockSpec((1, tq, LANE), lambda bi, hi, qi: (bi, qi, hi)),
        out_shape=jax.ShapeDtypeStruct((b, s, DIFF_WIDTH), BF16),
        scratch_shapes=_attn_scratch(tk, 2 * tq, LANE + ONES_ROWS),
        compiler_params=_cparams(("arbitrary", "arbitrary", "arbitrary")),
        name="diff_attn",
    )(lam4, g_subln, dq, dq, kc, vc, kl, vl)


def _mix_kernel(om_ref, od_ref, g_ref, x_ref, mod_ref, wom_ref, wod_ref, wout_ref, lng_ref, lnb_ref,
                wrh_ref, wrl_ref, x1_ref, ha_ref, hb_ref, lg_ref):
    gm = g_ref[0, :, 0:D_MODEL]
    gd = g_ref[0, :, D_MODEL:2 * D_MODEL]
    y = gm * _dot(om_ref[0], wom_ref[...]) + gd * _dot(od_ref[0], wod_ref[...])
    z = _dot(y.astype(BF16), wout_ref[...])
    g1 = mod_ref[0, :, 2 * D_MODEL:3 * D_MODEL]
    sh2 = mod_ref[0, :, 3 * D_MODEL:4 * D_MODEL]
    sc2 = mod_ref[0, :, 4 * D_MODEL:5 * D_MODEL]
    x1 = _layer_norm(DEEPNORM_ALPHA * x_ref[0] + g1 * z, lng_ref[...], lnb_ref[...])
    x1_ref[0] = x1
    h2 = x1 * (1.0 + sc2) + sh2
    hi = h2.astype(BF16)
    lo = (h2 - hi.astype(F32)).astype(BF16)
    bits = pltpu.bitcast(hi.astype(F32), U32)
    q4 = D_MODEL // 4
    ha_ref[0] = bits[:, 0:q4] | (bits[:, q4:2 * q4] >> 16)
    hb_ref[0] = bits[:, 2 * q4:3 * q4] | (bits[:, 3 * q4:4 * q4] >> 16)
    lg_ref[0] = _dot_nt(wrh_ref[...], hi) + (_dot_nt(wrh_ref[...], lo) + _dot_nt(wrl_ref[...], hi))


def _mix(om, od, gates, x, mod3, wom, wod, wout, lng, lnb, wrh, wrl, tm):
    b, s, _ = x.shape
    row = lambda w: pl.BlockSpec((1, tm, w), lambda bi, ti: (bi, ti, 0))
    return pl.pallas_call(
        _mix_kernel,
        grid=(b, s // tm),
        in_specs=[row(MLA_W), row(DIFF_WIDTH), row(2 * D_MODEL), row(D_MODEL),
                  pl.BlockSpec((1, 1, 6 * D_MODEL), lambda bi, ti: (bi, 0, 0)),
                  _full(wom.shape), _full(wod.shape), _full(wout.shape), _full(lng.shape), _full(lnb.shape),
                  _full(wrh.shape), _full(wrl.shape)],
        out_specs=[row(D_MODEL), row(D_MODEL // 4), row(D_MODEL // 4),
                   pl.BlockSpec((1, N_EXPERTS, tm), lambda bi, ti: (bi, 0, ti))],
        out_shape=[jax.ShapeDtypeStruct((b, s, D_MODEL), F32), jax.ShapeDtypeStruct((b, s, D_MODEL // 4), U32),
                   jax.ShapeDtypeStruct((b, s, D_MODEL // 4), U32), jax.ShapeDtypeStruct((b, N_EXPERTS, s), F32)],
        compiler_params=_cparams(("arbitrary", "arbitrary")),
        name="mix",
    )(om, od, gates, x, mod3, wom, wod, wout, lng, lnb, wrh, wrl)


def _route_kernel(lg_ref, br_ref, pos_ref, gate_ref, st_ref, *, cap, tile):
    e, s = lg_ref.shape[1], lg_ref.shape[2]
    lg = lg_ref[0] + br_ref[...]
    ex = jnp.exp(lg - jnp.max(lg, axis=0, keepdims=True))
    aff = ex / jnp.sum(ex, axis=0, keepdims=True)
    gate_ref[0] = aff
    bits = pltpu.bitcast(aff, I32)

    def count(mask):
        return jnp.sum(mask.astype(I32), axis=1, keepdims=True)

    def search(i, thr):
        cand = thr | (jnp.int32(1) << (30 - i))
        return jnp.where(count(bits >= cand) >= cap, cand, thr)

    thr = lax.fori_loop(0, 31, search, jnp.zeros((e, 1), I32))
    gt = bits > thr
    eq = bits == thr
    need = cap - count(gt)

    r = lax.broadcasted_iota(I32, (LANE, LANE), 0)
    c = lax.broadcasted_iota(I32, (LANE, LANE), 1)
    tri = (r < c).astype(BF16)
    lane = lax.broadcasted_iota(I32, (e, LANE), 1)

    def prefix(mask_fn, emit):
        carry = jnp.zeros((e, 1), F32)
        for j in range(s // LANE):
            m = mask_fn(j)
            emit(j, carry + _dot(m.astype(BF16), tri), m, carry)
            carry = carry + jnp.sum(m.astype(F32), axis=1, keepdims=True)

    def eq_blk(j):
        return eq[:, j * LANE:(j + 1) * LANE]

    def emit_sel(j, rank, m, carry):
        sel = gt[:, j * LANE:(j + 1) * LANE] | (m & (rank < need.astype(F32)))
        pos_ref[0, :, j * LANE:(j + 1) * LANE] = sel.astype(I32)

    prefix(eq_blk, emit_sel)

    starts = [jnp.zeros((e, LANE), I32)]

    def sel_blk(j):
        return pos_ref[0, :, j * LANE:(j + 1) * LANE] > 0

    def emit_pos(j, rank, m, carry):
        if (j * LANE) % tile == 0:
            starts[0] = jnp.where(lane == (j * LANE) // tile, carry.astype(I32), starts[0])
        pos_ref[0, :, j * LANE:(j + 1) * LANE] = jnp.where(m, rank.astype(I32), -1)

    prefix(sel_blk, emit_pos)
    st_ref[0] = jnp.where(lane == s // tile, cap, starts[0])


def _route(lgt, b_router, cap, tile):
    b, e, s = lgt.shape
    blk = pl.BlockSpec((1, e, s), lambda bi: (bi, 0, 0))
    return pl.pallas_call(
        functools.partial(_route_kernel, cap=cap, tile=tile),
        grid=(b,),
        in_specs=[blk, _full((e, 1))],
        out_specs=[blk, blk, pl.BlockSpec((1, e, LANE), lambda bi: (bi, 0, 0))],
        out_shape=[jax.ShapeDtypeStruct((b, e, s), I32), jax.ShapeDtypeStruct((b, e, s), F32),
                   jax.ShapeDtypeStruct((b, e, LANE), I32)],
        compiler_params=_cparams(("arbitrary",)),
        name="route",
    )(lgt, b_router.reshape(e, 1))


SLOT_COLS = 8


def _slots_kernel(st_ref, pos_ref, gate_ref, o_ref, acc_ref, *, chunk, tile):
    bi, ei = pl.program_id(0), pl.program_id(1)
    cap, s = acc_ref.shape[0], pos_ref.shape[3]
    acc_ref[...] = jnp.zeros_like(acc_ref)
    slot = lax.broadcasted_iota(I32, (chunk, tile), 0)
    row = lax.broadcasted_iota(I32, (SLOT_COLS, tile), 0)
    for u in range(s // tile):
        b0, b1 = st_ref[bi, ei, u], st_ref[bi, ei, u + 1]
        cols = slice(u * tile, (u + 1) * tile)

        @pl.when(b1 > b0)
        def _():
            t = lax.broadcasted_iota(I32, (SLOT_COLS, tile), 1) + u * tile
            g = jnp.broadcast_to(gate_ref[0, 0, :, cols], (SLOT_COLS, tile))
            g1 = g.astype(BF16).astype(F32)
            g2 = (g - g1).astype(BF16).astype(F32)
            g3 = g - g1 - g2
            rec = jnp.where(row == 0, (t >> 7).astype(F32),
                            jnp.where(row == 1, (t & 127).astype(F32),
                                      jnp.where(row == 2, g1, jnp.where(row == 3, g2,
                                                                        jnp.where(row == 4, g3, 0.0)))))
            rec = rec.astype(BF16)
            pos = pos_ref[0, 0, :, cols]
            for c in range(cap // chunk):
                lo, hi = c * chunk, (c + 1) * chunk

                @pl.when((b0 < hi) & (b1 > lo))
                def _():
                    hit = (slot == (pos - lo)).astype(BF16)
                    acc_ref[lo:hi, :] += _dot_nt(hit, rec)

    o_ref[0, 0] = acc_ref[...]


def _slots(starts, pos4, gate4, cap, tile, chunk):
    b, e, _, s = pos4.shape
    blk = pl.BlockSpec((1, 1, 1, s), lambda bi, ei, st: (bi, ei, 0, 0))
    gs = pltpu.PrefetchScalarGridSpec(
        num_scalar_prefetch=1,
        grid=(b, e),
        in_specs=[blk, blk],
        out_specs=pl.BlockSpec((1, 1, cap, SLOT_COLS), lambda bi, ei, st: (bi, ei, 0, 0)),
        scratch_shapes=[pltpu.VMEM((cap, SLOT_COLS), F32)],
    )
    return pl.pallas_call(
        functools.partial(_slots_kernel, chunk=chunk, tile=tile),
        grid_spec=gs,
        out_shape=jax.ShapeDtypeStruct((b, e, cap, SLOT_COLS), F32),
        compiler_params=_cparams(("arbitrary", "arbitrary")),
        name="slots",
    )(starts, pos4, gate4)


SC_GATHER_WINDOW = 128


def _sc_gather(table, idx):
    n = idx.shape[0]
    w = table.shape[1]
    mesh = plsc.VectorSubcoreMesh(core_axis_name="core", subcore_axis_name="subcore")

    @pl.kernel(out_type=jax.ShapeDtypeStruct((n, w), table.dtype), mesh=mesh)
    def gather(t_hbm, i_hbm, o_hbm):
        def body(i_vmem, o_vmem):
            pltpu.sync_copy(t_hbm.at[i_vmem.at[0]], o_vmem)

        pltpu.emit_pipeline(
            body,
            grid=(n // SC_GATHER_WINDOW,),
            in_specs=[pl.BlockSpec((1, SC_GATHER_WINDOW), index_map=lambda i: (0, i))],
            out_specs=[pl.BlockSpec((SC_GATHER_WINDOW, w), index_map=lambda i: (i, 0))],
            core_axis_name=("core", "subcore"),
            dimension_semantics=(pltpu.PARALLEL,),
        )(i_hbm, o_hbm)

    return gather(table, idx.reshape(1, n))


def _ffn_kernel(xa_ref, xb_ref, sl_ref, w1_ref, w3_ref, w2_ref, *rest):
    ye_ref = rest[-1]

    def halves(words):
        return pltpu.bitcast(words & jnp.uint32(0xFFFF0000), F32), pltpu.bitcast(words << 16, F32)

    a_hi, a_lo = halves(xa_ref[...])
    b_hi, b_lo = halves(xb_ref[...])
    xe = jnp.concatenate([a_hi, a_lo, b_hi, b_lo], axis=1).astype(BF16)
    sl = sl_ref[0, 0]
    gate = sl[:, 2:3] + sl[:, 3:4] + sl[:, 4:5]
    a = _dot(xe, w1_ref[0])
    hid = ((a / (1.0 + jnp.exp(-a))) * _dot(xe, w3_ref[0])).astype(BF16)
    ye_ref[0, 0] = (_dot(hid, w2_ref[0]) * gate).astype(BF16)


def _ffn(bi, ye_prev, xa, xb, slots, w1, w3, w2):
    b, e, cap, _ = slots.shape
    d = w2.shape[2]
    xspec = pl.BlockSpec((cap, xa.shape[1]), lambda ei: (ei, 0))
    wspec = lambda shp: pl.BlockSpec((1,) + shp, lambda ei: (ei, 0, 0))
    in_specs = [xspec, xspec, pl.BlockSpec((1, 1, cap, SLOT_COLS), lambda ei: (bi, ei, 0, 0)),
                wspec(w1.shape[1:]), wspec(w3.shape[1:]), wspec(w2.shape[1:])]
    args = [xa, xb, slots, w1, w3, w2]
    aliases = {}
    if ye_prev is not None:
        in_specs.append(pl.BlockSpec(memory_space=pl.ANY))
        args.append(ye_prev)
        aliases = {len(args) - 1: 0}
    return pl.pallas_call(
        _ffn_kernel,
        grid=(e,),
        in_specs=in_specs,
        out_specs=pl.BlockSpec((1, 1, cap, d), lambda ei: (bi, ei, 0, 0)),
        out_shape=jax.ShapeDtypeStruct((b, e, cap, d), BF16),
        input_output_aliases=aliases,
        compiler_params=_cparams(("arbitrary",)),
        name="ffn",
    )(*args)


def _combine_kernel(st_ref, ye_ref, pos_ref, x1_ref, mod_ref, lng_ref, lnb_ref, o_ref, acc_ref, *, chunk, sub):
    bi, ti, ei = pl.program_id(0), pl.program_id(1), pl.program_id(2)
    cap, tile = ye_ref.shape[2], x1_ref.shape[1]
    nsub = tile // sub
    bounds = [st_ref[bi, ei, ti * nsub + u] for u in range(nsub + 1)]

    @pl.when(ei == 0)
    def _():
        acc_ref[...] = jnp.zeros_like(acc_ref)

    slot = lax.broadcasted_iota(I32, (chunk, sub), 0)
    for c in range(cap // chunk):
        lo, hi = c * chunk, (c + 1) * chunk

        @pl.when((bounds[0] < hi) & (bounds[nsub] > lo))
        def _():
            for u in range(nsub):
                @pl.when((bounds[u] < hi) & (bounds[u + 1] > lo))
                def _():
                    cols = slice(u * sub, (u + 1) * sub)
                    hit = (slot == (pos_ref[0, 0, :, cols] - lo)).astype(BF16)
                    acc_ref[cols, :] += _dot_tn(hit, ye_ref[0, 0, lo:hi, :])

    @pl.when(ei == pl.num_programs(2) - 1)
    def _():
        g2 = mod_ref[0, :, 5 * D_MODEL:6 * D_MODEL]
        o_ref[0] = _layer_norm(DEEPNORM_ALPHA * x1_ref[0] + g2 * acc_ref[...], lng_ref[...], lnb_ref[...])


def _combine(starts, ye, pos4, x1, mod3, lng, lnb, tile, chunk, sub):
    b, s, d = x1.shape
    e, cap = ye.shape[1], ye.shape[2]
    gs = pltpu.PrefetchScalarGridSpec(
        num_scalar_prefetch=1,
        grid=(b, s // tile, e),
        in_specs=[pl.BlockSpec((1, 1, cap, d), lambda bi, ti, ei, st: (bi, ei, 0, 0)),
                  pl.BlockSpec((1, 1, 1, tile), lambda bi, ti, ei, st: (bi, ei, 0, ti)),
                  pl.BlockSpec((1, tile, d), lambda bi, ti, ei, st: (bi, ti, 0)),
                  pl.BlockSpec((1, 1, 6 * D_MODEL), lambda bi, ti, ei, st: (bi, 0, 0)),
                  pl.BlockSpec((1, d), lambda bi, ti, ei, st: (0, 0)),
                  pl.BlockSpec((1, d), lambda bi, ti, ei, st: (0, 0))],
        out_specs=pl.BlockSpec((1, tile, d), lambda bi, ti, ei, st: (bi, ti, 0)),
        scratch_shapes=[pltpu.VMEM((tile, d), F32)],
    )
    return pl.pallas_call(
        functools.partial(_combine_kernel, chunk=chunk, sub=sub),
        grid_spec=gs,
        out_shape=jax.ShapeDtypeStruct((b, s, d), F32),
        compiler_params=_cparams(("arbitrary", "arbitrary", "arbitrary")),
        name="combine",
    )(starts, ye, pos4, x1, mod3, lng, lnb)


def _head_pad(w, n_heads, width, offset=0):
    k = w.shape[0]
    w3 = w.reshape(k, n_heads, width)
    out = jnp.zeros((k, n_heads, LANE), w.dtype).at[:, :, offset:offset + width].set(w3)
    return out.reshape(k, n_heads * LANE)


def _prep_weights(w_in, w_uq, w_o_mla):
    kr = w_in[:, MLA_KR_OFF:DIFF_Q_OFF]
    dqw = w_in[:, DIFF_Q_OFF:DIFF_K_OFF]
    dkw = w_in[:, DIFF_K_OFF:DIFF_V_OFF]
    dvw = w_in[:, DIFF_V_OFF:GATE_OFF]
    kr_pad = _head_pad(kr, 1, MLA_ROPE, MLA_NOPE)
    wx = jnp.concatenate([w_in[:, MLA_Q_OFF:MLA_KR_OFF], kr_pad, dqw, dkw, dvw, w_in[:, GATE_OFF:]],
                         axis=1).astype(BF16)
    wc = jnp.concatenate([w_in[:, MLA_KV_OFF:MLA_KR_OFF], kr_pad, dkw, dvw], axis=1).astype(BF16)
    wuq = _head_pad(w_uq, MLA_HEADS, MLA_NOPE + MLA_ROPE).astype(BF16)
    wom = jnp.zeros((MLA_HEADS, LANE, D_MODEL), w_o_mla.dtype).at[:, :MLA_V, :].set(
        w_o_mla.reshape(MLA_HEADS, MLA_V, D_MODEL)).reshape(MLA_W, D_MODEL).astype(BF16)
    return wx, wc, wuq, wom


def kernel(x, c, ctx, c_ctx, w_ada, b_ada, w_in, mla_g_q, mla_w_uq, mla_g_kv, mla_w_ukv, mla_w_o, diff_lambda,
           diff_g_subln, diff_w_o, w_out, ln1_g, ln1_b, moe_w_router, moe_b_router, moe_w1, moe_w3, moe_w2,
           ln2_g, ln2_b):
    b, s, d = x.shape
    assert d == D_MODEL and w_ada.shape[0] == DEPTH == 1 and b <= 4
    cap = EC_CAPACITY * s // N_EXPERTS
    tile = min(1024, s)
    sub = tile
    gather_chunk = min(128, cap)
    scatter_chunk = min(256, cap)
    tm = min(512, s)
    tq = min(2048, s)
    tk = min(2048, s)

    cc = jnp.zeros((8, d), F32).at[:b].set(c).at[b].set(c_ctx)
    mod3 = _ada(cc, w_ada[0], b_ada[0]).reshape(8, 1, 6 * d)

    wx, wc, wuq, wom = _prep_weights(w_in[0], mla_w_uq[0], mla_w_o[0])
    gq = mla_g_q[0].reshape(1, -1)
    gkv = mla_g_kv[0].reshape(1, -1)
    wukv = mla_w_ukv[0].astype(BF16)
    tabs = _rope_tables(s)

    q, k, v, dq, dk, dv, gates = _proj_lat(x, mod3, wx, gq, wuq, gkv, wukv, tabs, tm, tk)
    kc, vc, dkc, dvc = _proj_ctx(ctx, mod3, wc, gkv, wukv)

    o_mla = _mla_attn(q, kc, vc, k, v, tq, tk)
    o_diff = _diff_attn(diff_lambda[0], diff_g_subln[0].reshape(-1, 1), dq, dkc, dvc, dk, dv, tq // 2, tk)

    wr = moe_w_router[0].T
    wrh = wr.astype(BF16)
    wrl = (wr - wrh.astype(F32)).astype(BF16)
    x1, ha, hb, lgt = _mix(o_mla, o_diff, gates, x, mod3, wom, diff_w_o[0].astype(BF16), w_out[0].astype(BF16),
                       ln1_g[0].reshape(1, -1), ln1_b[0].reshape(1, -1), wrh, wrl, tm)

    pos, gate, starts = _route(lgt, moe_b_router[0], cap, sub)
    pos4 = pos.reshape(b, N_EXPERTS, 1, s)
    gate4 = gate.reshape(b, N_EXPERTS, 1, s)
    slots = _slots(starts, pos4, gate4, cap, tile, gather_chunk)
    row0 = (jnp.arange(b, dtype=I32) * s)[:, None, None]
    idx = (slots[..., 0].astype(I32) * 128 + slots[..., 1].astype(I32) + row0).reshape(b, -1)
    ha2, hb2 = ha.reshape(b * s, -1), hb.reshape(b * s, -1)
    rows = [(_sc_gather(ha2, idx[bi]), _sc_gather(hb2, idx[bi])) for bi in range(b)]
    w1b, w3b, w2b = moe_w1[0].astype(BF16), moe_w3[0].astype(BF16), moe_w2[0].astype(BF16)
    ye = None
    for bi, (xa, xb) in enumerate(rows):
        ye = _ffn(bi, ye, xa, xb, slots, w1b, w3b, w2b)
    return _combine(starts, ye, pos4, x1, mod3, ln2_g[0].reshape(1, -1), ln2_b[0].reshape(1, -1), tile,
                    scatter_chunk, sub)
```

```python
import functools
import math

import numpy as np
import jax
import jax.numpy as jnp
from jax import lax
from jax.experimental import pallas as pl
from jax.experimental.pallas import tpu as pltpu
from jax.experimental.pallas import tpu_sc as plsc

F32 = jnp.float32
BF16 = jnp.bfloat16
I32 = jnp.int32
U32 = jnp.uint32

D_MODEL = 1024
DEPTH = 1
GRID_W = 64
ROPE_THETA = 10000.0
EPS = 1e-6
MLA_HEADS = 8
MLA_Q_RANK = 256
MLA_KV_RANK = 128
MLA_NOPE = 64
MLA_ROPE = 32
MLA_V = 64
DIFF_HEADS = 4
DIFF_HD = 64
N_EXPERTS = 16
EXPERT_FF = 1024
EC_CAPACITY = 2

MLA_Q_OFF = 0
MLA_KV_OFF = MLA_Q_OFF + MLA_Q_RANK
MLA_KR_OFF = MLA_KV_OFF + MLA_KV_RANK
DIFF_Q_OFF = MLA_KR_OFF + MLA_ROPE
DIFF_K_OFF = DIFF_Q_OFF + DIFF_HEADS * 2 * DIFF_HD
DIFF_V_OFF = DIFF_K_OFF + DIFF_HEADS * 2 * DIFF_HD
GATE_OFF = DIFF_V_OFF + DIFF_HEADS * 2 * DIFF_HD
N_IN = GATE_OFF + 2 * D_MODEL
DIFF_WIDTH = DIFF_HEADS * 2 * DIFF_HD
MLA_SCALE = (MLA_NOPE + MLA_ROPE) ** -0.5
DIFF_SCALE = DIFF_HD ** -0.5
DEEPNORM_ALPHA = (2 * DEPTH) ** 0.25
LAMBDA_INIT = 0.8 - 0.6 * math.exp(-0.3 * 0)
LOG2E = 1.4426950408889634
LAGGED_MAX_HEADROOM = 100.0

LANE = 128
MLA_W = MLA_HEADS * LANE
PX_Q = 0
PX_KV = PX_Q + MLA_Q_RANK
PX_KR = PX_KV + MLA_KV_RANK
PX_DQ = PX_KR + LANE
PX_DK = PX_DQ + DIFF_WIDTH
PX_DV = PX_DK + DIFF_WIDTH
PX_G = PX_DV + DIFF_WIDTH
PX_N = PX_G + 2 * D_MODEL
CX_KV = 0
CX_KR = CX_KV + MLA_KV_RANK
CX_DK = CX_KR + LANE
CX_DV = CX_DK + DIFF_WIDTH
CX_N = CX_DV + DIFF_WIDTH

VMEM_LIMIT = 56 * 1024 * 1024


def _cparams(sem):
    return pltpu.CompilerParams(dimension_semantics=sem, vmem_limit_bytes=VMEM_LIMIT)


def _rope(x, cos, sin_lo, sin_hi, quarter):
    return x * cos + pltpu.roll(x, LANE - quarter, 1) * sin_lo + pltpu.roll(x, quarter, 1) * sin_hi


def _rope_tables(seq):
    t = np.arange(seq)
    row = (t // GRID_W).astype(np.float64)
    col = (t % GRID_W).astype(np.float64)

    def axial(d):
        h = d // 2
        inv = ROPE_THETA ** (-(np.arange(h // 2, dtype=np.float64) * 2.0 / h))
        ar, ac = row[:, None] * inv, col[:, None] * inv
        cos = np.concatenate([np.cos(ar), np.cos(ar), np.cos(ac), np.cos(ac)], -1)
        sin = np.concatenate([np.sin(ar), np.sin(ar), np.sin(ac), np.sin(ac)], -1)
        return cos, sin

    def split(sin, d):
        first = (np.arange(sin.shape[1]) % (d // 2)) < d // 4
        return np.where(first, -sin, 0.0), np.where(first, 0.0, sin)

    c32, s32 = axial(MLA_ROPE)
    c64, s64 = axial(DIFF_HD)
    cos_m = np.ones((seq, LANE))
    sin_m = np.zeros((seq, LANE))
    cos_m[:, MLA_NOPE:MLA_NOPE + MLA_ROPE] = c32
    sin_m[:, MLA_NOPE:MLA_NOPE + MLA_ROPE] = s32
    cos_d = np.concatenate([c64, c64], -1)
    sin_d = np.concatenate([s64, s64], -1)
    tabs = (cos_m, *split(sin_m, MLA_ROPE), cos_d, *split(sin_d, DIFF_HD))
    return tuple(jnp.asarray(a, F32) for a in tabs)


def _rms(x, g):
    return x * lax.rsqrt(jnp.mean(x * x, axis=-1, keepdims=True) + EPS) * g


def _layer_norm(x, g, b):
    mu = jnp.mean(x, axis=-1, keepdims=True)
    xc = x - mu
    var = jnp.mean(xc * xc, axis=-1, keepdims=True)
    return xc * lax.rsqrt(var + EPS) * g + b


def _dot(a, b):
    return jnp.dot(a, b, preferred_element_type=F32)


def _dot_nt(a, b):
    return lax.dot_general(a, b, (((1,), (1,)), ((), ())), preferred_element_type=F32)


def _dot_tn(a, b):
    return lax.dot_general(a, b, (((0,), (0,)), ((), ())), preferred_element_type=F32)


def _ada_kernel(c_ref, w_ref, b_ref, o_ref):
    c = c_ref[...]
    s = (c / (1.0 + jnp.exp(-c))).astype(BF16)
    o_ref[...] = _dot(s, w_ref[...].astype(BF16)) + b_ref[...]


def _ada(cc, w_ada, b_ada):
    n = w_ada.shape[1]
    tn = 1024
    return pl.pallas_call(
        _ada_kernel,
        grid=(n // tn,),
        in_specs=[pl.BlockSpec((8, D_MODEL), lambda j: (0, 0)),
                  pl.BlockSpec((D_MODEL, tn), lambda j: (0, j)),
                  pl.BlockSpec((1, tn), lambda j: (0, j))],
        out_specs=pl.BlockSpec((8, tn), lambda j: (0, j)),
        out_shape=jax.ShapeDtypeStruct((8, n), F32),
        compiler_params=_cparams(("arbitrary",)),
        name="ada",
    )(cc, w_ada, b_ada.reshape(1, n))


def _proj_lat_kernel(x_ref, mod_ref, w_ref, gq_ref, wuq_ref, gkv_ref, wukv_ref,
                     cm_ref, slm_ref, shm_ref, cd_ref, sld_ref, shd_ref,
                     q_ref, k_ref, v_ref, dq_ref, dk_ref, dv_ref, g_ref):
    tm = x_ref.shape[1]
    sh = mod_ref[0, :, 0:D_MODEL]
    sc = mod_ref[0, :, D_MODEL:2 * D_MODEL]
    h = (x_ref[0] * (1.0 + sc) + sh).astype(BF16)

    def proj(a, n):
        return _dot(h, w_ref[:, a:a + n])

    lane = lax.broadcasted_iota(I32, (tm, LANE), 1)
    lo = lane < MLA_NOPE
    one0 = (lane == 0).astype(F32)

    def rope_m(t):
        return _rope(t, cm_ref[...], slm_ref[...], shm_ref[...], MLA_ROPE // 4)

    def rope_d(t):
        return _rope(t, cd_ref[...], sld_ref[...], shd_ref[...], DIFF_HD // 4)

    cq = _rms(proj(PX_Q, MLA_Q_RANK), gq_ref[...]).astype(BF16)
    q = _dot(cq, wuq_ref[...])
    for hd in range(MLA_HEADS):
        sl = slice(hd * LANE, (hd + 1) * LANE)
        q_ref[0, :, sl] = (rope_m(q[:, sl]) * (MLA_SCALE * LOG2E)).astype(BF16)

    ckv = _rms(proj(PX_KV, MLA_KV_RANK), gkv_ref[...]).astype(BF16)
    kv = _dot(ckv, wukv_ref[...])
    kr = rope_m(proj(PX_KR, LANE))
    for hd in range(MLA_HEADS):
        sl = slice(hd * LANE, (hd + 1) * LANE)
        kvh = kv[:, sl]
        k_ref[0, :, sl] = jnp.where(lo, kvh, kr).astype(BF16)
        v_ref[0, :, sl] = jnp.where(lo, one0, kvh).astype(BF16)

    dq = proj(PX_DQ, DIFF_WIDTH)
    dk = proj(PX_DK, DIFF_WIDTH)
    for hd in range(DIFF_HEADS):
        sl = slice(hd * LANE, (hd + 1) * LANE)
        qb = rope_d(dq[:, sl]) * (DIFF_SCALE * LOG2E)
        dq_ref[0, :, (2 * hd) * LANE:(2 * hd + 1) * LANE] = jnp.where(lo, qb, 0.0).astype(BF16)
        dq_ref[0, :, (2 * hd + 1) * LANE:(2 * hd + 2) * LANE] = jnp.where(lo, 0.0, qb).astype(BF16)
        dk_ref[0, :, sl] = rope_d(dk[:, sl]).astype(BF16)
    dv_ref[0] = proj(PX_DV, DIFF_WIDTH).astype(BF16)
    pg = proj(PX_G, 2 * D_MODEL)
    g_ref[0] = (1.0 / (1.0 + jnp.exp(-pg))).astype(BF16)


def _proj_ctx_kernel(x_ref, mod_ref, w_ref, gkv_ref, wukv_ref, k_ref, v_ref, dk_ref, dv_ref):
    tm = x_ref.shape[1]
    sh = mod_ref[0, :, 0:D_MODEL]
    sc = mod_ref[0, :, D_MODEL:2 * D_MODEL]
    h = (x_ref[0] * (1.0 + sc) + sh).astype(BF16)

    def proj(a, n):
        return _dot(h, w_ref[:, a:a + n])

    lane = lax.broadcasted_iota(I32, (tm, LANE), 1)
    lo = lane < MLA_NOPE
    one0 = (lane == 0).astype(F32)
    ckv = _rms(proj(CX_KV, MLA_KV_RANK), gkv_ref[...]).astype(BF16)
    kv = _dot(ckv, wukv_ref[...])
    kr = proj(CX_KR, LANE)
    for hd in range(MLA_HEADS):
        sl = slice(hd * LANE, (hd + 1) * LANE)
        kvh = kv[:, sl]
        k_ref[0, :, sl] = jnp.where(lo, kvh, kr).astype(BF16)
        v_ref[0, :, sl] = jnp.where(lo, one0, kvh).astype(BF16)
    dk_ref[0] = proj(CX_DK, DIFF_WIDTH).astype(BF16)
    dv_ref[0] = proj(CX_DV, DIFF_WIDTH).astype(BF16)


def _full(shape):
    nd = len(shape)
    return pl.BlockSpec(shape, lambda *_: (0,) * nd)


def _proj_lat(x, mod3, wx, gq, wuq, gkv, wukv, tabs, tm):
    b, s, _ = x.shape
    row = lambda w: pl.BlockSpec((1, tm, w), lambda bi, ti: (bi, ti, 0))
    tab = pl.BlockSpec((tm, LANE), lambda bi, ti: (ti, 0))
    outs = [(MLA_W, BF16), (MLA_W, BF16), (MLA_W, BF16), (2 * DIFF_WIDTH, BF16), (DIFF_WIDTH, BF16),
            (DIFF_WIDTH, BF16), (2 * D_MODEL, BF16)]
    return pl.pallas_call(
        _proj_lat_kernel,
        grid=(b, s // tm),
        in_specs=[row(D_MODEL),
                  pl.BlockSpec((1, 1, 6 * D_MODEL), lambda bi, ti: (bi, 0, 0)),
                  _full(wx.shape), _full(gq.shape), _full(wuq.shape),
                  _full(gkv.shape), _full(wukv.shape)] + [tab] * len(tabs),
        out_specs=[row(w) for w, _ in outs],
        out_shape=[jax.ShapeDtypeStruct((b, s, w), dt) for w, dt in outs],
        compiler_params=_cparams(("arbitrary", "arbitrary")),
        name="proj_lat",
    )(x, mod3, wx, gq, wuq, gkv, wukv, *tabs)


def _proj_ctx(ctx, mod3, wc, gkv, wukv):
    b, s, _ = ctx.shape
    row = lambda w: pl.BlockSpec((1, s, w), lambda bi: (bi, 0, 0))
    outs = [MLA_W, MLA_W, DIFF_WIDTH, DIFF_WIDTH]
    return pl.pallas_call(
        _proj_ctx_kernel,
        grid=(b,),
        in_specs=[row(D_MODEL),
                  pl.BlockSpec((1, 1, 6 * D_MODEL), lambda bi: (b, 0, 0)),
                  _full(wc.shape), _full(gkv.shape), _full(wukv.shape)],
        out_specs=[row(w) for w in outs],
        out_shape=[jax.ShapeDtypeStruct((b, s, w), BF16) for w in outs],
        compiler_params=_cparams(("arbitrary",)),
        name="proj_ctx",
    )(ctx, mod3, wc, gkv, wukv)


ONES_ROWS = 16


def _flash_t(q_t, kc_ref, vc_ref, kl_ref, vl_ref, p_ref, g_ref, m_ref, x_ref, acc_ref, tk, ones_rows):
    n = kl_ref.shape[1] // tk

    def chunk(ref, i):
        return ref[0, pl.ds(pl.multiple_of(i * tk, tk), tk), :]

    def pv(v, p):
        if ones_rows:
            return _dot(jnp.concatenate([v.T, jnp.ones((ONES_ROWS, v.shape[0]), BF16)], axis=0), p)
        return _dot_tn(v, p)

    def scores_exp(i):
        stab = m_ref[...]
        s_t = _dot(chunk(kl_ref, i), q_t)
        p_ref[i % 2] = jnp.exp2(s_t - stab).astype(BF16)
        cmax = jnp.max(s_t, axis=0, keepdims=True)
        m_new = jnp.maximum(stab, cmax)
        g_ref[(i + 1) % 2] = jnp.exp2(stab - m_new)
        m_ref[...] = m_new
        x_ref[...] = jnp.maximum(x_ref[...], cmax - stab)

    def values(i):
        acc_ref[...] = g_ref[i % 2] * acc_ref[...] + pv(chunk(vl_ref, i), p_ref[i % 2])

    def exact_chunk(k, v, m, first):
        out = pv(v, jnp.exp2(_dot(k, q_t) - m).astype(BF16))
        acc_ref[...] = out if first else acc_ref[...] + out

    m0 = jnp.max(_dot(kc_ref[0], q_t), axis=0, keepdims=True)
    m_ref[...] = m0
    g_ref[0] = jnp.ones_like(m0)
    x_ref[...] = jnp.zeros_like(m0)
    exact_chunk(kc_ref[0], vc_ref[0], m0, True)

    scores_exp(0)

    def body(i, carry):
        values(i - 1)
        scores_exp(i)
        return carry

    lax.fori_loop(1, n, body, 0, unroll=True)
    values(n - 1)

    @pl.when(jnp.max(x_ref[...]) > LAGGED_MAX_HEADROOM)
    def _():
        m = m_ref[...]
        exact_chunk(kc_ref[0], vc_ref[0], m, True)

        def redo(i, carry):
            exact_chunk(chunk(kl_ref, i), chunk(vl_ref, i), m, False)
            return carry

        lax.fori_loop(0, n, redo, 0)


def _attn_scratch(tk, nq, ones_rows):
    v = pltpu.VMEM
    rows = LANE + (ONES_ROWS if ones_rows else 0)
    return [v((2, tk, nq), BF16), v((2, 1, nq), F32), v((1, nq), F32), v((1, nq), F32), v((rows, nq), F32)]


def _mla_attn_kernel(q_ref, kc_ref, vc_ref, kl_ref, vl_ref, o_ref, p_ref, g_ref, m_ref, x_ref, acc_ref, *, tk):
    _flash_t(q_ref[0].T, kc_ref, vc_ref, kl_ref, vl_ref, p_ref, g_ref, m_ref, x_ref, acc_ref, tk, False)
    acc = acc_ref[...]
    o_ref[0] = (acc / acc[0:1, :]).T.astype(BF16)


def _diff_attn_kernel(lam_ref, gs_ref, q1_ref, q2_ref, kc_ref, vc_ref, kl_ref, vl_ref, o_ref,
                      p_ref, g_ref, m_ref, x_ref, acc_ref, *, tk):
    tq = q1_ref.shape[1]
    q_t = jnp.concatenate([q1_ref[0].T, q2_ref[0].T], axis=1)
    _flash_t(q_t, kc_ref, vc_ref, kl_ref, vl_ref, p_ref, g_ref, m_ref, x_ref, acc_ref, tk, True)
    o = acc_ref[0:LANE, :] / acc_ref[LANE:LANE + 1, :]
    dl = lam_ref[...]
    lam = (jnp.exp(jnp.sum(dl[0:1] * dl[1:2], axis=-1, keepdims=True))
           - jnp.exp(jnp.sum(dl[2:3] * dl[3:4], axis=-1, keepdims=True)) + LAMBDA_INIT)
    w = o[:, :tq] - lam * o[:, tq:]
    w = w * lax.rsqrt(jnp.mean(w * w, axis=0, keepdims=True) + EPS) * gs_ref[...] * (1.0 - LAMBDA_INIT)
    o_ref[0] = w.T.astype(BF16)


def _mla_attn(q, kc, vc, kl, vl, tq, tk):
    b, s, _ = q.shape
    sc = kc.shape[1]
    assert s % tk == 0 and s % tq == 0
    qs = pl.BlockSpec((1, tq, LANE), lambda bi, hi, qi: (bi, qi, hi))
    cs = pl.BlockSpec((1, sc, LANE), lambda bi, hi, qi: (bi, 0, hi))
    ls = pl.BlockSpec((1, s, LANE), lambda bi, hi, qi: (bi, 0, hi))
    return pl.pallas_call(
        functools.partial(_mla_attn_kernel, tk=tk),
        grid=(b, MLA_HEADS, s // tq),
        in_specs=[qs, cs, cs, ls, ls],
        out_specs=qs,
        out_shape=jax.ShapeDtypeStruct((b, s, MLA_W), BF16),
        scratch_shapes=_attn_scratch(tk, tq, False),
        compiler_params=_cparams(("arbitrary", "arbitrary", "arbitrary")),
        name="mla_attn",
    )(q, kc, vc, kl, vl)


def _diff_attn(lam4, g_subln, dq, kc, vc, kl, vl, tq, tk):
    b, s, _ = dq.shape
    sc = kc.shape[1]
    q1 = pl.BlockSpec((1, tq, LANE), lambda bi, hi, qi: (bi, qi, 2 * hi))
    q2 = pl.BlockSpec((1, tq, LANE), lambda bi, hi, qi: (bi, qi, 2 * hi + 1))
    cs = pl.BlockSpec((1, sc, LANE), lambda bi, hi, qi: (bi, 0, hi))
    ls = pl.BlockSpec((1, s, LANE), lambda bi, hi, qi: (bi, 0, hi))
    return pl.pallas_call(
        functools.partial(_diff_attn_kernel, tk=tk),
        grid=(b, DIFF_HEADS, s // tq),
        in_specs=[_full(lam4.shape), _full(g_subln.shape), q1, q2, cs, cs, ls, ls],
        out_specs=pl.BlockSpec((1, tq, LANE), lambda bi, hi, qi: (bi, qi, hi)),
        out_shape=jax.ShapeDtypeStruct((b, s, DIFF_WIDTH), BF16),
        scratch_shapes=_attn_scratch(tk, 2 * tq, True),
        compiler_params=_cparams(("arbitrary", "arbitrary", "arbitrary")),
        name="diff_attn",
    )(lam4, g_subln, dq, dq, kc, vc, kl, vl)


def _mix_kernel(om_ref, od_ref, g_ref, x_ref, mod_ref, wom_ref, wod_ref, wout_ref, lng_ref, lnb_ref,
                wrh_ref, wrl_ref, x1_ref, ha_ref, hb_ref, lg_ref):
    gm = g_ref[0, :, 0:D_MODEL]
    gd = g_ref[0, :, D_MODEL:2 * D_MODEL]
    y = gm * _dot(om_ref[0], wom_ref[...]) + gd * _dot(od_ref[0], wod_ref[...])
    z = _dot(y.astype(BF16), wout_ref[...])
    g1 = mod_ref[0, :, 2 * D_MODEL:3 * D_MODEL]
    sh2 = mod_ref[0, :, 3 * D_MODEL:4 * D_MODEL]
    sc2 = mod_ref[0, :, 4 * D_MODEL:5 * D_MODEL]
    x1 = _layer_norm(DEEPNORM_ALPHA * x_ref[0] + g1 * z, lng_ref[...], lnb_ref[...])
    x1_ref[0] = x1
    h2 = x1 * (1.0 + sc2) + sh2
    hi = h2.astype(BF16)
    lo = (h2 - hi.astype(F32)).astype(BF16)
    bits = pltpu.bitcast(hi.astype(F32), U32)
    q4 = D_MODEL // 4
    ha_ref[0] = bits[:, 0:q4] | (bits[:, q4:2 * q4] >> 16)
    hb_ref[0] = bits[:, 2 * q4:3 * q4] | (bits[:, 3 * q4:4 * q4] >> 16)
    lg_ref[0] = _dot_nt(wrh_ref[...], hi) + (_dot_nt(wrh_ref[...], lo) + _dot_nt(wrl_ref[...], hi))


def _mix(om, od, gates, x, mod3, wom, wod, wout, lng, lnb, wrh, wrl, tm):
    b, s, _ = x.shape
    row = lambda w: pl.BlockSpec((1, tm, w), lambda bi, ti: (bi, ti, 0))
    return pl.pallas_call(
        _mix_kernel,
        grid=(b, s // tm),
        in_specs=[row(MLA_W), row(DIFF_WIDTH), row(2 * D_MODEL), row(D_MODEL),
                  pl.BlockSpec((1, 1, 6 * D_MODEL), lambda bi, ti: (bi, 0, 0)),
                  _full(wom.shape), _full(wod.shape), _full(wout.shape), _full(lng.shape), _full(lnb.shape),
                  _full(wrh.shape), _full(wrl.shape)],
        out_specs=[row(D_MODEL), row(D_MODEL // 4), row(D_MODEL // 4),
                   pl.BlockSpec((1, N_EXPERTS, tm), lambda bi, ti: (bi, 0, ti))],
        out_shape=[jax.ShapeDtypeStruct((b, s, D_MODEL), F32), jax.ShapeDtypeStruct((b, s, D_MODEL // 4), U32),
                   jax.ShapeDtypeStruct((b, s, D_MODEL // 4), U32), jax.ShapeDtypeStruct((b, N_EXPERTS, s), F32)],
        compiler_params=_cparams(("arbitrary", "arbitrary")),
        name="mix",
    )(om, od, gates, x, mod3, wom, wod, wout, lng, lnb, wrh, wrl)


def _route_kernel(lg_ref, br_ref, pos_ref, gate_ref, st_ref, *, cap, tile):
    e, s = lg_ref.shape[1], lg_ref.shape[2]
    lg = lg_ref[0] + br_ref[...]
    ex = jnp.exp(lg - jnp.max(lg, axis=0, keepdims=True))
    aff = ex / jnp.sum(ex, axis=0, keepdims=True)
    gate_ref[0] = aff
    bits = pltpu.bitcast(aff, I32)

    def count(mask):
        return jnp.sum(mask.astype(I32), axis=1, keepdims=True)

    def search(i, thr):
        cand = thr | (jnp.int32(1) << (30 - i))
        return jnp.where(count(bits >= cand) >= cap, cand, thr)

    thr = lax.fori_loop(0, 31, search, jnp.zeros((e, 1), I32))
    gt = bits > thr
    eq = bits == thr
    need = cap - count(gt)

    r = lax.broadcasted_iota(I32, (LANE, LANE), 0)
    c = lax.broadcasted_iota(I32, (LANE, LANE), 1)
    tri = (r < c).astype(BF16)
    lane = lax.broadcasted_iota(I32, (e, LANE), 1)

    def prefix(mask_fn, emit):
        carry = jnp.zeros((e, 1), F32)
        for j in range(s // LANE):
            m = mask_fn(j)
            emit(j, carry + _dot(m.astype(BF16), tri), m, carry)
            carry = carry + jnp.sum(m.astype(F32), axis=1, keepdims=True)

    def eq_blk(j):
        return eq[:, j * LANE:(j + 1) * LANE]

    def emit_sel(j, rank, m, carry):
        sel = gt[:, j * LANE:(j + 1) * LANE] | (m & (rank < need.astype(F32)))
        pos_ref[0, :, j * LANE:(j + 1) * LANE] = sel.astype(I32)

    prefix(eq_blk, emit_sel)

    starts = [jnp.zeros((e, LANE), I32)]

    def sel_blk(j):
        return pos_ref[0, :, j * LANE:(j + 1) * LANE] > 0

    def emit_pos(j, rank, m, carry):
        if (j * LANE) % tile == 0:
            starts[0] = jnp.where(lane == (j * LANE) // tile, carry.astype(I32), starts[0])
        pos_ref[0, :, j * LANE:(j + 1) * LANE] = jnp.where(m, rank.astype(I32), -1)

    prefix(sel_blk, emit_pos)
    st_ref[0] = jnp.where(lane == s // tile, cap, starts[0])


def _route(lgt, b_router, cap, tile):
    b, e, s = lgt.shape
    blk = pl.BlockSpec((1, e, s), lambda bi: (bi, 0, 0))
    return pl.pallas_call(
        functools.partial(_route_kernel, cap=cap, tile=tile),
        grid=(b,),
        in_specs=[blk, _full((e, 1))],
        out_specs=[blk, blk, pl.BlockSpec((1, e, LANE), lambda bi: (bi, 0, 0))],
        out_shape=[jax.ShapeDtypeStruct((b, e, s), I32), jax.ShapeDtypeStruct((b, e, s), F32),
                   jax.ShapeDtypeStruct((b, e, LANE), I32)],
        compiler_params=_cparams(("arbitrary",)),
        name="route",
    )(lgt, b_router.reshape(e, 1))


SLOT_COLS = 8


def _slots_kernel(st_ref, pos_ref, gate_ref, o_ref, acc_ref, *, chunk, tile):
    bi, ei = pl.program_id(0), pl.program_id(1)
    cap, s = acc_ref.shape[0], pos_ref.shape[3]
    acc_ref[...] = jnp.zeros_like(acc_ref)
    slot = lax.broadcasted_iota(I32, (chunk, tile), 0)
    row = lax.broadcasted_iota(I32, (SLOT_COLS, tile), 0)
    for u in range(s // tile):
        b0, b1 = st_ref[bi, ei, u], st_ref[bi, ei, u + 1]
        cols = slice(u * tile, (u + 1) * tile)

        @pl.when(b1 > b0)
        def _():
            t = lax.broadcasted_iota(I32, (SLOT_COLS, tile), 1) + u * tile
            g = jnp.broadcast_to(gate_ref[0, 0, :, cols], (SLOT_COLS, tile))
            g1 = g.astype(BF16).astype(F32)
            g2 = (g - g1).astype(BF16).astype(F32)
            g3 = g - g1 - g2
            rec = jnp.where(row == 0, (t >> 7).astype(F32),
                            jnp.where(row == 1, (t & 127).astype(F32),
                                      jnp.where(row == 2, g1, jnp.where(row == 3, g2,
                                                                        jnp.where(row == 4, g3, 0.0)))))
            rec = rec.astype(BF16)
            pos = pos_ref[0, 0, :, cols]
            for c in range(cap // chunk):
                lo, hi = c * chunk, (c + 1) * chunk

                @pl.when((b0 < hi) & (b1 > lo))
                def _():
                    hit = (slot == (pos - lo)).astype(BF16)
                    acc_ref[lo:hi, :] += _dot_nt(hit, rec)

    o_ref[0, 0] = acc_ref[...]


def _slots(starts, pos4, gate4, cap, tile, chunk):
    b, e, _, s = pos4.shape
    blk = pl.BlockSpec((1, 1, 1, s), lambda bi, ei, st: (bi, ei, 0, 0))
    gs = pltpu.PrefetchScalarGridSpec(
        num_scalar_prefetch=1,
        grid=(b, e),
        in_specs=[blk, blk],
        out_specs=pl.BlockSpec((1, 1, cap, SLOT_COLS), lambda bi, ei, st: (bi, ei, 0, 0)),
        scratch_shapes=[pltpu.VMEM((cap, SLOT_COLS), F32)],
    )
    return pl.pallas_call(
        functools.partial(_slots_kernel, chunk=chunk, tile=tile),
        grid_spec=gs,
        out_shape=jax.ShapeDtypeStruct((b, e, cap, SLOT_COLS), F32),
        compiler_params=_cparams(("arbitrary", "arbitrary")),
        name="slots",
    )(starts, pos4, gate4)


SC_GATHER_WINDOW = 128


def _sc_gather(table, idx):
    n = idx.shape[0]
    w = table.shape[1]
    mesh = plsc.VectorSubcoreMesh(core_axis_name="core", subcore_axis_name="subcore")

    @pl.kernel(out_type=jax.ShapeDtypeStruct((n, w), table.dtype), mesh=mesh)
    def gather(t_hbm, i_hbm, o_hbm):
        def body(i_vmem, o_vmem):
            pltpu.sync_copy(t_hbm.at[i_vmem.at[0]], o_vmem)

        pltpu.emit_pipeline(
            body,
            grid=(n // SC_GATHER_WINDOW,),
            in_specs=[pl.BlockSpec((1, SC_GATHER_WINDOW), index_map=lambda i: (0, i))],
            out_specs=[pl.BlockSpec((SC_GATHER_WINDOW, w), index_map=lambda i: (i, 0))],
            core_axis_name=("core", "subcore"),
            dimension_semantics=(pltpu.PARALLEL,),
        )(i_hbm, o_hbm)

    return gather(table, idx.reshape(1, n))


def _ffn_kernel(xa_ref, xb_ref, sl_ref, w1_ref, w3_ref, w2_ref, ye_ref):
    def halves(words):
        return pltpu.bitcast(words & jnp.uint32(0xFFFF0000), F32), pltpu.bitcast(words << 16, F32)

    a_hi, a_lo = halves(xa_ref[...])
    b_hi, b_lo = halves(xb_ref[...])
    xe = jnp.concatenate([a_hi, a_lo, b_hi, b_lo], axis=1).astype(BF16)
    sl = sl_ref[0, 0]
    gate = sl[:, 2:3] + sl[:, 3:4] + sl[:, 4:5]
    a = _dot(xe, w1_ref[0])
    hid = ((a / (1.0 + jnp.exp(-a))) * _dot(xe, w3_ref[0])).astype(BF16)
    ye_ref[0, 0] = (_dot(hid, w2_ref[0]) * gate).astype(BF16)


def _ffn(xa, xb, slots, w1, w3, w2):
    b, e, cap, _ = slots.shape
    d = w2.shape[2]
    xspec = pl.BlockSpec((cap, xa.shape[1]), lambda bi, ei: (bi * e + ei, 0))
    wspec = lambda shp: pl.BlockSpec((1,) + shp, lambda bi, ei: (ei, 0, 0))
    return pl.pallas_call(
        _ffn_kernel,
        grid=(b, e),
        in_specs=[xspec, xspec, pl.BlockSpec((1, 1, cap, SLOT_COLS), lambda bi, ei: (bi, ei, 0, 0)),
                  wspec(w1.shape[1:]), wspec(w3.shape[1:]), wspec(w2.shape[1:])],
        out_specs=pl.BlockSpec((1, 1, cap, d), lambda bi, ei: (bi, ei, 0, 0)),
        out_shape=jax.ShapeDtypeStruct((b, e, cap, d), BF16),
        compiler_params=_cparams(("arbitrary", "arbitrary")),
        name="ffn",
    )(xa, xb, slots, w1, w3, w2)


def _combine_kernel(st_ref, ye_ref, pos_ref, x1_ref, mod_ref, lng_ref, lnb_ref, o_ref, acc_ref, *, chunk):
    bi, ti, ei = pl.program_id(0), pl.program_id(1), pl.program_id(2)
    cap, tile = ye_ref.shape[2], x1_ref.shape[1]
    b0, b1 = st_ref[bi, ei, ti], st_ref[bi, ei, ti + 1]

    @pl.when(ei == 0)
    def _():
        acc_ref[...] = jnp.zeros_like(acc_ref)

    slot = lax.broadcasted_iota(I32, (chunk, tile), 0)
    for c in range(cap // chunk):
        lo, hi = c * chunk, (c + 1) * chunk

        @pl.when((b0 < hi) & (b1 > lo))
        def _():
            hit = (slot == (pos_ref[0, 0] - lo)).astype(BF16)
            acc_ref[...] += _dot_tn(hit, ye_ref[0, 0, lo:hi, :])

    @pl.when(ei == pl.num_programs(2) - 1)
    def _():
        g2 = mod_ref[0, :, 5 * D_MODEL:6 * D_MODEL]
        o_ref[0] = _layer_norm(DEEPNORM_ALPHA * x1_ref[0] + g2 * acc_ref[...], lng_ref[...], lnb_ref[...])


def _combine(starts, ye, pos4, x1, mod3, lng, lnb, tile, chunk):
    b, s, d = x1.shape
    e, cap = ye.shape[1], ye.shape[2]
    gs = pltpu.PrefetchScalarGridSpec(
        num_scalar_prefetch=1,
        grid=(b, s // tile, e),
        in_specs=[pl.BlockSpec((1, 1, cap, d), lambda bi, ti, ei, st: (bi, ei, 0, 0)),
                  pl.BlockSpec((1, 1, 1, tile), lambda bi, ti, ei, st: (bi, ei, 0, ti)),
                  pl.BlockSpec((1, tile, d), lambda bi, ti, ei, st: (bi, ti, 0)),
                  pl.BlockSpec((1, 1, 6 * D_MODEL), lambda bi, ti, ei, st: (bi, 0, 0)),
                  pl.BlockSpec((1, d), lambda bi, ti, ei, st: (0, 0)),
                  pl.BlockSpec((1, d), lambda bi, ti, ei, st: (0, 0))],
        out_specs=pl.BlockSpec((1, tile, d), lambda bi, ti, ei, st: (bi, ti, 0)),
        scratch_shapes=[pltpu.VMEM((tile, d), F32)],
    )
    return pl.pallas_call(
        functools.partial(_combine_kernel, chunk=chunk),
        grid_spec=gs,
        out_shape=jax.ShapeDtypeStruct((b, s, d), F32),
        compiler_params=_cparams(("arbitrary", "arbitrary", "arbitrary")),
        name="combine",
    )(starts, ye, pos4, x1, mod3, lng, lnb)


def _head_pad(w, n_heads, width, offset=0):
    k = w.shape[0]
    w3 = w.reshape(k, n_heads, width)
    out = jnp.zeros((k, n_heads, LANE), w.dtype).at[:, :, offset:offset + width].set(w3)
    return out.reshape(k, n_heads * LANE)


def _prep_weights(w_in, w_uq, w_o_mla):
    kr = w_in[:, MLA_KR_OFF:DIFF_Q_OFF]
    dqw = w_in[:, DIFF_Q_OFF:DIFF_K_OFF]
    dkw = w_in[:, DIFF_K_OFF:DIFF_V_OFF]
    dvw = w_in[:, DIFF_V_OFF:GATE_OFF]
    kr_pad = _head_pad(kr, 1, MLA_ROPE, MLA_NOPE)
    wx = jnp.concatenate([w_in[:, MLA_Q_OFF:MLA_KR_OFF], kr_pad, dqw, dkw, dvw, w_in[:, GATE_OFF:]],
                         axis=1).astype(BF16)
    wc = jnp.concatenate([w_in[:, MLA_KV_OFF:MLA_KR_OFF], kr_pad, dkw, dvw], axis=1).astype(BF16)
    wuq = _head_pad(w_uq, MLA_HEADS, MLA_NOPE + MLA_ROPE).astype(BF16)
    wom = jnp.zeros((MLA_HEADS, LANE, D_MODEL), w_o_mla.dtype).at[:, MLA_NOPE:, :].set(
        w_o_mla.reshape(MLA_HEADS, MLA_V, D_MODEL)).reshape(MLA_W, D_MODEL).astype(BF16)
    return wx, wc, wuq, wom


def kernel(x, c, ctx, c_ctx, w_ada, b_ada, w_in, mla_g_q, mla_w_uq, mla_g_kv, mla_w_ukv, mla_w_o, diff_lambda,
           diff_g_subln, diff_w_o, w_out, ln1_g, ln1_b, moe_w_router, moe_b_router, moe_w1, moe_w3, moe_w2,
           ln2_g, ln2_b):
    b, s, d = x.shape
    assert d == D_MODEL and w_ada.shape[0] == DEPTH == 1 and b < 8
    cap = EC_CAPACITY * s // N_EXPERTS
    tile = min(1024, s)
    gather_chunk = min(128, cap)
    scatter_chunk = min(256, cap)
    tm = min(512, s)
    tq = min(2048, s)
    tk = min(2048, s)

    cc = jnp.zeros((8, d), F32).at[:b].set(c).at[b].set(c_ctx)
    mod3 = _ada(cc, w_ada[0], b_ada[0]).reshape(8, 1, 6 * d)

    wx, wc, wuq, wom = _prep_weights(w_in[0], mla_w_uq[0], mla_w_o[0])
    gq = mla_g_q[0].reshape(1, -1)
    gkv = mla_g_kv[0].reshape(1, -1)
    wukv = mla_w_ukv[0].astype(BF16)
    tabs = _rope_tables(s)

    q, k, v, dq, dk, dv, gates = _proj_lat(x, mod3, wx, gq, wuq, gkv, wukv, tabs, tm)
    kc, vc, dkc, dvc = _proj_ctx(ctx, mod3, wc, gkv, wukv)

    o_mla = _mla_attn(q, kc, vc, k, v, tq, tk)
    o_diff = _diff_attn(diff_lambda[0], diff_g_subln[0].reshape(-1, 1), dq, dkc, dvc, dk, dv, tq // 2, tk)

    wr = moe_w_router[0].T
    wrh = wr.astype(BF16)
    wrl = (wr - wrh.astype(F32)).astype(BF16)
    x1, ha, hb, lgt = _mix(o_mla, o_diff, gates, x, mod3, wom, diff_w_o[0].astype(BF16), w_out[0].astype(BF16),
                           ln1_g[0].reshape(1, -1), ln1_b[0].reshape(1, -1), wrh, wrl, tm)

    pos, gate, starts = _route(lgt, moe_b_router[0], cap, tile)
    pos4 = pos.reshape(b, N_EXPERTS, 1, s)
    gate4 = gate.reshape(b, N_EXPERTS, 1, s)
    slots = _slots(starts, pos4, gate4, cap, tile, gather_chunk)
    row0 = (jnp.arange(b, dtype=I32) * s)[:, None, None]
    idx = (slots[..., 0].astype(I32) * 128 + slots[..., 1].astype(I32) + row0).reshape(-1)
    xa = _sc_gather(ha.reshape(b * s, -1), idx)
    xb = _sc_gather(hb.reshape(b * s, -1), idx)
    ye = _ffn(xa, xb, slots, moe_w1[0].astype(BF16), moe_w3[0].astype(BF16), moe_w2[0].astype(BF16))
    return _combine(starts, ye, pos4, x1, mod3, ln2_g[0].reshape(1, -1), ln2_b[0].reshape(1, -1), tile, scatter_chunk)
```

```python
import functools
import math

import numpy as np
import jax
import jax.numpy as jnp
from jax import lax
from jax.experimental import pallas as pl
from jax.experimental.pallas import tpu as pltpu
from jax.experimental.pallas import tpu_sc as plsc

F32 = jnp.float32
BF16 = jnp.bfloat16
I32 = jnp.int32
U32 = jnp.uint32

D_MODEL = 1024
DEPTH = 1
GRID_W = 64
ROPE_THETA = 10000.0
EPS = 1e-6
MLA_HEADS = 8
MLA_Q_RANK = 256
MLA_KV_RANK = 128
MLA_NOPE = 64
MLA_ROPE = 32
MLA_V = 64
DIFF_HEADS = 4
DIFF_HD = 64
N_EXPERTS = 16
EXPERT_FF = 1024
EC_CAPACITY = 2

MLA_Q_OFF = 0
MLA_KV_OFF = MLA_Q_OFF + MLA_Q_RANK
MLA_KR_OFF = MLA_KV_OFF + MLA_KV_RANK
DIFF_Q_OFF = MLA_KR_OFF + MLA_ROPE
DIFF_K_OFF = DIFF_Q_OFF + DIFF_HEADS * 2 * DIFF_HD
DIFF_V_OFF = DIFF_K_OFF + DIFF_HEADS * 2 * DIFF_HD
GATE_OFF = DIFF_V_OFF + DIFF_HEADS * 2 * DIFF_HD
N_IN = GATE_OFF + 2 * D_MODEL
DIFF_WIDTH = DIFF_HEADS * 2 * DIFF_HD
MLA_SCALE = (MLA_NOPE + MLA_ROPE) ** -0.5
DIFF_SCALE = DIFF_HD ** -0.5
DEEPNORM_ALPHA = (2 * DEPTH) ** 0.25
LAMBDA_INIT = 0.8 - 0.6 * math.exp(-0.3 * 0)
LOG2E = 1.4426950408889634
LAGGED_MAX_HEADROOM = 100.0

LANE = 128
MLA_W = MLA_HEADS * LANE
PX_Q = 0
PX_KV = PX_Q + MLA_Q_RANK
PX_KR = PX_KV + MLA_KV_RANK
PX_DQ = PX_KR + LANE
PX_DK = PX_DQ + DIFF_WIDTH
PX_DV = PX_DK + DIFF_WIDTH
PX_G = PX_DV + DIFF_WIDTH
PX_N = PX_G + 2 * D_MODEL
CX_KV = 0
CX_KR = CX_KV + MLA_KV_RANK
CX_DK = CX_KR + LANE
CX_DV = CX_DK + DIFF_WIDTH
CX_N = CX_DV + DIFF_WIDTH

VMEM_LIMIT = 56 * 1024 * 1024


def _cparams(sem):
    return pltpu.CompilerParams(dimension_semantics=sem, vmem_limit_bytes=VMEM_LIMIT)


def _rope(x, cos, sin_lo, sin_hi, quarter):
    return x * cos + pltpu.roll(x, LANE - quarter, 1) * sin_lo + pltpu.roll(x, quarter, 1) * sin_hi


def _rope_tables(seq):
    t = np.arange(seq)
    row = (t // GRID_W).astype(np.float64)
    col = (t % GRID_W).astype(np.float64)

    def axial(d):
        h = d // 2
        inv = ROPE_THETA ** (-(np.arange(h // 2, dtype=np.float64) * 2.0 / h))
        ar, ac = row[:, None] * inv, col[:, None] * inv
        cos = np.concatenate([np.cos(ar), np.cos(ar), np.cos(ac), np.cos(ac)], -1)
        sin = np.concatenate([np.sin(ar), np.sin(ar), np.sin(ac), np.sin(ac)], -1)
        return cos, sin

    def split(sin, d):
        first = (np.arange(sin.shape[1]) % (d // 2)) < d // 4
        return np.where(first, -sin, 0.0), np.where(first, 0.0, sin)

    c32, s32 = axial(MLA_ROPE)
    c64, s64 = axial(DIFF_HD)
    cos_m = np.ones((seq, LANE))
    sin_m = np.zeros((seq, LANE))
    cos_m[:, MLA_NOPE:MLA_NOPE + MLA_ROPE] = c32
    sin_m[:, MLA_NOPE:MLA_NOPE + MLA_ROPE] = s32
    cos_d = np.concatenate([c64, c64], -1)
    sin_d = np.concatenate([s64, s64], -1)
    tabs = (cos_m, *split(sin_m, MLA_ROPE), cos_d, *split(sin_d, DIFF_HD))
    return tuple(jnp.asarray(a, F32) for a in tabs)


def _rms(x, g):
    return x * lax.rsqrt(jnp.mean(x * x, axis=-1, keepdims=True) + EPS) * g


def _layer_norm(x, g, b):
    mu = jnp.mean(x, axis=-1, keepdims=True)
    xc = x - mu
    var = jnp.mean(xc * xc, axis=-1, keepdims=True)
    return xc * lax.rsqrt(var + EPS) * g + b


def _dot(a, b):
    return jnp.dot(a, b, preferred_element_type=F32)


def _dot_nt(a, b):
    return lax.dot_general(a, b, (((1,), (1,)), ((), ())), preferred_element_type=F32)


def _dot_tn(a, b):
    return lax.dot_general(a, b, (((0,), (0,)), ((), ())), preferred_element_type=F32)


def _ada_kernel(c_ref, w_ref, b_ref, o_ref):
    c = c_ref[...]
    s = (c / (1.0 + jnp.exp(-c))).astype(BF16)
    o_ref[...] = _dot(s, w_ref[...].astype(BF16)) + b_ref[...]


def _ada(cc, w_ada, b_ada):
    n = w_ada.shape[1]
    tn = 1024
    return pl.pallas_call(
        _ada_kernel,
        grid=(n // tn,),
        in_specs=[pl.BlockSpec((8, D_MODEL), lambda j: (0, 0)),
                  pl.BlockSpec((D_MODEL, tn), lambda j: (0, j)),
                  pl.BlockSpec((1, tn), lambda j: (0, j))],
        out_specs=pl.BlockSpec((8, tn), lambda j: (0, j)),
        out_shape=jax.ShapeDtypeStruct((8, n), F32),
        compiler_params=_cparams(("arbitrary",)),
        name="ada",
    )(cc, w_ada, b_ada.reshape(1, n))


def _proj_lat_kernel(x_ref, mod_ref, w_ref, gq_ref, wuq_ref, gkv_ref, wukv_ref,
                     cm_ref, slm_ref, shm_ref, cd_ref, sld_ref, shd_ref,
                     q_ref, k_ref, v_ref, dq_ref, dk_ref, dv_ref, g_ref):
    tm = x_ref.shape[1]
    sh = mod_ref[0, :, 0:D_MODEL]
    sc = mod_ref[0, :, D_MODEL:2 * D_MODEL]
    h = (x_ref[0] * (1.0 + sc) + sh).astype(BF16)

    def proj(a, n):
        return _dot(h, w_ref[:, a:a + n])

    lane = lax.broadcasted_iota(I32, (tm, LANE), 1)
    lo = lane < MLA_NOPE
    one0 = (lane == 0).astype(F32)

    def rope_m(t):
        return _rope(t, cm_ref[...], slm_ref[...], shm_ref[...], MLA_ROPE // 4)

    def rope_d(t):
        return _rope(t, cd_ref[...], sld_ref[...], shd_ref[...], DIFF_HD // 4)

    cq = _rms(proj(PX_Q, MLA_Q_RANK), gq_ref[...]).astype(BF16)
    q = _dot(cq, wuq_ref[...])
    for hd in range(MLA_HEADS):
        sl = slice(hd * LANE, (hd + 1) * LANE)
        q_ref[0, :, sl] = (rope_m(q[:, sl]) * (MLA_SCALE * LOG2E)).astype(BF16)

    ckv = _rms(proj(PX_KV, MLA_KV_RANK), gkv_ref[...]).astype(BF16)
    kv = _dot(ckv, wukv_ref[...])
    kr = rope_m(proj(PX_KR, LANE))
    for hd in range(MLA_HEADS):
        sl = slice(hd * LANE, (hd + 1) * LANE)
        kvh = kv[:, sl]
        k_ref[0, :, sl] = jnp.where(lo, kvh, kr).astype(BF16)
        v_ref[0, :, sl] = jnp.where(lo, one0, kvh).astype(BF16)

    dq = proj(PX_DQ, DIFF_WIDTH)
    dk = proj(PX_DK, DIFF_WIDTH)
    for hd in range(DIFF_HEADS):
        sl = slice(hd * LANE, (hd + 1) * LANE)
        qb = rope_d(dq[:, sl]) * (DIFF_SCALE * LOG2E)
        dq_ref[0, :, (2 * hd) * LANE:(2 * hd + 1) * LANE] = jnp.where(lo, qb, 0.0).astype(BF16)
        dq_ref[0, :, (2 * hd + 1) * LANE:(2 * hd + 2) * LANE] = jnp.where(lo, 0.0, qb).astype(BF16)
        dk_ref[0, :, sl] = rope_d(dk[:, sl]).astype(BF16)
    dv_ref[0] = proj(PX_DV, DIFF_WIDTH).astype(BF16)
    pg = proj(PX_G, 2 * D_MODEL)
    g_ref[0] = (1.0 / (1.0 + jnp.exp(-pg))).astype(BF16)


def _proj_ctx_kernel(x_ref, mod_ref, w_ref, gkv_ref, wukv_ref, k_ref, v_ref, dk_ref, dv_ref):
    tm = x_ref.shape[1]
    sh = mod_ref[0, :, 0:D_MODEL]
    sc = mod_ref[0, :, D_MODEL:2 * D_MODEL]
    h = (x_ref[0] * (1.0 + sc) + sh).astype(BF16)

    def proj(a, n):
        return _dot(h, w_ref[:, a:a + n])

    lane = lax.broadcasted_iota(I32, (tm, LANE), 1)
    lo = lane < MLA_NOPE
    one0 = (lane == 0).astype(F32)
    ckv = _rms(proj(CX_KV, MLA_KV_RANK), gkv_ref[...]).astype(BF16)
    kv = _dot(ckv, wukv_ref[...])
    kr = proj(CX_KR, LANE)
    for hd in range(MLA_HEADS):
        sl = slice(hd * LANE, (hd + 1) * LANE)
        kvh = kv[:, sl]
        k_ref[0, :, sl] = jnp.where(lo, kvh, kr).astype(BF16)
        v_ref[0, :, sl] = jnp.where(lo, one0, kvh).astype(BF16)
    dk_ref[0] = proj(CX_DK, DIFF_WIDTH).astype(BF16)
    dv_ref[0] = proj(CX_DV, DIFF_WIDTH).astype(BF16)


def _full(shape):
    nd = len(shape)
    return pl.BlockSpec(shape, lambda *_: (0,) * nd)


def _proj_lat(x, mod3, wx, gq, wuq, gkv, wukv, tabs, tm):
    b, s, _ = x.shape
    row = lambda w: pl.BlockSpec((1, tm, w), lambda bi, ti: (bi, ti, 0))
    tab = pl.BlockSpec((tm, LANE), lambda bi, ti: (ti, 0))
    outs = [(MLA_W, BF16), (MLA_W, BF16), (MLA_W, BF16), (2 * DIFF_WIDTH, BF16), (DIFF_WIDTH, BF16),
            (DIFF_WIDTH, BF16), (2 * D_MODEL, BF16)]
    return pl.pallas_call(
        _proj_lat_kernel,
        grid=(b, s // tm),
        in_specs=[row(D_MODEL),
                  pl.BlockSpec((1, 1, 6 * D_MODEL), lambda bi, ti: (bi, 0, 0)),
                  _full(wx.shape), _full(gq.shape), _full(wuq.shape),
                  _full(gkv.shape), _full(wukv.shape)] + [tab] * len(tabs),
        out_specs=[row(w) for w, _ in outs],
        out_shape=[jax.ShapeDtypeStruct((b, s, w), dt) for w, dt in outs],
        compiler_params=_cparams(("arbitrary", "arbitrary")),
        name="proj_lat",
    )(x, mod3, wx, gq, wuq, gkv, wukv, *tabs)


def _proj_ctx(ctx, mod3, wc, gkv, wukv):
    b, s, _ = ctx.shape
    row = lambda w: pl.BlockSpec((1, s, w), lambda bi: (bi, 0, 0))
    outs = [MLA_W, MLA_W, DIFF_WIDTH, DIFF_WIDTH]
    return pl.pallas_call(
        _proj_ctx_kernel,
        grid=(b,),
        in_specs=[row(D_MODEL),
                  pl.BlockSpec((1, 1, 6 * D_MODEL), lambda bi: (b, 0, 0)),
                  _full(wc.shape), _full(gkv.shape), _full(wukv.shape)],
        out_specs=[row(w) for w in outs],
        out_shape=[jax.ShapeDtypeStruct((b, s, w), BF16) for w in outs],
        compiler_params=_cparams(("arbitrary",)),
        name="proj_ctx",
    )(ctx, mod3, wc, gkv, wukv)


ONES_ROWS = 16


def _flash_t(q_t, kc_ref, vc_ref, kl_ref, vl_ref, p_ref, g_ref, m_ref, x_ref, acc_ref, tk, ones_rows):
    n = kl_ref.shape[1] // tk

    def chunk(ref, i):
        return ref[0, pl.ds(pl.multiple_of(i * tk, tk), tk), :]

    def pv(v, p):
        if ones_rows:
            return _dot(jnp.concatenate([v.T, jnp.ones((ONES_ROWS, v.shape[0]), BF16)], axis=0), p)
        return _dot_tn(v, p)

    def scores_exp(i):
        stab = m_ref[...]
        s_t = _dot(chunk(kl_ref, i), q_t)
        p_ref[i % 2] = jnp.exp2(s_t - stab).astype(BF16)
        cmax = jnp.max(s_t, axis=0, keepdims=True)
        m_new = jnp.maximum(stab, cmax)
        g_ref[(i + 1) % 2] = jnp.exp2(stab - m_new)
        m_ref[...] = m_new
        x_ref[...] = jnp.maximum(x_ref[...], cmax - stab)

    def values(i):
        acc_ref[...] = g_ref[i % 2] * acc_ref[...] + pv(chunk(vl_ref, i), p_ref[i % 2])

    def exact_chunk(k, v, m, first):
        out = pv(v, jnp.exp2(_dot(k, q_t) - m).astype(BF16))
        acc_ref[...] = out if first else acc_ref[...] + out

    m0 = jnp.max(_dot(kc_ref[0], q_t), axis=0, keepdims=True)
    m_ref[...] = m0
    g_ref[0] = jnp.ones_like(m0)
    x_ref[...] = jnp.zeros_like(m0)
    exact_chunk(kc_ref[0], vc_ref[0], m0, True)

    scores_exp(0)

    def body(i, carry):
        values(i - 1)
        scores_exp(i)
        return carry

    lax.fori_loop(1, n, body, 0, unroll=True)
    values(n - 1)

    @pl.when(jnp.max(x_ref[...]) > LAGGED_MAX_HEADROOM)
    def _():
        m = m_ref[...]
        exact_chunk(kc_ref[0], vc_ref[0], m, True)

        def redo(i, carry):
            exact_chunk(chunk(kl_ref, i), chunk(vl_ref, i), m, False)
            return carry

        lax.fori_loop(0, n, redo, 0)


def _attn_scratch(tk, nq, ones_rows):
    v = pltpu.VMEM
    rows = LANE + (ONES_ROWS if ones_rows else 0)
    return [v((2, tk, nq), BF16), v((2, 1, nq), F32), v((1, nq), F32), v((1, nq), F32), v((rows, nq), F32)]


def _mla_attn_kernel(q_ref, kc_ref, vc_ref, kl_ref, vl_ref, o_ref, p_ref, g_ref, m_ref, x_ref, acc_ref, *, tk):
    _flash_t(q_ref[0].T, kc_ref, vc_ref, kl_ref, vl_ref, p_ref, g_ref, m_ref, x_ref, acc_ref, tk, False)
    acc = acc_ref[...]
    o_ref[0] = (acc / acc[0:1, :]).T.astype(BF16)


def _diff_attn_kernel(lam_ref, gs_ref, q1_ref, q2_ref, kc_ref, vc_ref, kl_ref, vl_ref, o_ref,
                      p_ref, g_ref, m_ref, x_ref, acc_ref, *, tk):
    tq = q1_ref.shape[1]
    q_t = jnp.concatenate([q1_ref[0].T, q2_ref[0].T], axis=1)
    _flash_t(q_t, kc_ref, vc_ref, kl_ref, vl_ref, p_ref, g_ref, m_ref, x_ref, acc_ref, tk, True)
    o = acc_ref[0:LANE, :] / acc_ref[LANE:LANE + 1, :]
    dl = lam_ref[...]
    lam = (jnp.exp(jnp.sum(dl[0:1] * dl[1:2], axis=-1, keepdims=True))
           - jnp.exp(jnp.sum(dl[2:3] * dl[3:4], axis=-1, keepdims=True)) + LAMBDA_INIT)
    w = o[:, :tq] - lam * o[:, tq:]
    w = w * lax.rsqrt(jnp.mean(w * w, axis=0, keepdims=True) + EPS) * gs_ref[...] * (1.0 - LAMBDA_INIT)
    o_ref[0] = w.T.astype(BF16)


def _mla_attn(q, kc, vc, kl, vl, tq, tk):
    b, s, _ = q.shape
    sc = kc.shape[1]
    assert s % tk == 0 and s % tq == 0
    qs = pl.BlockSpec((1, tq, LANE), lambda bi, hi, qi: (bi, qi, hi))
    cs = pl.BlockSpec((1, sc, LANE), lambda bi, hi, qi: (bi, 0, hi))
    ls = pl.BlockSpec((1, s, LANE), lambda bi, hi, qi: (bi, 0, hi))
    return pl.pallas_call(
        functools.partial(_mla_attn_kernel, tk=tk),
        grid=(b, MLA_HEADS, s // tq),
        in_specs=[qs, cs, cs, ls, ls],
        out_specs=qs,
        out_shape=jax.ShapeDtypeStruct((b, s, MLA_W), BF16),
        scratch_shapes=_attn_scratch(tk, tq, False),
        compiler_params=_cparams(("arbitrary", "arbitrary", "arbitrary")),
        name="mla_attn",
    )(q, kc, vc, kl, vl)


def _diff_attn(lam4, g_subln, dq, kc, vc, kl, vl, tq, tk):
    b, s, _ = dq.shape
    sc = kc.shape[1]
    q1 = pl.BlockSpec((1, tq, LANE), lambda bi, hi, qi: (bi, qi, 2 * hi))
    q2 = pl.BlockSpec((1, tq, LANE), lambda bi, hi, qi: (bi, qi, 2 * hi + 1))
    cs = pl.BlockSpec((1, sc, LANE), lambda bi, hi, qi: (bi, 0, hi))
    ls = pl.BlockSpec((1, s, LANE), lambda bi, hi, qi: (bi, 0, hi))
    return pl.pallas_call(
        functools.partial(_diff_attn_kernel, tk=tk),
        grid=(b, DIFF_HEADS, s // tq),
        in_specs=[_full(lam4.shape), _full(g_subln.shape), q1, q2, cs, cs, ls, ls],
        out_specs=pl.BlockSpec((1, tq, LANE), lambda bi, hi, qi: (bi, qi, hi)),
        out_shape=jax.ShapeDtypeStruct((b, s, DIFF_WIDTH), BF16),
        scratch_shapes=_attn_scratch(tk, 2 * tq, True),
        compiler_params=_cparams(("arbitrary", "arbitrary", "arbitrary")),
        name="diff_attn",
    )(lam4, g_subln, dq, dq, kc, vc, kl, vl)


def _mix_kernel(om_ref, od_ref, g_ref, x_ref, mod_ref, wom_ref, wod_ref, wout_ref, lng_ref, lnb_ref,
                wrh_ref, wrl_ref, x1_ref, ha_ref, hb_ref, lg_ref):
    gm = g_ref[0, :, 0:D_MODEL]
    gd = g_ref[0, :, D_MODEL:2 * D_MODEL]
    y = gm * _dot(om_ref[0], wom_ref[...]) + gd * _dot(od_ref[0], wod_ref[...])
    z = _dot(y.astype(BF16), wout_ref[...])
    g1 = mod_ref[0, :, 2 * D_MODEL:3 * D_MODEL]
    sh2 = mod_ref[0, :, 3 * D_MODEL:4 * D_MODEL]
    sc2 = mod_ref[0, :, 4 * D_MODEL:5 * D_MODEL]
    x1 = _layer_norm(DEEPNORM_ALPHA * x_ref[0] + g1 * z, lng_ref[...], lnb_ref[...])
    x1_ref[0] = x1
    h2 = x1 * (1.0 + sc2) + sh2
    hi = h2.astype(BF16)
    lo = (h2 - hi.astype(F32)).astype(BF16)
    bits = pltpu.bitcast(hi.astype(F32), U32)
    q4 = D_MODEL // 4
    ha_ref[0] = bits[:, 0:q4] | (bits[:, q4:2 * q4] >> 16)
    hb_ref[0] = bits[:, 2 * q4:3 * q4] | (bits[:, 3 * q4:4 * q4] >> 16)
    lg_ref[0] = _dot_nt(wrh_ref[...], hi) + (_dot_nt(wrh_ref[...], lo) + _dot_nt(wrl_ref[...], hi))


def _mix(om, od, gates, x, mod3, wom, wod, wout, lng, lnb, wrh, wrl, tm):
    b, s, _ = x.shape
    row = lambda w: pl.BlockSpec((1, tm, w), lambda bi, ti: (bi, ti, 0))
    return pl.pallas_call(
        _mix_kernel,
        grid=(b, s // tm),
        in_specs=[row(MLA_W), row(DIFF_WIDTH), row(2 * D_MODEL), row(D_MODEL),
                  pl.BlockSpec((1, 1, 6 * D_MODEL), lambda bi, ti: (bi, 0, 0)),
                  _full(wom.shape), _full(wod.shape), _full(wout.shape), _full(lng.shape), _full(lnb.shape),
                  _full(wrh.shape), _full(wrl.shape)],
        out_specs=[row(D_MODEL), row(D_MODEL // 4), row(D_MODEL // 4),
                   pl.BlockSpec((1, N_EXPERTS, tm), lambda bi, ti: (bi, 0, ti))],
        out_shape=[jax.ShapeDtypeStruct((b, s, D_MODEL), F32), jax.ShapeDtypeStruct((b, s, D_MODEL // 4), U32),
                   jax.ShapeDtypeStruct((b, s, D_MODEL // 4), U32), jax.ShapeDtypeStruct((b, N_EXPERTS, s), F32)],
        compiler_params=_cparams(("arbitrary", "arbitrary")),
        name="mix",
    )(om, od, gates, x, mod3, wom, wod, wout, lng, lnb, wrh, wrl)


def _route_kernel(lg_ref, br_ref, pos_ref, gate_ref, st_ref, *, cap, tile):
    e, s = lg_ref.shape[1], lg_ref.shape[2]
    lg = lg_ref[0] + br_ref[...]
    ex = jnp.exp(lg - jnp.max(lg, axis=0, keepdims=True))
    aff = ex / jnp.sum(ex, axis=0, keepdims=True)
    gate_ref[0] = aff
    bits = pltpu.bitcast(aff, I32)

    def count(mask):
        return jnp.sum(mask.astype(I32), axis=1, keepdims=True)

    def search(i, thr):
        cand = thr | (jnp.int32(1) << (30 - i))
        return jnp.where(count(bits >= cand) >= cap, cand, thr)

    thr = lax.fori_loop(0, 31, search, jnp.zeros((e, 1), I32))
    gt = bits > thr
    eq = bits == thr
    need = cap - count(gt)

    r = lax.broadcasted_iota(I32, (LANE, LANE), 0)
    c = lax.broadcasted_iota(I32, (LANE, LANE), 1)
    tri = (r < c).astype(BF16)
    lane = lax.broadcasted_iota(I32, (e, LANE), 1)

    def prefix(mask_fn, emit):
        carry = jnp.zeros((e, 1), F32)
        for j in range(s // LANE):
            m = mask_fn(j)
            emit(j, carry + _dot(m.astype(BF16), tri), m, carry)
            carry = carry + jnp.sum(m.astype(F32), axis=1, keepdims=True)

    def eq_blk(j):
        return eq[:, j * LANE:(j + 1) * LANE]

    def emit_sel(j, rank, m, carry):
        sel = gt[:, j * LANE:(j + 1) * LANE] | (m & (rank < need.astype(F32)))
        pos_ref[0, :, j * LANE:(j + 1) * LANE] = sel.astype(I32)

    prefix(eq_blk, emit_sel)

    starts = [jnp.zeros((e, LANE), I32)]

    def sel_blk(j):
        return pos_ref[0, :, j * LANE:(j + 1) * LANE] > 0

    def emit_pos(j, rank, m, carry):
        if (j * LANE) % tile == 0:
            starts[0] = jnp.where(lane == (j * LANE) // tile, carry.astype(I32), starts[0])
        pos_ref[0, :, j * LANE:(j + 1) * LANE] = jnp.where(m, rank.astype(I32), -1)

    prefix(sel_blk, emit_pos)
    st_ref[0] = jnp.where(lane == s // tile, cap, starts[0])


def _route(lgt, b_router, cap, tile):
    b, e, s = lgt.shape
    blk = pl.BlockSpec((1, e, s), lambda bi: (bi, 0, 0))
    return pl.pallas_call(
        functools.partial(_route_kernel, cap=cap, tile=tile),
        grid=(b,),
        in_specs=[blk, _full((e, 1))],
        out_specs=[blk, blk, pl.BlockSpec((1, e, LANE), lambda bi: (bi, 0, 0))],
        out_shape=[jax.ShapeDtypeStruct((b, e, s), I32), jax.ShapeDtypeStruct((b, e, s), F32),
                   jax.ShapeDtypeStruct((b, e, LANE), I32)],
        compiler_params=_cparams(("arbitrary",)),
        name="route",
    )(lgt, b_router.reshape(e, 1))


SLOT_COLS = 8


def _slots_kernel(st_ref, pos_ref, gate_ref, o_ref, acc_ref, *, chunk, tile):
    bi, ei = pl.program_id(0), pl.program_id(1)
    cap, s = acc_ref.shape[0], pos_ref.shape[3]
    acc_ref[...] = jnp.zeros_like(acc_ref)
    slot = lax.broadcasted_iota(I32, (chunk, tile), 0)
    row = lax.broadcasted_iota(I32, (SLOT_COLS, tile), 0)
    for u in range(s // tile):
        b0, b1 = st_ref[bi, ei, u], st_ref[bi, ei, u + 1]
        cols = slice(u * tile, (u + 1) * tile)

        @pl.when(b1 > b0)
        def _():
            t = lax.broadcasted_iota(I32, (SLOT_COLS, tile), 1) + u * tile
            g = jnp.broadcast_to(gate_ref[0, 0, :, cols], (SLOT_COLS, tile))
            g1 = g.astype(BF16).astype(F32)
            g2 = (g - g1).astype(BF16).astype(F32)
            g3 = g - g1 - g2
            rec = jnp.where(row == 0, (t >> 7).astype(F32),
                            jnp.where(row == 1, (t & 127).astype(F32),
                                      jnp.where(row == 2, g1, jnp.where(row == 3, g2,
                                                                        jnp.where(row == 4, g3, 0.0)))))
            rec = rec.astype(BF16)
            pos = pos_ref[0, 0, :, cols]
            for c in range(cap // chunk):
                lo, hi = c * chunk, (c + 1) * chunk

                @pl.when((b0 < hi) & (b1 > lo))
                def _():
                    hit = (slot == (pos - lo)).astype(BF16)
                    acc_ref[lo:hi, :] += _dot_nt(hit, rec)

    o_ref[0, 0] = acc_ref[...]


def _slots(starts, pos4, gate4, cap, tile, chunk):
    b, e, _, s = pos4.shape
    blk = pl.BlockSpec((1, 1, 1, s), lambda bi, ei, st: (bi, ei, 0, 0))
    gs = pltpu.PrefetchScalarGridSpec(
        num_scalar_prefetch=1,
        grid=(b, e),
        in_specs=[blk, blk],
        out_specs=pl.BlockSpec((1, 1, cap, SLOT_COLS), lambda bi, ei, st: (bi, ei, 0, 0)),
        scratch_shapes=[pltpu.VMEM((cap, SLOT_COLS), F32)],
    )
    return pl.pallas_call(
        functools.partial(_slots_kernel, chunk=chunk, tile=tile),
        grid_spec=gs,
        out_shape=jax.ShapeDtypeStruct((b, e, cap, SLOT_COLS), F32),
        compiler_params=_cparams(("arbitrary", "arbitrary")),
        name="slots",
    )(starts, pos4, gate4)


SC_GATHER_WINDOW = 128


def _sc_gather(table, idx):
    n = idx.shape[0]
    w = table.shape[1]
    mesh = plsc.VectorSubcoreMesh(core_axis_name="core", subcore_axis_name="subcore")

    @pl.kernel(out_type=jax.ShapeDtypeStruct((n, w), table.dtype), mesh=mesh)
    def gather(t_hbm, i_hbm, o_hbm):
        def body(i_vmem, o_vmem):
            pltpu.sync_copy(t_hbm.at[i_vmem.at[0]], o_vmem)

        pltpu.emit_pipeline(
            body,
            grid=(n // SC_GATHER_WINDOW,),
            in_specs=[pl.BlockSpec((1, SC_GATHER_WINDOW), index_map=lambda i: (0, i))],
            out_specs=[pl.BlockSpec((SC_GATHER_WINDOW, w), index_map=lambda i: (i, 0))],
            core_axis_name=("core", "subcore"),
            dimension_semantics=(pltpu.PARALLEL,),
        )(i_hbm, o_hbm)

    return gather(table, idx.reshape(1, n))


def _ffn_kernel(xa_ref, xb_ref, sl_ref, w1_ref, w3_ref, w2_ref, ye_ref):
    def halves(words):
        return pltpu.bitcast(words & jnp.uint32(0xFFFF0000), F32), pltpu.bitcast(words << 16, F32)

    a_hi, a_lo = halves(xa_ref[...])
    b_hi, b_lo = halves(xb_ref[...])
    xe = jnp.concatenate([a_hi, a_lo, b_hi, b_lo], axis=1).astype(BF16)
    sl = sl_ref[0, 0]
    gate = sl[:, 2:3] + sl[:, 3:4] + sl[:, 4:5]
    a = _dot(xe, w1_ref[0])
    hid = ((a / (1.0 + jnp.exp(-a))) * _dot(xe, w3_ref[0])).astype(BF16)
    ye_ref[0, 0] = (_dot(hid, w2_ref[0]) * gate).astype(BF16)


def _ffn(xa, xb, slots, w1, w3, w2):
    b, e, cap, _ = slots.shape
    d = w2.shape[2]
    xspec = pl.BlockSpec((cap, xa.shape[1]), lambda bi, ei: (bi * e + ei, 0))
    wspec = lambda shp: pl.BlockSpec((1,) + shp, lambda bi, ei: (ei, 0, 0))
    return pl.pallas_call(
        _ffn_kernel,
        grid=(b, e),
        in_specs=[xspec, xspec, pl.BlockSpec((1, 1, cap, SLOT_COLS), lambda bi, ei: (bi, ei, 0, 0)),
                  wspec(w1.shape[1:]), wspec(w3.shape[1:]), wspec(w2.shape[1:])],
        out_specs=pl.BlockSpec((1, 1, cap, d), lambda bi, ei: (bi, ei, 0, 0)),
        out_shape=jax.ShapeDtypeStruct((b, e, cap, d), BF16),
        compiler_params=_cparams(("arbitrary", "arbitrary")),
        name="ffn",
    )(xa, xb, slots, w1, w3, w2)


COMBINE_EXPERTS = 4


def _combine_kernel(st_ref, ye_ref, pos_ref, x1_ref, mod_ref, lng_ref, lnb_ref, o_ref, acc_ref, *, chunk):
    bi, ti, ei = pl.program_id(0), pl.program_id(1), pl.program_id(2)
    group, cap, tile = ye_ref.shape[1], ye_ref.shape[2], x1_ref.shape[1]

    @pl.when(ei == 0)
    def _():
        acc_ref[...] = jnp.zeros_like(acc_ref)

    slot = lax.broadcasted_iota(I32, (chunk, tile), 0)
    for g in range(group):
        expert = ei * group + g
        b0, b1 = st_ref[bi, expert, ti], st_ref[bi, expert, ti + 1]
        for c in range(cap // chunk):
            lo, hi = c * chunk, (c + 1) * chunk

            @pl.when((b0 < hi) & (b1 > lo))
            def _():
                hit = (slot == (pos_ref[0, g] - lo)).astype(BF16)
                acc_ref[...] += _dot_tn(hit, ye_ref[0, g, lo:hi, :])

    @pl.when(ei == pl.num_programs(2) - 1)
    def _():
        g2 = mod_ref[0, :, 5 * D_MODEL:6 * D_MODEL]
        o_ref[0] = _layer_norm(DEEPNORM_ALPHA * x1_ref[0] + g2 * acc_ref[...], lng_ref[...], lnb_ref[...])


def _combine(starts, ye, pos4, x1, mod3, lng, lnb, tile, chunk):
    b, s, d = x1.shape
    e, cap = ye.shape[1], ye.shape[2]
    g = COMBINE_EXPERTS
    gs = pltpu.PrefetchScalarGridSpec(
        num_scalar_prefetch=1,
        grid=(b, s // tile, e // g),
        in_specs=[pl.BlockSpec((1, g, cap, d), lambda bi, ti, ei, st: (bi, ei, 0, 0)),
                  pl.BlockSpec((1, g, 1, tile), lambda bi, ti, ei, st: (bi, ei, 0, ti)),
                  pl.BlockSpec((1, tile, d), lambda bi, ti, ei, st: (bi, ti, 0)),
                  pl.BlockSpec((1, 1, 6 * D_MODEL), lambda bi, ti, ei, st: (bi, 0, 0)),
                  pl.BlockSpec((1, d), lambda bi, ti, ei, st: (0, 0)),
                  pl.BlockSpec((1, d), lambda bi, ti, ei, st: (0, 0))],
        out_specs=pl.BlockSpec((1, tile, d), lambda bi, ti, ei, st: (bi, ti, 0)),
        scratch_shapes=[pltpu.VMEM((tile, d), F32)],
    )
    return pl.pallas_call(
        functools.partial(_combine_kernel, chunk=chunk),
        grid_spec=gs,
        out_shape=jax.ShapeDtypeStruct((b, s, d), F32),
        compiler_params=_cparams(("arbitrary", "arbitrary", "arbitrary")),
        name="combine",
    )(starts, ye, pos4, x1, mod3, lng, lnb)


def _head_pad(w, n_heads, width, offset=0):
    k = w.shape[0]
    w3 = w.reshape(k, n_heads, width)
    out = jnp.zeros((k, n_heads, LANE), w.dtype).at[:, :, offset:offset + width].set(w3)
    return out.reshape(k, n_heads * LANE)


def _prep_weights(w_in, w_uq, w_o_mla):
    kr = w_in[:, MLA_KR_OFF:DIFF_Q_OFF]
    dqw = w_in[:, DIFF_Q_OFF:DIFF_K_OFF]
    dkw = w_in[:, DIFF_K_OFF:DIFF_V_OFF]
    dvw = w_in[:, DIFF_V_OFF:GATE_OFF]
    kr_pad = _head_pad(kr, 1, MLA_ROPE, MLA_NOPE)
    wx = jnp.concatenate([w_in[:, MLA_Q_OFF:MLA_KR_OFF], kr_pad, dqw, dkw, dvw, w_in[:, GATE_OFF:]],
                         axis=1).astype(BF16)
    wc = jnp.concatenate([w_in[:, MLA_KV_OFF:MLA_KR_OFF], kr_pad, dkw, dvw], axis=1).astype(BF16)
    wuq = _head_pad(w_uq, MLA_HEADS, MLA_NOPE + MLA_ROPE).astype(BF16)
    wom = jnp.zeros((MLA_HEADS, LANE, D_MODEL), w_o_mla.dtype).at[:, MLA_NOPE:, :].set(
        w_o_mla.reshape(MLA_HEADS, MLA_V, D_MODEL)).reshape(MLA_W, D_MODEL).astype(BF16)
    return wx, wc, wuq, wom


def kernel(x, c, ctx, c_ctx, w_ada, b_ada, w_in, mla_g_q, mla_w_uq, mla_g_kv, mla_w_ukv, mla_w_o, diff_lambda,
           diff_g_subln, diff_w_o, w_out, ln1_g, ln1_b, moe_w_router, moe_b_router, moe_w1, moe_w3, moe_w2,
           ln2_g, ln2_b):
    b, s, d = x.shape
    assert d == D_MODEL and w_ada.shape[0] == DEPTH == 1 and b < 8
    cap = EC_CAPACITY * s // N_EXPERTS
    tile = min(1024, s)
    gather_chunk = min(128, cap)
    scatter_chunk = min(256, cap)
    tm = min(512, s)
    tq = min(2048, s)
    tk = min(2048, s)

    cc = jnp.zeros((8, d), F32).at[:b].set(c).at[b].set(c_ctx)
    mod3 = _ada(cc, w_ada[0], b_ada[0]).reshape(8, 1, 6 * d)

    wx, wc, wuq, wom = _prep_weights(w_in[0], mla_w_uq[0], mla_w_o[0])
    gq = mla_g_q[0].reshape(1, -1)
    gkv = mla_g_kv[0].reshape(1, -1)
    wukv = mla_w_ukv[0].astype(BF16)
    tabs = _rope_tables(s)

    q, k, v, dq, dk, dv, gates = _proj_lat(x, mod3, wx, gq, wuq, gkv, wukv, tabs, tm)
    kc, vc, dkc, dvc = _proj_ctx(ctx, mod3, wc, gkv, wukv)

    o_mla = _mla_attn(q, kc, vc, k, v, tq, tk)
    o_diff = _diff_attn(diff_lambda[0], diff_g_subln[0].reshape(-1, 1), dq, dkc, dvc, dk, dv, tq // 2, tk)

    wr = moe_w_router[0].T
    wrh = wr.astype(BF16)
    wrl = (wr - wrh.astype(F32)).astype(BF16)
    x1, ha, hb, lgt = _mix(o_mla, o_diff, gates, x, mod3, wom, diff_w_o[0].astype(BF16), w_out[0].astype(BF16),
                           ln1_g[0].reshape(1, -1), ln1_b[0].reshape(1, -1), wrh, wrl, tm)

    pos, gate, starts = _route(lgt, moe_b_router[0], cap, tile)
    pos4 = pos.reshape(b, N_EXPERTS, 1, s)
    gate4 = gate.reshape(b, N_EXPERTS, 1, s)
    slots = _slots(starts, pos4, gate4, cap, tile, gather_chunk)
    row0 = (jnp.arange(b, dtype=I32) * s)[:, None, None]
    idx = (slots[..., 0].astype(I32) * 128 + slots[..., 1].astype(I32) + row0).reshape(-1)
    xa = _sc_gather(ha.reshape(b * s, -1), idx)
    xb = _sc_gather(hb.reshape(b * s, -1), idx)
    ye = _ffn(xa, xb, slots, moe_w1[0].astype(BF16), moe_w3[0].astype(BF16), moe_w2[0].astype(BF16))
    return _combine(starts, ye, pos4, x1, mod3, ln2_g[0].reshape(1, -1), ln2_b[0].reshape(1, -1), tile, scatter_chunk)
```

```python
import functools
import math

import numpy as np
import jax
import jax.numpy as jnp
from jax import lax
from jax.experimental import pallas as pl
from jax.experimental.pallas import tpu as pltpu
from jax.experimental.pallas import tpu_sc as plsc

F32 = jnp.float32
BF16 = jnp.bfloat16
I32 = jnp.int32
U32 = jnp.uint32

D_MODEL = 1024
DEPTH = 1
GRID_W = 64
ROPE_THETA = 10000.0
EPS = 1e-6
MLA_HEADS = 8
MLA_Q_RANK = 256
MLA_KV_RANK = 128
MLA_NOPE = 64
MLA_ROPE = 32
MLA_V = 64
DIFF_HEADS = 4
DIFF_HD = 64
N_EXPERTS = 16
EXPERT_FF = 1024
EC_CAPACITY = 2

MLA_Q_OFF = 0
MLA_KV_OFF = MLA_Q_OFF + MLA_Q_RANK
MLA_KR_OFF = MLA_KV_OFF + MLA_KV_RANK
DIFF_Q_OFF = MLA_KR_OFF + MLA_ROPE
DIFF_K_OFF = DIFF_Q_OFF + DIFF_HEADS * 2 * DIFF_HD
DIFF_V_OFF = DIFF_K_OFF + DIFF_HEADS * 2 * DIFF_HD
GATE_OFF = DIFF_V_OFF + DIFF_HEADS * 2 * DIFF_HD
N_IN = GATE_OFF + 2 * D_MODEL
DIFF_WIDTH = DIFF_HEADS * 2 * DIFF_HD
MLA_SCALE = (MLA_NOPE + MLA_ROPE) ** -0.5
DIFF_SCALE = DIFF_HD ** -0.5
DEEPNORM_ALPHA = (2 * DEPTH) ** 0.25
LAMBDA_INIT = 0.8 - 0.6 * math.exp(-0.3 * 0)
LOG2E = 1.4426950408889634
LAGGED_MAX_HEADROOM = 100.0

LANE = 128
MLA_W = MLA_HEADS * LANE
PX_Q = 0
PX_KV = PX_Q + MLA_Q_RANK
PX_KR = PX_KV + MLA_KV_RANK
PX_DQ = PX_KR + LANE
PX_DK = PX_DQ + DIFF_WIDTH
PX_DV = PX_DK + DIFF_WIDTH
PX_G = PX_DV + DIFF_WIDTH
PX_N = PX_G + 2 * D_MODEL
CX_KV = 0
CX_KR = CX_KV + MLA_KV_RANK
CX_DK = CX_KR + LANE
CX_DV = CX_DK + DIFF_WIDTH
CX_N = CX_DV + DIFF_WIDTH

VMEM_LIMIT = 56 * 1024 * 1024


def _cparams(sem):
    return pltpu.CompilerParams(dimension_semantics=sem, vmem_limit_bytes=VMEM_LIMIT)


def _rope(x, cos, sin_lo, sin_hi, quarter):
    return x * cos + pltpu.roll(x, LANE - quarter, 1) * sin_lo + pltpu.roll(x, quarter, 1) * sin_hi


def _rope_tables(seq):
    t = np.arange(seq)
    row = (t // GRID_W).astype(np.float64)
    col = (t % GRID_W).astype(np.float64)

    def axial(d):
        h = d // 2
        inv = ROPE_THETA ** (-(np.arange(h // 2, dtype=np.float64) * 2.0 / h))
        ar, ac = row[:, None] * inv, col[:, None] * inv
        cos = np.concatenate([np.cos(ar), np.cos(ar), np.cos(ac), np.cos(ac)], -1)
        sin = np.concatenate([np.sin(ar), np.sin(ar), np.sin(ac), np.sin(ac)], -1)
        return cos, sin

    def split(sin, d):
        first = (np.arange(sin.shape[1]) % (d // 2)) < d // 4
        return np.where(first, -sin, 0.0), np.where(first, 0.0, sin)

    c32, s32 = axial(MLA_ROPE)
    c64, s64 = axial(DIFF_HD)
    cos_m = np.ones((seq, LANE))
    sin_m = np.zeros((seq, LANE))
    cos_m[:, MLA_NOPE:MLA_NOPE + MLA_ROPE] = c32
    sin_m[:, MLA_NOPE:MLA_NOPE + MLA_ROPE] = s32
    cos_d = np.concatenate([c64, c64], -1)
    sin_d = np.concatenate([s64, s64], -1)
    tabs = (cos_m, *split(sin_m, MLA_ROPE), cos_d, *split(sin_d, DIFF_HD))
    return tuple(jnp.asarray(a, F32) for a in tabs)


def _rms(x, g):
    return x * lax.rsqrt(jnp.mean(x * x, axis=-1, keepdims=True) + EPS) * g


def _layer_norm(x, g, b):
    mu = jnp.mean(x, axis=-1, keepdims=True)
    xc = x - mu
    var = jnp.mean(xc * xc, axis=-1, keepdims=True)
    return xc * lax.rsqrt(var + EPS) * g + b


def _dot(a, b):
    return jnp.dot(a, b, preferred_element_type=F32)


def _dot_nt(a, b):
    return lax.dot_general(a, b, (((1,), (1,)), ((), ())), preferred_element_type=F32)


def _dot_tn(a, b):
    return lax.dot_general(a, b, (((0,), (0,)), ((), ())), preferred_element_type=F32)


def _ada_kernel(c_ref, w_ref, b_ref, o_ref):
    c = c_ref[...]
    s = (c / (1.0 + jnp.exp(-c))).astype(BF16)
    o_ref[...] = _dot(s, w_ref[...].astype(BF16)) + b_ref[...]


def _ada(cc, w_ada, b_ada):
    n = w_ada.shape[1]
    tn = 1024
    return pl.pallas_call(
        _ada_kernel,
        grid=(n // tn,),
        in_specs=[pl.BlockSpec((8, D_MODEL), lambda j: (0, 0)),
                  pl.BlockSpec((D_MODEL, tn), lambda j: (0, j)),
                  pl.BlockSpec((1, tn), lambda j: (0, j))],
        out_specs=pl.BlockSpec((8, tn), lambda j: (0, j)),
        out_shape=jax.ShapeDtypeStruct((8, n), F32),
        compiler_params=_cparams(("arbitrary",)),
        name="ada",
    )(cc, w_ada, b_ada.reshape(1, n))


def _proj_lat_kernel(x_ref, mod_ref, w_ref, gq_ref, wuq_ref, gkv_ref, wukv_ref,
                     cm_ref, slm_ref, shm_ref, cd_ref, sld_ref, shd_ref,
                     q_ref, k_ref, v_ref, dq_ref, dk_ref, dv_ref, g_ref):
    tm = x_ref.shape[1]
    sh = mod_ref[0, :, 0:D_MODEL]
    sc = mod_ref[0, :, D_MODEL:2 * D_MODEL]
    h = (x_ref[0] * (1.0 + sc) + sh).astype(BF16)

    def proj(a, n):
        return _dot(h, w_ref[:, a:a + n])

    lane = lax.broadcasted_iota(I32, (tm, LANE), 1)
    lo = lane < MLA_NOPE
    one0 = (lane == 0).astype(F32)

    def rope_m(t):
        return _rope(t, cm_ref[...], slm_ref[...], shm_ref[...], MLA_ROPE // 4)

    def rope_d(t):
        return _rope(t, cd_ref[...], sld_ref[...], shd_ref[...], DIFF_HD // 4)

    cq = _rms(proj(PX_Q, MLA_Q_RANK), gq_ref[...]).astype(BF16)
    q = _dot(cq, wuq_ref[...])
    for hd in range(MLA_HEADS):
        sl = slice(hd * LANE, (hd + 1) * LANE)
        q_ref[0, :, sl] = (rope_m(q[:, sl]) * (MLA_SCALE * LOG2E)).astype(BF16)

    ckv = _rms(proj(PX_KV, MLA_KV_RANK), gkv_ref[...]).astype(BF16)
    kv = _dot(ckv, wukv_ref[...])
    kr = rope_m(proj(PX_KR, LANE))
    for hd in range(MLA_HEADS):
        sl = slice(hd * LANE, (hd + 1) * LANE)
        kvh = kv[:, sl]
        k_ref[0, :, sl] = jnp.where(lo, kvh, kr).astype(BF16)
        v_ref[0, :, sl] = jnp.where(lo, one0, kvh).astype(BF16)

    dq = proj(PX_DQ, DIFF_WIDTH)
    dk = proj(PX_DK, DIFF_WIDTH)
    for hd in range(DIFF_HEADS):
        sl = slice(hd * LANE, (hd + 1) * LANE)
        qb = rope_d(dq[:, sl]) * (DIFF_SCALE * LOG2E)
        dq_ref[0, :, (2 * hd) * LANE:(2 * hd + 1) * LANE] = jnp.where(lo, qb, 0.0).astype(BF16)
        dq_ref[0, :, (2 * hd + 1) * LANE:(2 * hd + 2) * LANE] = jnp.where(lo, 0.0, qb).astype(BF16)
        dk_ref[0, :, sl] = rope_d(dk[:, sl]).astype(BF16)
    dv_ref[0] = proj(PX_DV, DIFF_WIDTH).astype(BF16)
    pg = proj(PX_G, 2 * D_MODEL)
    g_ref[0] = (1.0 / (1.0 + jnp.exp(-pg))).astype(BF16)


def _proj_ctx_kernel(x_ref, mod_ref, w_ref, gkv_ref, wukv_ref, k_ref, v_ref, dk_ref, dv_ref):
    tm = x_ref.shape[1]
    sh = mod_ref[0, :, 0:D_MODEL]
    sc = mod_ref[0, :, D_MODEL:2 * D_MODEL]
    h = (x_ref[0] * (1.0 + sc) + sh).astype(BF16)

    def proj(a, n):
        return _dot(h, w_ref[:, a:a + n])

    lane = lax.broadcasted_iota(I32, (tm, LANE), 1)
    lo = lane < MLA_NOPE
    one0 = (lane == 0).astype(F32)
    ckv = _rms(proj(CX_KV, MLA_KV_RANK), gkv_ref[...]).astype(BF16)
    kv = _dot(ckv, wukv_ref[...])
    kr = proj(CX_KR, LANE)
    for hd in range(MLA_HEADS):
        sl = slice(hd * LANE, (hd + 1) * LANE)
        kvh = kv[:, sl]
        k_ref[0, :, sl] = jnp.where(lo, kvh, kr).astype(BF16)
        v_ref[0, :, sl] = jnp.where(lo, one0, kvh).astype(BF16)
    dk_ref[0] = proj(CX_DK, DIFF_WIDTH).astype(BF16)
    dv_ref[0] = proj(CX_DV, DIFF_WIDTH).astype(BF16)


def _full(shape):
    nd = len(shape)
    return pl.BlockSpec(shape, lambda *_: (0,) * nd)


def _proj_lat(x, mod3, wx, gq, wuq, gkv, wukv, tabs, tm):
    b, s, _ = x.shape
    row = lambda w: pl.BlockSpec((1, tm, w), lambda bi, ti: (bi, ti, 0))
    tab = pl.BlockSpec((tm, LANE), lambda bi, ti: (ti, 0))
    outs = [(MLA_W, BF16), (MLA_W, BF16), (MLA_W, BF16), (2 * DIFF_WIDTH, BF16), (DIFF_WIDTH, BF16),
            (DIFF_WIDTH, BF16), (2 * D_MODEL, BF16)]
    return pl.pallas_call(
        _proj_lat_kernel,
        grid=(b, s // tm),
        in_specs=[row(D_MODEL),
                  pl.BlockSpec((1, 1, 6 * D_MODEL), lambda bi, ti: (bi, 0, 0)),
                  _full(wx.shape), _full(gq.shape), _full(wuq.shape),
                  _full(gkv.shape), _full(wukv.shape)] + [tab] * len(tabs),
        out_specs=[row(w) for w, _ in outs],
        out_shape=[jax.ShapeDtypeStruct((b, s, w), dt) for w, dt in outs],
        compiler_params=_cparams(("arbitrary", "arbitrary")),
        name="proj_lat",
    )(x, mod3, wx, gq, wuq, gkv, wukv, *tabs)


def _proj_ctx(ctx, mod3, wc, gkv, wukv):
    b, s, _ = ctx.shape
    row = lambda w: pl.BlockSpec((1, s, w), lambda bi: (bi, 0, 0))
    outs = [MLA_W, MLA_W, DIFF_WIDTH, DIFF_WIDTH]
    return pl.pallas_call(
        _proj_ctx_kernel,
        grid=(b,),
        in_specs=[row(D_MODEL),
                  pl.BlockSpec((1, 1, 6 * D_MODEL), lambda bi: (b, 0, 0)),
                  _full(wc.shape), _full(gkv.shape), _full(wukv.shape)],
        out_specs=[row(w) for w in outs],
        out_shape=[jax.ShapeDtypeStruct((b, s, w), BF16) for w in outs],
        compiler_params=_cparams(("arbitrary",)),
        name="proj_ctx",
    )(ctx, mod3, wc, gkv, wukv)


ONES_ROWS = 16


def _flash_t(q_t, kc_ref, vc_ref, kl_ref, vl_ref, p_ref, g_ref, m_ref, x_ref, acc_ref, tk, ones_rows):
    n = kl_ref.shape[1] // tk

    def chunk(ref, i):
        return ref[0, pl.ds(pl.multiple_of(i * tk, tk), tk), :]

    def pv(v, p):
        if ones_rows:
            return _dot(jnp.concatenate([v.T, jnp.ones((ONES_ROWS, v.shape[0]), BF16)], axis=0), p)
        return _dot_tn(v, p)

    def scores_exp(i):
        stab = m_ref[...]
        s_t = _dot(chunk(kl_ref, i), q_t)
        p_ref[i % 2] = jnp.exp2(s_t - stab).astype(BF16)
        cmax = jnp.max(s_t, axis=0, keepdims=True)
        m_new = jnp.maximum(stab, cmax)
        g_ref[(i + 1) % 2] = jnp.exp2(stab - m_new)
        m_ref[...] = m_new
        x_ref[...] = jnp.maximum(x_ref[...], cmax - stab)

    def values(i):
        acc_ref[...] = g_ref[i % 2] * acc_ref[...] + pv(chunk(vl_ref, i), p_ref[i % 2])

    def exact_chunk(k, v, m, first):
        out = pv(v, jnp.exp2(_dot(k, q_t) - m).astype(BF16))
        acc_ref[...] = out if first else acc_ref[...] + out

    m0 = jnp.max(_dot(kc_ref[0], q_t), axis=0, keepdims=True)
    m_ref[...] = m0
    g_ref[0] = jnp.ones_like(m0)
    x_ref[...] = jnp.zeros_like(m0)
    exact_chunk(kc_ref[0], vc_ref[0], m0, True)

    scores_exp(0)

    def body(i, carry):
        values(i - 1)
        scores_exp(i)
        return carry

    lax.fori_loop(1, n, body, 0, unroll=True)
    values(n - 1)

    @pl.when(jnp.max(x_ref[...]) > LAGGED_MAX_HEADROOM)
    def _():
        m = m_ref[...]
        exact_chunk(kc_ref[0], vc_ref[0], m, True)

        def redo(i, carry):
            exact_chunk(chunk(kl_ref, i), chunk(vl_ref, i), m, False)
            return carry

        lax.fori_loop(0, n, redo, 0)


def _attn_scratch(tk, nq, ones_rows):
    v = pltpu.VMEM
    rows = LANE + (ONES_ROWS if ones_rows else 0)
    return [v((2, tk, nq), BF16), v((2, 1, nq), F32), v((1, nq), F32), v((1, nq), F32), v((rows, nq), F32)]


def _mla_attn_kernel(q_ref, kc_ref, vc_ref, kl_ref, vl_ref, o_ref, p_ref, g_ref, m_ref, x_ref, acc_ref, *, tk):
    _flash_t(q_ref[0].T, kc_ref, vc_ref, kl_ref, vl_ref, p_ref, g_ref, m_ref, x_ref, acc_ref, tk, False)
    acc = acc_ref[...]
    o_ref[0] = (acc / acc[0:1, :]).T.astype(BF16)


def _diff_attn_kernel(lam_ref, gs_ref, q1_ref, q2_ref, kc_ref, vc_ref, kl_ref, vl_ref, o_ref,
                      p_ref, g_ref, m_ref, x_ref, acc_ref, *, tk):
    tq = q1_ref.shape[1]
    q_t = jnp.concatenate([q1_ref[0].T, q2_ref[0].T], axis=1)
    _flash_t(q_t, kc_ref, vc_ref, kl_ref, vl_ref, p_ref, g_ref, m_ref, x_ref, acc_ref, tk, True)
    o = acc_ref[0:LANE, :] / acc_ref[LANE:LANE + 1, :]
    dl = lam_ref[...]
    lam = (jnp.exp(jnp.sum(dl[0:1] * dl[1:2], axis=-1, keepdims=True))
           - jnp.exp(jnp.sum(dl[2:3] * dl[3:4], axis=-1, keepdims=True)) + LAMBDA_INIT)
    w = o[:, :tq] - lam * o[:, tq:]
    w = w * lax.rsqrt(jnp.mean(w * w, axis=0, keepdims=True) + EPS) * gs_ref[...] * (1.0 - LAMBDA_INIT)
    o_ref[0] = w.T.astype(BF16)


def _mla_attn(q, kc, vc, kl, vl, tq, tk):
    b, s, _ = q.shape
    sc = kc.shape[1]
    assert s % tk == 0 and s % tq == 0
    qs = pl.BlockSpec((1, tq, LANE), lambda bi, hi, qi: (bi, qi, hi))
    cs = pl.BlockSpec((1, sc, LANE), lambda bi, hi, qi: (bi, 0, hi))
    ls = pl.BlockSpec((1, s, LANE), lambda bi, hi, qi: (bi, 0, hi))
    return pl.pallas_call(
        functools.partial(_mla_attn_kernel, tk=tk),
        grid=(b, MLA_HEADS, s // tq),
        in_specs=[qs, cs, cs, ls, ls],
        out_specs=qs,
        out_shape=jax.ShapeDtypeStruct((b, s, MLA_W), BF16),
        scratch_shapes=_attn_scratch(tk, tq, False),
        compiler_params=_cparams(("arbitrary", "arbitrary", "arbitrary")),
        name="mla_attn",
    )(q, kc, vc, kl, vl)


def _diff_attn(lam4, g_subln, dq, kc, vc, kl, vl, tq, tk):
    b, s, _ = dq.shape
    sc = kc.shape[1]
    q1 = pl.BlockSpec((1, tq, LANE), lambda bi, hi, qi: (bi, qi, 2 * hi))
    q2 = pl.BlockSpec((1, tq, LANE), lambda bi, hi, qi: (bi, qi, 2 * hi + 1))
    cs = pl.BlockSpec((1, sc, LANE), lambda bi, hi, qi: (bi, 0, hi))
    ls = pl.BlockSpec((1, s, LANE), lambda bi, hi, qi: (bi, 0, hi))
    return pl.pallas_call(
        functools.partial(_diff_attn_kernel, tk=tk),
        grid=(b, DIFF_HEADS, s // tq),
        in_specs=[_full(lam4.shape), _full(g_subln.shape), q1, q2, cs, cs, ls, ls],
        out_specs=pl.BlockSpec((1, tq, LANE), lambda bi, hi, qi: (bi, qi, hi)),
        out_shape=jax.ShapeDtypeStruct((b, s, DIFF_WIDTH), BF16),
        scratch_shapes=_attn_scratch(tk, 2 * tq, True),
        compiler_params=_cparams(("arbitrary", "arbitrary", "arbitrary")),
        name="diff_attn",
    )(lam4, g_subln, dq, dq, kc, vc, kl, vl)


def _mix_kernel(om_ref, od_ref, g_ref, x_ref, mod_ref, wom_ref, wod_ref, wout_ref, lng_ref, lnb_ref,
                wrh_ref, wrl_ref, x1_ref, ha_ref, hb_ref, lg_ref):
    gm = g_ref[0, :, 0:D_MODEL]
    gd = g_ref[0, :, D_MODEL:2 * D_MODEL]
    y = gm * _dot(om_ref[0], wom_ref[...]) + gd * _dot(od_ref[0], wod_ref[...])
    z = _dot(y.astype(BF16), wout_ref[...])
    g1 = mod_ref[0, :, 2 * D_MODEL:3 * D_MODEL]
    sh2 = mod_ref[0, :, 3 * D_MODEL:4 * D_MODEL]
    sc2 = mod_ref[0, :, 4 * D_MODEL:5 * D_MODEL]
    x1 = _layer_norm(DEEPNORM_ALPHA * x_ref[0] + g1 * z, lng_ref[...], lnb_ref[...])
    x1_ref[0] = x1
    h2 = x1 * (1.0 + sc2) + sh2
    hi = h2.astype(BF16)
    lo = (h2 - hi.astype(F32)).astype(BF16)
    bits = pltpu.bitcast(hi.astype(F32), U32)
    q4 = D_MODEL // 4
    ha_ref[0] = bits[:, 0:q4] | (bits[:, q4:2 * q4] >> 16)
    hb_ref[0] = bits[:, 2 * q4:3 * q4] | (bits[:, 3 * q4:4 * q4] >> 16)
    lg_ref[0] = _dot_nt(wrh_ref[...], hi) + (_dot_nt(wrh_ref[...], lo) + _dot_nt(wrl_ref[...], hi))


def _mix(om, od, gates, x, mod3, wom, wod, wout, lng, lnb, wrh, wrl, tm):
    b, s, _ = x.shape
    row = lambda w: pl.BlockSpec((1, tm, w), lambda bi, ti: (bi, ti, 0))
    return pl.pallas_call(
        _mix_kernel,
        grid=(b, s // tm),
        in_specs=[row(MLA_W), row(DIFF_WIDTH), row(2 * D_MODEL), row(D_MODEL),
                  pl.BlockSpec((1, 1, 6 * D_MODEL), lambda bi, ti: (bi, 0, 0)),
                  _full(wom.shape), _full(wod.shape), _full(wout.shape), _full(lng.shape), _full(lnb.shape),
                  _full(wrh.shape), _full(wrl.shape)],
        out_specs=[row(D_MODEL), row(D_MODEL // 4), row(D_MODEL // 4),
                   pl.BlockSpec((1, N_EXPERTS, tm), lambda bi, ti: (bi, 0, ti))],
        out_shape=[jax.ShapeDtypeStruct((b, s, D_MODEL), F32), jax.ShapeDtypeStruct((b, s, D_MODEL // 4), U32),
                   jax.ShapeDtypeStruct((b, s, D_MODEL // 4), U32), jax.ShapeDtypeStruct((b, N_EXPERTS, s), F32)],
        compiler_params=_cparams(("arbitrary", "arbitrary")),
        name="mix",
    )(om, od, gates, x, mod3, wom, wod, wout, lng, lnb, wrh, wrl)


def _route_kernel(lg_ref, br_ref, pos_ref, gate_ref, st_ref, *, cap, tile):
    e, s = lg_ref.shape[1], lg_ref.shape[2]
    lg = lg_ref[0] + br_ref[...]
    ex = jnp.exp(lg - jnp.max(lg, axis=0, keepdims=True))
    aff = ex / jnp.sum(ex, axis=0, keepdims=True)
    gate_ref[0] = aff
    bits = pltpu.bitcast(aff, I32)

    def count(mask):
        return jnp.sum(mask.astype(I32), axis=1, keepdims=True)

    def search(i, thr):
        cand = thr | (jnp.int32(1) << (30 - i))
        return jnp.where(count(bits >= cand) >= cap, cand, thr)

    thr = lax.fori_loop(0, 31, search, jnp.zeros((e, 1), I32))
    gt = bits > thr
    eq = bits == thr
    need = cap - count(gt)

    r = lax.broadcasted_iota(I32, (LANE, LANE), 0)
    c = lax.broadcasted_iota(I32, (LANE, LANE), 1)
    tri = (r < c).astype(BF16)
    lane = lax.broadcasted_iota(I32, (e, LANE), 1)

    def prefix(mask_fn, emit):
        carry = jnp.zeros((e, 1), F32)
        for j in range(s // LANE):
            m = mask_fn(j)
            emit(j, carry + _dot(m.astype(BF16), tri), m, carry)
            carry = carry + jnp.sum(m.astype(F32), axis=1, keepdims=True)

    def eq_blk(j):
        return eq[:, j * LANE:(j + 1) * LANE]

    def emit_sel(j, rank, m, carry):
        sel = gt[:, j * LANE:(j + 1) * LANE] | (m & (rank < need.astype(F32)))
        pos_ref[0, :, j * LANE:(j + 1) * LANE] = sel.astype(I32)

    prefix(eq_blk, emit_sel)

    starts = [jnp.zeros((e, LANE), I32)]

    def sel_blk(j):
        return pos_ref[0, :, j * LANE:(j + 1) * LANE] > 0

    def emit_pos(j, rank, m, carry):
        if (j * LANE) % tile == 0:
            starts[0] = jnp.where(lane == (j * LANE) // tile, carry.astype(I32), starts[0])
        pos_ref[0, :, j * LANE:(j + 1) * LANE] = jnp.where(m, rank.astype(I32), -1)

    prefix(sel_blk, emit_pos)
    st_ref[0] = jnp.where(lane == s // tile, cap, starts[0])


def _route(lgt, b_router, cap, tile):
    b, e, s = lgt.shape
    blk = pl.BlockSpec((1, e, s), lambda bi: (bi, 0, 0))
    return pl.pallas_call(
        functools.partial(_route_kernel, cap=cap, tile=tile),
        grid=(b,),
        in_specs=[blk, _full((e, 1))],
        out_specs=[blk, blk, pl.BlockSpec((1, e, LANE), lambda bi: (bi, 0, 0))],
        out_shape=[jax.ShapeDtypeStruct((b, e, s), I32), jax.ShapeDtypeStruct((b, e, s), F32),
                   jax.ShapeDtypeStruct((b, e, LANE), I32)],
        compiler_params=_cparams(("arbitrary",)),
        name="route",
    )(lgt, b_router.reshape(e, 1))


SLOT_COLS = 8


def _slots_kernel(st_ref, pos_ref, gate_ref, o_ref, acc_ref, *, chunk, tile, e0):
    bi, ei = pl.program_id(0), pl.program_id(1) + e0
    cap, s = acc_ref.shape[0], pos_ref.shape[3]
    acc_ref[...] = jnp.zeros_like(acc_ref)
    slot = lax.broadcasted_iota(I32, (chunk, tile), 0)
    row = lax.broadcasted_iota(I32, (SLOT_COLS, tile), 0)
    for u in range(s // tile):
        b0, b1 = st_ref[bi, ei, u], st_ref[bi, ei, u + 1]
        cols = slice(u * tile, (u + 1) * tile)

        @pl.when(b1 > b0)
        def _():
            t = lax.broadcasted_iota(I32, (SLOT_COLS, tile), 1) + u * tile
            g = jnp.broadcast_to(gate_ref[0, 0, :, cols], (SLOT_COLS, tile))
            g1 = g.astype(BF16).astype(F32)
            g2 = (g - g1).astype(BF16).astype(F32)
            g3 = g - g1 - g2
            rec = jnp.where(row == 0, (t >> 7).astype(F32),
                            jnp.where(row == 1, (t & 127).astype(F32),
                                      jnp.where(row == 2, g1, jnp.where(row == 3, g2,
                                                                        jnp.where(row == 4, g3, 0.0)))))
            rec = rec.astype(BF16)
            pos = pos_ref[0, 0, :, cols]
            for c in range(cap // chunk):
                lo, hi = c * chunk, (c + 1) * chunk

                @pl.when((b0 < hi) & (b1 > lo))
                def _():
                    hit = (slot == (pos - lo)).astype(BF16)
                    acc_ref[lo:hi, :] += _dot_nt(hit, rec)

    o_ref[0, 0] = acc_ref[...]


def _slots(starts, pos4, gate4, cap, tile, chunk, e0, ne):
    b, _, _, s = pos4.shape
    blk = pl.BlockSpec((1, 1, 1, s), lambda bi, ei, st: (bi, ei + e0, 0, 0))
    gs = pltpu.PrefetchScalarGridSpec(
        num_scalar_prefetch=1,
        grid=(b, ne),
        in_specs=[blk, blk],
        out_specs=pl.BlockSpec((1, 1, cap, SLOT_COLS), lambda bi, ei, st: (bi, ei, 0, 0)),
        scratch_shapes=[pltpu.VMEM((cap, SLOT_COLS), F32)],
    )
    return pl.pallas_call(
        functools.partial(_slots_kernel, chunk=chunk, tile=tile, e0=e0),
        grid_spec=gs,
        out_shape=jax.ShapeDtypeStruct((b, ne, cap, SLOT_COLS), F32),
        compiler_params=_cparams(("arbitrary", "arbitrary")),
        name="slots",
    )(starts, pos4, gate4)


SC_GATHER_WINDOW = 128
MOE_GROUPS = 2


def _sc_gather(table, idx):
    n = idx.shape[0]
    w = table.shape[1]
    mesh = plsc.VectorSubcoreMesh(core_axis_name="core", subcore_axis_name="subcore")

    @pl.kernel(out_type=jax.ShapeDtypeStruct((n, w), table.dtype), mesh=mesh)
    def gather(t_hbm, i_hbm, o_hbm):
        def body(i_vmem, o_vmem):
            pltpu.sync_copy(t_hbm.at[i_vmem.at[0]], o_vmem)

        pltpu.emit_pipeline(
            body,
            grid=(n // SC_GATHER_WINDOW,),
            in_specs=[pl.BlockSpec((1, SC_GATHER_WINDOW), index_map=lambda i: (0, i))],
            out_specs=[pl.BlockSpec((SC_GATHER_WINDOW, w), index_map=lambda i: (i, 0))],
            core_axis_name=("core", "subcore"),
            dimension_semantics=(pltpu.PARALLEL,),
        )(i_hbm, o_hbm)

    return gather(table, idx.reshape(1, n))


def _ffn_kernel(xa_ref, xb_ref, sl_ref, w1_ref, w3_ref, w2_ref, *rest):
    ye_ref = rest[-1]

    def halves(words):
        return pltpu.bitcast(words & jnp.uint32(0xFFFF0000), F32), pltpu.bitcast(words << 16, F32)

    a_hi, a_lo = halves(xa_ref[...])
    b_hi, b_lo = halves(xb_ref[...])
    xe = jnp.concatenate([a_hi, a_lo, b_hi, b_lo], axis=1).astype(BF16)
    sl = sl_ref[0, 0]
    gate = sl[:, 2:3] + sl[:, 3:4] + sl[:, 4:5]
    a = _dot(xe, w1_ref[0])
    hid = ((a / (1.0 + jnp.exp(-a))) * _dot(xe, w3_ref[0])).astype(BF16)
    ye_ref[0, 0] = (_dot(hid, w2_ref[0]) * gate).astype(BF16)


def _ffn(ye_prev, xa, xb, slots, w1, w3, w2, e0):
    b, ne, cap, _ = slots.shape
    e, d = w1.shape[0], w2.shape[2]
    xspec = pl.BlockSpec((cap, xa.shape[1]), lambda bi, ei: (bi * ne + ei, 0))
    wspec = lambda shp: pl.BlockSpec((1,) + shp, lambda bi, ei: (ei + e0, 0, 0))
    in_specs = [xspec, xspec, pl.BlockSpec((1, 1, cap, SLOT_COLS), lambda bi, ei: (bi, ei, 0, 0)),
                wspec(w1.shape[1:]), wspec(w3.shape[1:]), wspec(w2.shape[1:])]
    args = [xa, xb, slots, w1, w3, w2]
    aliases = {}
    if ye_prev is not None:
        in_specs.append(pl.BlockSpec(memory_space=pl.ANY))
        args.append(ye_prev)
        aliases = {len(args) - 1: 0}
    return pl.pallas_call(
        _ffn_kernel,
        grid=(b, ne),
        in_specs=in_specs,
        out_specs=pl.BlockSpec((1, 1, cap, d), lambda bi, ei: (bi, ei + e0, 0, 0)),
        out_shape=jax.ShapeDtypeStruct((b, e, cap, d), BF16),
        input_output_aliases=aliases,
        compiler_params=_cparams(("arbitrary", "arbitrary")),
        name="ffn",
    )(*args)


COMBINE_EXPERTS = 4


def _combine_kernel(st_ref, ye_ref, pos_ref, x1_ref, mod_ref, lng_ref, lnb_ref, o_ref, acc_ref, *, chunk):
    bi, ti, ei = pl.program_id(0), pl.program_id(1), pl.program_id(2)
    group, cap, tile = ye_ref.shape[1], ye_ref.shape[2], x1_ref.shape[1]

    @pl.when(ei == 0)
    def _():
        acc_ref[...] = jnp.zeros_like(acc_ref)

    slot = lax.broadcasted_iota(I32, (chunk, tile), 0)
    for g in range(group):
        expert = ei * group + g
        b0, b1 = st_ref[bi, expert, ti], st_ref[bi, expert, ti + 1]
        for c in range(cap // chunk):
            lo, hi = c * chunk, (c + 1) * chunk

            @pl.when((b0 < hi) & (b1 > lo))
            def _():
                hit = (slot == (pos_ref[0, g] - lo)).astype(BF16)
                acc_ref[...] += _dot_tn(hit, ye_ref[0, g, lo:hi, :])

    @pl.when(ei == pl.num_programs(2) - 1)
    def _():
        g2 = mod_ref[0, :, 5 * D_MODEL:6 * D_MODEL]
        o_ref[0] = _layer_norm(DEEPNORM_ALPHA * x1_ref[0] + g2 * acc_ref[...], lng_ref[...], lnb_ref[...])


def _combine(starts, ye, pos4, x1, mod3, lng, lnb, tile, chunk):
    b, s, d = x1.shape
    e, cap = ye.shape[1], ye.shape[2]
    g = COMBINE_EXPERTS
    gs = pltpu.PrefetchScalarGridSpec(
        num_scalar_prefetch=1,
        grid=(b, s // tile, e // g),
        in_specs=[pl.BlockSpec((1, g, cap, d), lambda bi, ti, ei, st: (bi, ei, 0, 0)),
                  pl.BlockSpec((1, g, 1, tile), lambda bi, ti, ei, st: (bi, ei, 0, ti)),
                  pl.BlockSpec((1, tile, d), lambda bi, ti, ei, st: (bi, ti, 0)),
                  pl.BlockSpec((1, 1, 6 * D_MODEL), lambda bi, ti, ei, st: (bi, 0, 0)),
                  pl.BlockSpec((1, d), lambda bi, ti, ei, st: (0, 0)),
                  pl.BlockSpec((1, d), lambda bi, ti, ei, st: (0, 0))],
        out_specs=pl.BlockSpec((1, tile, d), lambda bi, ti, ei, st: (bi, ti, 0)),
        scratch_shapes=[pltpu.VMEM((tile, d), F32)],
    )
    return pl.pallas_call(
        functools.partial(_combine_kernel, chunk=chunk),
        grid_spec=gs,
        out_shape=jax.ShapeDtypeStruct((b, s, d), F32),
        compiler_params=_cparams(("arbitrary", "arbitrary", "arbitrary")),
        name="combine",
    )(starts, ye, pos4, x1, mod3, lng, lnb)


def _head_pad(w, n_heads, width, offset=0):
    k = w.shape[0]
    w3 = w.reshape(k, n_heads, width)
    out = jnp.zeros((k, n_heads, LANE), w.dtype).at[:, :, offset:offset + width].set(w3)
    return out.reshape(k, n_heads * LANE)


def _prep_weights(w_in, w_uq, w_o_mla):
    kr = w_in[:, MLA_KR_OFF:DIFF_Q_OFF]
    dqw = w_in[:, DIFF_Q_OFF:DIFF_K_OFF]
    dkw = w_in[:, DIFF_K_OFF:DIFF_V_OFF]
    dvw = w_in[:, DIFF_V_OFF:GATE_OFF]
    kr_pad = _head_pad(kr, 1, MLA_ROPE, MLA_NOPE)
    wx = jnp.concatenate([w_in[:, MLA_Q_OFF:MLA_KR_OFF], kr_pad, dqw, dkw, dvw, w_in[:, GATE_OFF:]],
                         axis=1).astype(BF16)
    wc = jnp.concatenate([w_in[:, MLA_KV_OFF:MLA_KR_OFF], kr_pad, dkw, dvw], axis=1).astype(BF16)
    wuq = _head_pad(w_uq, MLA_HEADS, MLA_NOPE + MLA_ROPE).astype(BF16)
    wom = jnp.zeros((MLA_HEADS, LANE, D_MODEL), w_o_mla.dtype).at[:, MLA_NOPE:, :].set(
        w_o_mla.reshape(MLA_HEADS, MLA_V, D_MODEL)).reshape(MLA_W, D_MODEL).astype(BF16)
    return wx, wc, wuq, wom


def kernel(x, c, ctx, c_ctx, w_ada, b_ada, w_in, mla_g_q, mla_w_uq, mla_g_kv, mla_w_ukv, mla_w_o, diff_lambda,
           diff_g_subln, diff_w_o, w_out, ln1_g, ln1_b, moe_w_router, moe_b_router, moe_w1, moe_w3, moe_w2,
           ln2_g, ln2_b):
    b, s, d = x.shape
    assert d == D_MODEL and w_ada.shape[0] == DEPTH == 1 and b < 8
    cap = EC_CAPACITY * s // N_EXPERTS
    tile = min(1024, s)
    gather_chunk = min(128, cap)
    scatter_chunk = min(256, cap)
    tm = min(512, s)
    tq = min(2048, s)
    tk = min(2048, s)

    cc = jnp.zeros((8, d), F32).at[:b].set(c).at[b].set(c_ctx)
    mod3 = _ada(cc, w_ada[0], b_ada[0]).reshape(8, 1, 6 * d)

    wx, wc, wuq, wom = _prep_weights(w_in[0], mla_w_uq[0], mla_w_o[0])
    gq = mla_g_q[0].reshape(1, -1)
    gkv = mla_g_kv[0].reshape(1, -1)
    wukv = mla_w_ukv[0].astype(BF16)
    tabs = _rope_tables(s)

    q, k, v, dq, dk, dv, gates = _proj_lat(x, mod3, wx, gq, wuq, gkv, wukv, tabs, tm)
    kc, vc, dkc, dvc = _proj_ctx(ctx, mod3, wc, gkv, wukv)

    o_mla = _mla_attn(q, kc, vc, k, v, tq, tk)
    o_diff = _diff_attn(diff_lambda[0], diff_g_subln[0].reshape(-1, 1), dq, dkc, dvc, dk, dv, tq // 2, tk)

    wr = moe_w_router[0].T
    wrh = wr.astype(BF16)
    wrl = (wr - wrh.astype(F32)).astype(BF16)
    x1, ha, hb, lgt = _mix(o_mla, o_diff, gates, x, mod3, wom, diff_w_o[0].astype(BF16), w_out[0].astype(BF16),
                           ln1_g[0].reshape(1, -1), ln1_b[0].reshape(1, -1), wrh, wrl, tm)

    pos, gate, starts = _route(lgt, moe_b_router[0], cap, tile)
    pos4 = pos.reshape(b, N_EXPERTS, 1, s)
    gate4 = gate.reshape(b, N_EXPERTS, 1, s)
    row0 = (jnp.arange(b, dtype=I32) * s)[:, None, None]
    ha2, hb2 = ha.reshape(b * s, -1), hb.reshape(b * s, -1)
    w1b, w3b, w2b = moe_w1[0].astype(BF16), moe_w3[0].astype(BF16), moe_w2[0].astype(BF16)
    ne = N_EXPERTS // MOE_GROUPS
    groups = []
    for gi in range(MOE_GROUPS):
        slots = _slots(starts, pos4, gate4, cap, tile, gather_chunk, gi * ne, ne)
        idx = (slots[..., 0].astype(I32) * 128 + slots[..., 1].astype(I32) + row0).reshape(-1)
        groups.append((slots, _sc_gather(ha2, idx), _sc_gather(hb2, idx)))
    ye = None
    for gi, (slots, xa, xb) in enumerate(groups):
        ye = _ffn(ye, xa, xb, slots, w1b, w3b, w2b, gi * ne)
    return _combine(starts, ye, pos4, x1, mod3, ln2_g[0].reshape(1, -1), ln2_b[0].reshape(1, -1), tile, scatter_chunk)
```

```python
import functools
import math

import numpy as np
import jax
import jax.numpy as jnp
from jax import lax
from jax.experimental import pallas as pl
from jax.experimental.pallas import tpu as pltpu
from jax.experimental.pallas import tpu_sc as plsc

F32 = jnp.float32
BF16 = jnp.bfloat16
I32 = jnp.int32
U32 = jnp.uint32

D_MODEL = 1024
DEPTH = 1
GRID_W = 64
ROPE_THETA = 10000.0
EPS = 1e-6
MLA_HEADS = 8
MLA_Q_RANK = 256
MLA_KV_RANK = 128
MLA_NOPE = 64
MLA_ROPE = 32
MLA_V = 64
DIFF_HEADS = 4
DIFF_HD = 64
N_EXPERTS = 16
EXPERT_FF = 1024
EC_CAPACITY = 2

MLA_Q_OFF = 0
MLA_KV_OFF = MLA_Q_OFF + MLA_Q_RANK
MLA_KR_OFF = MLA_KV_OFF + MLA_KV_RANK
DIFF_Q_OFF = MLA_KR_OFF + MLA_ROPE
DIFF_K_OFF = DIFF_Q_OFF + DIFF_HEADS * 2 * DIFF_HD
DIFF_V_OFF = DIFF_K_OFF + DIFF_HEADS * 2 * DIFF_HD
GATE_OFF = DIFF_V_OFF + DIFF_HEADS * 2 * DIFF_HD
N_IN = GATE_OFF + 2 * D_MODEL
DIFF_WIDTH = DIFF_HEADS * 2 * DIFF_HD
MLA_SCALE = (MLA_NOPE + MLA_ROPE) ** -0.5
DIFF_SCALE = DIFF_HD ** -0.5
DEEPNORM_ALPHA = (2 * DEPTH) ** 0.25
LAMBDA_INIT = 0.8 - 0.6 * math.exp(-0.3 * 0)
LOG2E = 1.4426950408889634
LAGGED_MAX_HEADROOM = 100.0

LANE = 128
MLA_W = MLA_HEADS * LANE
PX_Q = 0
PX_KV = PX_Q + MLA_Q_RANK
PX_KR = PX_KV + MLA_KV_RANK
PX_DQ = PX_KR + LANE
PX_DK = PX_DQ + DIFF_WIDTH
PX_DV = PX_DK + DIFF_WIDTH
PX_G = PX_DV + DIFF_WIDTH
PX_N = PX_G + 2 * D_MODEL
CX_KV = 0
CX_KR = CX_KV + MLA_KV_RANK
CX_DK = CX_KR + LANE
CX_DV = CX_DK + DIFF_WIDTH
CX_N = CX_DV + DIFF_WIDTH

VMEM_LIMIT = 56 * 1024 * 1024


def _cparams(sem):
    return pltpu.CompilerParams(dimension_semantics=sem, vmem_limit_bytes=VMEM_LIMIT)


def _rope(x, cos, sin_lo, sin_hi, quarter):
    return x * cos + pltpu.roll(x, LANE - quarter, 1) * sin_lo + pltpu.roll(x, quarter, 1) * sin_hi


def _rope_tables(seq):
    t = np.arange(seq)
    row = (t // GRID_W).astype(np.float64)
    col = (t % GRID_W).astype(np.float64)

    def axial(d):
        h = d // 2
        inv = ROPE_THETA ** (-(np.arange(h // 2, dtype=np.float64) * 2.0 / h))
        ar, ac = row[:, None] * inv, col[:, None] * inv
        cos = np.concatenate([np.cos(ar), np.cos(ar), np.cos(ac), np.cos(ac)], -1)
        sin = np.concatenate([np.sin(ar), np.sin(ar), np.sin(ac), np.sin(ac)], -1)
        return cos, sin

    def split(sin, d):
        first = (np.arange(sin.shape[1]) % (d // 2)) < d // 4
        return np.where(first, -sin, 0.0), np.where(first, 0.0, sin)

    c32, s32 = axial(MLA_ROPE)
    c64, s64 = axial(DIFF_HD)
    cos_m = np.ones((seq, LANE))
    sin_m = np.zeros((seq, LANE))
    cos_m[:, MLA_NOPE:MLA_NOPE + MLA_ROPE] = c32
    sin_m[:, MLA_NOPE:MLA_NOPE + MLA_ROPE] = s32
    cos_d = np.concatenate([c64, c64], -1)
    sin_d = np.concatenate([s64, s64], -1)
    tabs = (cos_m, *split(sin_m, MLA_ROPE), cos_d, *split(sin_d, DIFF_HD))
    return tuple(jnp.asarray(a, F32) for a in tabs)


def _rms(x, g):
    return x * lax.rsqrt(jnp.mean(x * x, axis=-1, keepdims=True) + EPS) * g


def _layer_norm(x, g, b):
    mu = jnp.mean(x, axis=-1, keepdims=True)
    xc = x - mu
    var = jnp.mean(xc * xc, axis=-1, keepdims=True)
    return xc * lax.rsqrt(var + EPS) * g + b


def _dot(a, b):
    return jnp.dot(a, b, preferred_element_type=F32)


def _dot_nt(a, b):
    return lax.dot_general(a, b, (((1,), (1,)), ((), ())), preferred_element_type=F32)


def _dot_tn(a, b):
    return lax.dot_general(a, b, (((0,), (0,)), ((), ())), preferred_element_type=F32)


def _ada_kernel(c_ref, w_ref, b_ref, o_ref):
    c = c_ref[...]
    s = (c / (1.0 + jnp.exp(-c))).astype(BF16)
    o_ref[...] = _dot(s, w_ref[...].astype(BF16)) + b_ref[...]


def _ada(cc, w_ada, b_ada):
    n = w_ada.shape[1]
    tn = 1024
    return pl.pallas_call(
        _ada_kernel,
        grid=(n // tn,),
        in_specs=[pl.BlockSpec((8, D_MODEL), lambda j: (0, 0)),
                  pl.BlockSpec((D_MODEL, tn), lambda j: (0, j)),
                  pl.BlockSpec((1, tn), lambda j: (0, j))],
        out_specs=pl.BlockSpec((8, tn), lambda j: (0, j)),
        out_shape=jax.ShapeDtypeStruct((8, n), F32),
        compiler_params=_cparams(("arbitrary",)),
        name="ada",
    )(cc, w_ada, b_ada.reshape(1, n))


def _proj_lat_kernel(x_ref, mod_ref, w_ref, gq_ref, wuq_ref, gkv_ref, wukv_ref,
                     cm_ref, slm_ref, shm_ref, cd_ref, sld_ref, shd_ref,
                     q_ref, k_ref, v_ref, dq_ref, dk_ref, dv_ref, g_ref):
    tm = x_ref.shape[1]
    sh = mod_ref[0, :, 0:D_MODEL]
    sc = mod_ref[0, :, D_MODEL:2 * D_MODEL]
    h = (x_ref[0] * (1.0 + sc) + sh).astype(BF16)

    def proj(a, n):
        return _dot(h, w_ref[:, a:a + n])

    lane = lax.broadcasted_iota(I32, (tm, LANE), 1)
    lo = lane < MLA_NOPE
    one0 = (lane == 0).astype(F32)

    def rope_m(t):
        return _rope(t, cm_ref[...], slm_ref[...], shm_ref[...], MLA_ROPE // 4)

    def rope_d(t):
        return _rope(t, cd_ref[...], sld_ref[...], shd_ref[...], DIFF_HD // 4)

    cq = _rms(proj(PX_Q, MLA_Q_RANK), gq_ref[...]).astype(BF16)
    q = _dot(cq, wuq_ref[...])
    for hd in range(MLA_HEADS):
        sl = slice(hd * LANE, (hd + 1) * LANE)
        q_ref[0, :, sl] = (rope_m(q[:, sl]) * (MLA_SCALE * LOG2E)).astype(BF16)

    ckv = _rms(proj(PX_KV, MLA_KV_RANK), gkv_ref[...]).astype(BF16)
    kv = _dot(ckv, wukv_ref[...])
    kr = rope_m(proj(PX_KR, LANE))
    for hd in range(MLA_HEADS):
        sl = slice(hd * LANE, (hd + 1) * LANE)
        kvh = kv[:, sl]
        k_ref[0, :, sl] = jnp.where(lo, kvh, kr).astype(BF16)
        v_ref[0, :, sl] = jnp.where(lo, one0, kvh).astype(BF16)

    dq = proj(PX_DQ, DIFF_WIDTH)
    dk = proj(PX_DK, DIFF_WIDTH)
    for hd in range(DIFF_HEADS):
        sl = slice(hd * LANE, (hd + 1) * LANE)
        qb = rope_d(dq[:, sl]) * (DIFF_SCALE * LOG2E)
        dq_ref[0, :, (2 * hd) * LANE:(2 * hd + 1) * LANE] = jnp.where(lo, qb, 0.0).astype(BF16)
        dq_ref[0, :, (2 * hd + 1) * LANE:(2 * hd + 2) * LANE] = jnp.where(lo, 0.0, qb).astype(BF16)
        dk_ref[0, :, sl] = rope_d(dk[:, sl]).astype(BF16)
    dv_ref[0] = proj(PX_DV, DIFF_WIDTH).astype(BF16)
    pg = proj(PX_G, 2 * D_MODEL)
    g_ref[0] = (1.0 / (1.0 + jnp.exp(-pg))).astype(BF16)


def _proj_ctx_kernel(x_ref, mod_ref, w_ref, gkv_ref, wukv_ref, k_ref, v_ref, dk_ref, dv_ref):
    tm = x_ref.shape[1]
    sh = mod_ref[0, :, 0:D_MODEL]
    sc = mod_ref[0, :, D_MODEL:2 * D_MODEL]
    h = (x_ref[0] * (1.0 + sc) + sh).astype(BF16)

    def proj(a, n):
        return _dot(h, w_ref[:, a:a + n])

    lane = lax.broadcasted_iota(I32, (tm, LANE), 1)
    lo = lane < MLA_NOPE
    one0 = (lane == 0).astype(F32)
    ckv = _rms(proj(CX_KV, MLA_KV_RANK), gkv_ref[...]).astype(BF16)
    kv = _dot(ckv, wukv_ref[...])
    kr = proj(CX_KR, LANE)
    for hd in range(MLA_HEADS):
        sl = slice(hd * LANE, (hd + 1) * LANE)
        kvh = kv[:, sl]
        k_ref[0, :, sl] = jnp.where(lo, kvh, kr).astype(BF16)
        v_ref[0, :, sl] = jnp.where(lo, one0, kvh).astype(BF16)
    dk_ref[0] = proj(CX_DK, DIFF_WIDTH).astype(BF16)
    dv_ref[0] = proj(CX_DV, DIFF_WIDTH).astype(BF16)


def _full(shape):
    nd = len(shape)
    return pl.BlockSpec(shape, lambda *_: (0,) * nd)


def _proj_lat(x, mod3, wx, gq, wuq, gkv, wukv, tabs, tm):
    b, s, _ = x.shape
    row = lambda w: pl.BlockSpec((1, tm, w), lambda bi, ti: (bi, ti, 0))
    tab = pl.BlockSpec((tm, LANE), lambda bi, ti: (ti, 0))
    outs = [(MLA_W, BF16), (MLA_W, BF16), (MLA_W, BF16), (2 * DIFF_WIDTH, BF16), (DIFF_WIDTH, BF16),
            (DIFF_WIDTH, BF16), (2 * D_MODEL, BF16)]
    return pl.pallas_call(
        _proj_lat_kernel,
        grid=(b, s // tm),
        in_specs=[row(D_MODEL),
                  pl.BlockSpec((1, 1, 6 * D_MODEL), lambda bi, ti: (bi, 0, 0)),
                  _full(wx.shape), _full(gq.shape), _full(wuq.shape),
                  _full(gkv.shape), _full(wukv.shape)] + [tab] * len(tabs),
        out_specs=[row(w) for w, _ in outs],
        out_shape=[jax.ShapeDtypeStruct((b, s, w), dt) for w, dt in outs],
        compiler_params=_cparams(("arbitrary", "arbitrary")),
        name="proj_lat",
    )(x, mod3, wx, gq, wuq, gkv, wukv, *tabs)


def _proj_ctx(ctx, mod3, wc, gkv, wukv):
    b, s, _ = ctx.shape
    row = lambda w: pl.BlockSpec((1, s, w), lambda bi: (bi, 0, 0))
    outs = [MLA_W, MLA_W, DIFF_WIDTH, DIFF_WIDTH]
    return pl.pallas_call(
        _proj_ctx_kernel,
        grid=(b,),
        in_specs=[row(D_MODEL),
                  pl.BlockSpec((1, 1, 6 * D_MODEL), lambda bi: (b, 0, 0)),
                  _full(wc.shape), _full(gkv.shape), _full(wukv.shape)],
        out_specs=[row(w) for w in outs],
        out_shape=[jax.ShapeDtypeStruct((b, s, w), BF16) for w in outs],
        compiler_params=_cparams(("arbitrary",)),
        name="proj_ctx",
    )(ctx, mod3, wc, gkv, wukv)


ONES_ROWS = 16


def _flash_t(q_t, kc_ref, vc_ref, kl_ref, vl_ref, p_ref, g_ref, m_ref, x_ref, acc_ref, tk, ones_rows):
    n = kl_ref.shape[1] // tk

    def chunk(ref, i):
        return ref[0, pl.ds(pl.multiple_of(i * tk, tk), tk), :]

    def pv(v, p):
        if ones_rows:
            return _dot(jnp.concatenate([v.T, jnp.ones((ONES_ROWS, v.shape[0]), BF16)], axis=0), p)
        return _dot_tn(v, p)

    def scores_exp(i):
        stab = m_ref[...]
        s_t = _dot(chunk(kl_ref, i), q_t)
        p_ref[i % 2] = jnp.exp2(s_t - stab).astype(BF16)
        cmax = jnp.max(s_t, axis=0, keepdims=True)
        m_new = jnp.maximum(stab, cmax)
        g_ref[(i + 1) % 2] = jnp.exp2(stab - m_new)
        m_ref[...] = m_new
        x_ref[...] = jnp.maximum(x_ref[...], cmax - stab)

    def values(i):
        acc_ref[...] = g_ref[i % 2] * acc_ref[...] + pv(chunk(vl_ref, i), p_ref[i % 2])

    def exact_chunk(k, v, m, first):
        out = pv(v, jnp.exp2(_dot(k, q_t) - m).astype(BF16))
        acc_ref[...] = out if first else acc_ref[...] + out

    s_c = _dot(kc_ref[0], q_t)
    m0 = jnp.max(s_c, axis=0, keepdims=True)
    m_ref[...] = m0
    g_ref[0] = jnp.ones_like(m0)
    x_ref[...] = jnp.zeros_like(m0)
    acc_ref[...] = pv(vc_ref[0], jnp.exp2(s_c - m0).astype(BF16))

    scores_exp(0)

    def body(i, carry):
        values(i - 1)
        scores_exp(i)
        return carry

    lax.fori_loop(1, n, body, 0, unroll=True)
    values(n - 1)

    @pl.when(jnp.max(x_ref[...]) > LAGGED_MAX_HEADROOM)
    def _():
        m = m_ref[...]
        exact_chunk(kc_ref[0], vc_ref[0], m, True)

        def redo(i, carry):
            exact_chunk(chunk(kl_ref, i), chunk(vl_ref, i), m, False)
            return carry

        lax.fori_loop(0, n, redo, 0)


def _attn_scratch(tk, nq, ones_rows):
    v = pltpu.VMEM
    rows = LANE + (ONES_ROWS if ones_rows else 0)
    return [v((2, tk, nq), BF16), v((2, 1, nq), F32), v((1, nq), F32), v((1, nq), F32), v((rows, nq), F32)]


def _mla_attn_kernel(q_ref, kc_ref, vc_ref, kl_ref, vl_ref, o_ref, p_ref, g_ref, m_ref, x_ref, acc_ref, *, tk):
    _flash_t(q_ref[0].T, kc_ref, vc_ref, kl_ref, vl_ref, p_ref, g_ref, m_ref, x_ref, acc_ref, tk, False)
    acc = acc_ref[...]
    o_ref[0] = (acc / acc[0:1, :]).T.astype(BF16)


def _diff_attn_kernel(lam_ref, gs_ref, q1_ref, q2_ref, kc_ref, vc_ref, kl_ref, vl_ref, o_ref,
                      p_ref, g_ref, m_ref, x_ref, acc_ref, *, tk):
    tq = q1_ref.shape[1]
    q_t = jnp.concatenate([q1_ref[0].T, q2_ref[0].T], axis=1)
    _flash_t(q_t, kc_ref, vc_ref, kl_ref, vl_ref, p_ref, g_ref, m_ref, x_ref, acc_ref, tk, True)
    o = acc_ref[0:LANE, :] / acc_ref[LANE:LANE + 1, :]
    dl = lam_ref[...]
    lam = (jnp.exp(jnp.sum(dl[0:1] * dl[1:2], axis=-1, keepdims=True))
           - jnp.exp(jnp.sum(dl[2:3] * dl[3:4], axis=-1, keepdims=True)) + LAMBDA_INIT)
    w = o[:, :tq] - lam * o[:, tq:]
    w = w * lax.rsqrt(jnp.mean(w * w, axis=0, keepdims=True) + EPS) * gs_ref[...] * (1.0 - LAMBDA_INIT)
    o_ref[0] = w.T.astype(BF16)


def _mla_attn(q, kc, vc, kl, vl, tq, tk):
    b, s, _ = q.shape
    sc = kc.shape[1]
    assert s % tk == 0 and s % tq == 0
    qs = pl.BlockSpec((1, tq, LANE), lambda bi, hi, qi: (bi, qi, hi))
    cs = pl.BlockSpec((1, sc, LANE), lambda bi, hi, qi: (bi, 0, hi))
    ls = pl.BlockSpec((1, s, LANE), lambda bi, hi, qi: (bi, 0, hi))
    return pl.pallas_call(
        functools.partial(_mla_attn_kernel, tk=tk),
        grid=(b, MLA_HEADS, s // tq),
        in_specs=[qs, cs, cs, ls, ls],
        out_specs=qs,
        out_shape=jax.ShapeDtypeStruct((b, s, MLA_W), BF16),
        scratch_shapes=_attn_scratch(tk, tq, False),
        compiler_params=_cparams(("arbitrary", "arbitrary", "arbitrary")),
        name="mla_attn",
    )(q, kc, vc, kl, vl)


def _diff_attn(lam4, g_subln, dq, kc, vc, kl, vl, tq, tk):
    b, s, _ = dq.shape
    sc = kc.shape[1]
    q1 = pl.BlockSpec((1, tq, LANE), lambda bi, hi, qi: (bi, qi, 2 * hi))
    q2 = pl.BlockSpec((1, tq, LANE), lambda bi, hi, qi: (bi, qi, 2 * hi + 1))
    cs = pl.BlockSpec((1, sc, LANE), lambda bi, hi, qi: (bi, 0, hi))
    ls = pl.BlockSpec((1, s, LANE), lambda bi, hi, qi: (bi, 0, hi))
    return pl.pallas_call(
        functools.partial(_diff_attn_kernel, tk=tk),
        grid=(b, DIFF_HEADS, s // tq),
        in_specs=[_full(lam4.shape), _full(g_subln.shape), q1, q2, cs, cs, ls, ls],
        out_specs=pl.BlockSpec((1, tq, LANE), lambda bi, hi, qi: (bi, qi, hi)),
        out_shape=jax.ShapeDtypeStruct((b, s, DIFF_WIDTH), BF16),
        scratch_shapes=_attn_scratch(tk, 2 * tq, True),
        compiler_params=_cparams(("arbitrary", "arbitrary", "arbitrary")),
        name="diff_attn",
    )(lam4, g_subln, dq, dq, kc, vc, kl, vl)


def _mix_kernel(om_ref, od_ref, g_ref, x_ref, mod_ref, wom_ref, wod_ref, wout_ref, lng_ref, lnb_ref,
                wrh_ref, wrl_ref, x1_ref, ha_ref, hb_ref, lg_ref):
    gm = g_ref[0, :, 0:D_MODEL]
    gd = g_ref[0, :, D_MODEL:2 * D_MODEL]
    y = gm * _dot(om_ref[0], wom_ref[...]) + gd * _dot(od_ref[0], wod_ref[...])
    z = _dot(y.astype(BF16), wout_ref[...])
    g1 = mod_ref[0, :, 2 * D_MODEL:3 * D_MODEL]
    sh2 = mod_ref[0, :, 3 * D_MODEL:4 * D_MODEL]
    sc2 = mod_ref[0, :, 4 * D_MODEL:5 * D_MODEL]
    x1 = _layer_norm(DEEPNORM_ALPHA * x_ref[0] + g1 * z, lng_ref[...], lnb_ref[...])
    x1_ref[0] = x1
    h2 = x1 * (1.0 + sc2) + sh2
    hi = h2.astype(BF16)
    lo = (h2 - hi.astype(F32)).astype(BF16)
    bits = pltpu.bitcast(hi.astype(F32), U32)
    q4 = D_MODEL // 4
    ha_ref[0] = bits[:, 0:q4] | (bits[:, q4:2 * q4] >> 16)
    hb_ref[0] = bits[:, 2 * q4:3 * q4] | (bits[:, 3 * q4:4 * q4] >> 16)
    lg_ref[0] = _dot_nt(wrh_ref[...], hi) + (_dot_nt(wrh_ref[...], lo) + _dot_nt(wrl_ref[...], hi))


def _mix(om, od, gates, x, mod3, wom, wod, wout, lng, lnb, wrh, wrl, tm):
    b, s, _ = x.shape
    row = lambda w: pl.BlockSpec((1, tm, w), lambda bi, ti: (bi, ti, 0))
    return pl.pallas_call(
        _mix_kernel,
        grid=(b, s // tm),
        in_specs=[row(MLA_W), row(DIFF_WIDTH), row(2 * D_MODEL), row(D_MODEL),
                  pl.BlockSpec((1, 1, 6 * D_MODEL), lambda bi, ti: (bi, 0, 0)),
                  _full(wom.shape), _full(wod.shape), _full(wout.shape), _full(lng.shape), _full(lnb.shape),
                  _full(wrh.shape), _full(wrl.shape)],
        out_specs=[row(D_MODEL), row(D_MODEL // 4), row(D_MODEL // 4),
                   pl.BlockSpec((1, N_EXPERTS, tm), lambda bi, ti: (bi, 0, ti))],
        out_shape=[jax.ShapeDtypeStruct((b, s, D_MODEL), F32), jax.ShapeDtypeStruct((b, s, D_MODEL // 4), U32),
                   jax.ShapeDtypeStruct((b, s, D_MODEL // 4), U32), jax.ShapeDtypeStruct((b, N_EXPERTS, s), F32)],
        compiler_params=_cparams(("arbitrary", "arbitrary")),
        name="mix",
    )(om, od, gates, x, mod3, wom, wod, wout, lng, lnb, wrh, wrl)


def _route_kernel(lg_ref, br_ref, pos_ref, gate_ref, st_ref, *, cap, tile):
    e, s = lg_ref.shape[1], lg_ref.shape[2]
    lg = lg_ref[0] + br_ref[...]
    ex = jnp.exp(lg - jnp.max(lg, axis=0, keepdims=True))
    aff = ex / jnp.sum(ex, axis=0, keepdims=True)
    gate_ref[0] = aff
    bits = pltpu.bitcast(aff, I32)

    def count(mask):
        return jnp.sum(mask.astype(I32), axis=1, keepdims=True)

    def search(i, thr):
        cand = thr | (jnp.int32(1) << (30 - i))
        return jnp.where(count(bits >= cand) >= cap, cand, thr)

    thr = lax.fori_loop(0, 31, search, jnp.zeros((e, 1), I32))
    gt = bits > thr
    eq = bits == thr
    need = cap - count(gt)

    r = lax.broadcasted_iota(I32, (LANE, LANE), 0)
    c = lax.broadcasted_iota(I32, (LANE, LANE), 1)
    tri = (r < c).astype(BF16)
    lane = lax.broadcasted_iota(I32, (e, LANE), 1)

    def prefix(mask_fn, emit):
        carry = jnp.zeros((e, 1), F32)
        for j in range(s // LANE):
            m = mask_fn(j)
            emit(j, carry + _dot(m.astype(BF16), tri), m, carry)
            carry = carry + jnp.sum(m.astype(F32), axis=1, keepdims=True)

    def eq_blk(j):
        return eq[:, j * LANE:(j + 1) * LANE]

    def emit_sel(j, rank, m, carry):
        sel = gt[:, j * LANE:(j + 1) * LANE] | (m & (rank < need.astype(F32)))
        pos_ref[0, :, j * LANE:(j + 1) * LANE] = sel.astype(I32)

    prefix(eq_blk, emit_sel)

    starts = [jnp.zeros((e, LANE), I32)]

    def sel_blk(j):
        return pos_ref[0, :, j * LANE:(j + 1) * LANE] > 0

    def emit_pos(j, rank, m, carry):
        if (j * LANE) % tile == 0:
            starts[0] = jnp.where(lane == (j * LANE) // tile, carry.astype(I32), starts[0])
        pos_ref[0, :, j * LANE:(j + 1) * LANE] = jnp.where(m, rank.astype(I32), -1)

    prefix(sel_blk, emit_pos)
    st_ref[0] = jnp.where(lane == s // tile, cap, starts[0])


def _route(lgt, b_router, cap, tile):
    b, e, s = lgt.shape
    blk = pl.BlockSpec((1, e, s), lambda bi: (bi, 0, 0))
    return pl.pallas_call(
        functools.partial(_route_kernel, cap=cap, tile=tile),
        grid=(b,),
        in_specs=[blk, _full((e, 1))],
        out_specs=[blk, blk, pl.BlockSpec((1, e, LANE), lambda bi: (bi, 0, 0))],
        out_shape=[jax.ShapeDtypeStruct((b, e, s), I32), jax.ShapeDtypeStruct((b, e, s), F32),
                   jax.ShapeDtypeStruct((b, e, LANE), I32)],
        compiler_params=_cparams(("arbitrary",)),
        name="route",
    )(lgt, b_router.reshape(e, 1))


SLOT_COLS = 8


def _slots_kernel(st_ref, pos_ref, gate_ref, o_ref, acc_ref, *, chunk, tile):
    bi, ei = pl.program_id(0), pl.program_id(1)
    cap, s = acc_ref.shape[0], pos_ref.shape[3]
    acc_ref[...] = jnp.zeros_like(acc_ref)
    slot = lax.broadcasted_iota(I32, (chunk, tile), 0)
    row = lax.broadcasted_iota(I32, (SLOT_COLS, tile), 0)
    for u in range(s // tile):
        b0, b1 = st_ref[bi, ei, u], st_ref[bi, ei, u + 1]
        cols = slice(u * tile, (u + 1) * tile)

        @pl.when(b1 > b0)
        def _():
            t = lax.broadcasted_iota(I32, (SLOT_COLS, tile), 1) + u * tile
            g = jnp.broadcast_to(gate_ref[0, 0, :, cols], (SLOT_COLS, tile))
            g1 = g.astype(BF16).astype(F32)
            g2 = (g - g1).astype(BF16).astype(F32)
            g3 = g - g1 - g2
            rec = jnp.where(row == 0, (t >> 7).astype(F32),
                            jnp.where(row == 1, (t & 127).astype(F32),
                                      jnp.where(row == 2, g1, jnp.where(row == 3, g2,
                                                                        jnp.where(row == 4, g3, 0.0)))))
            rec = rec.astype(BF16)
            pos = pos_ref[0, 0, :, cols]
            for c in range(cap // chunk):
                lo, hi = c * chunk, (c + 1) * chunk

                @pl.when((b0 < hi) & (b1 > lo))
                def _():
                    hit = (slot == (pos - lo)).astype(BF16)
                    acc_ref[lo:hi, :] += _dot_nt(hit, rec)

    o_ref[0, 0] = acc_ref[...]


def _slots(starts, pos4, gate4, cap, tile, chunk):
    b, e, _, s = pos4.shape
    blk = pl.BlockSpec((1, 1, 1, s), lambda bi, ei, st: (bi, ei, 0, 0))
    gs = pltpu.PrefetchScalarGridSpec(
        num_scalar_prefetch=1,
        grid=(b, e),
        in_specs=[blk, blk],
        out_specs=pl.BlockSpec((1, 1, cap, SLOT_COLS), lambda bi, ei, st: (bi, ei, 0, 0)),
        scratch_shapes=[pltpu.VMEM((cap, SLOT_COLS), F32)],
    )
    return pl.pallas_call(
        functools.partial(_slots_kernel, chunk=chunk, tile=tile),
        grid_spec=gs,
        out_shape=jax.ShapeDtypeStruct((b, e, cap, SLOT_COLS), F32),
        compiler_params=_cparams(("arbitrary", "arbitrary")),
        name="slots",
    )(starts, pos4, gate4)


SC_GATHER_WINDOW = 128


def _sc_gather(table, idx):
    n = idx.shape[0]
    w = table.shape[1]
    mesh = plsc.VectorSubcoreMesh(core_axis_name="core", subcore_axis_name="subcore")

    @pl.kernel(out_type=jax.ShapeDtypeStruct((n, w), table.dtype), mesh=mesh)
    def gather(t_hbm, i_hbm, o_hbm):
        def body(i_vmem, o_vmem):
            pltpu.sync_copy(t_hbm.at[i_vmem.at[0]], o_vmem)

        pltpu.emit_pipeline(
            body,
            grid=(n // SC_GATHER_WINDOW,),
            in_specs=[pl.BlockSpec((1, SC_GATHER_WINDOW), index_map=lambda i: (0, i))],
            out_specs=[pl.BlockSpec((SC_GATHER_WINDOW, w), index_map=lambda i: (i, 0))],
            core_axis_name=("core", "subcore"),
            dimension_semantics=(pltpu.PARALLEL,),
        )(i_hbm, o_hbm)

    return gather(table, idx.reshape(1, n))


def _ffn_kernel(xa_ref, xb_ref, sl_ref, w1_ref, w3_ref, w2_ref, ye_ref):
    def halves(words):
        return pltpu.bitcast(words & jnp.uint32(0xFFFF0000), F32), pltpu.bitcast(words << 16, F32)

    a_hi, a_lo = halves(xa_ref[...])
    b_hi, b_lo = halves(xb_ref[...])
    xe = jnp.concatenate([a_hi, a_lo, b_hi, b_lo], axis=1).astype(BF16)
    sl = sl_ref[0, 0]
    gate = sl[:, 2:3] + sl[:, 3:4] + sl[:, 4:5]
    a = _dot(xe, w1_ref[0])
    hid = ((a / (1.0 + jnp.exp(-a))) * _dot(xe, w3_ref[0])).astype(BF16)
    ye_ref[0, 0] = (_dot(hid, w2_ref[0]) * gate).astype(BF16)


def _ffn(xa, xb, slots, w1, w3, w2):
    b, e, cap, _ = slots.shape
    d = w2.shape[2]
    xspec = pl.BlockSpec((cap, xa.shape[1]), lambda bi, ei: (bi * e + ei, 0))
    wspec = lambda shp: pl.BlockSpec((1,) + shp, lambda bi, ei: (ei, 0, 0))
    return pl.pallas_call(
        _ffn_kernel,
        grid=(b, e),
        in_specs=[xspec, xspec, pl.BlockSpec((1, 1, cap, SLOT_COLS), lambda bi, ei: (bi, ei, 0, 0)),
                  wspec(w1.shape[1:]), wspec(w3.shape[1:]), wspec(w2.shape[1:])],
        out_specs=pl.BlockSpec((1, 1, cap, d), lambda bi, ei: (bi, ei, 0, 0)),
        out_shape=jax.ShapeDtypeStruct((b, e, cap, d), BF16),
        compiler_params=_cparams(("arbitrary", "arbitrary")),
        name="ffn",
    )(xa, xb, slots, w1, w3, w2)


COMBINE_EXPERTS = 4


def _combine_kernel(st_ref, ye_ref, pos_ref, x1_ref, mod_ref, lng_ref, lnb_ref, o_ref, acc_ref, *, chunk):
    bi, ti, ei = pl.program_id(0), pl.program_id(1), pl.program_id(2)
    group, cap, tile = ye_ref.shape[1], ye_ref.shape[2], x1_ref.shape[1]

    @pl.when(ei == 0)
    def _():
        acc_ref[...] = jnp.zeros_like(acc_ref)

    slot = lax.broadcasted_iota(I32, (chunk, tile), 0)
    for g in range(group):
        expert = ei * group + g
        b0, b1 = st_ref[bi, expert, ti], st_ref[bi, expert, ti + 1]
        for c in range(cap // chunk):
            lo, hi = c * chunk, (c + 1) * chunk

            @pl.when((b0 < hi) & (b1 > lo))
            def _():
                hit = (slot == (pos_ref[0, g] - lo)).astype(BF16)
                acc_ref[...] += _dot_tn(hit, ye_ref[0, g, lo:hi, :])

    @pl.when(ei == pl.num_programs(2) - 1)
    def _():
        g2 = mod_ref[0, :, 5 * D_MODEL:6 * D_MODEL]
        o_ref[0] = _layer_norm(DEEPNORM_ALPHA * x1_ref[0] + g2 * acc_ref[...], lng_ref[...], lnb_ref[...])


def _combine(starts, ye, pos4, x1, mod3, lng, lnb, tile, chunk):
    b, s, d = x1.shape
    e, cap = ye.shape[1], ye.shape[2]
    g = COMBINE_EXPERTS
    gs = pltpu.PrefetchScalarGridSpec(
        num_scalar_prefetch=1,
        grid=(b, s // tile, e // g),
        in_specs=[pl.BlockSpec((1, g, cap, d), lambda bi, ti, ei, st: (bi, ei, 0, 0)),
                  pl.BlockSpec((1, g, 1, tile), lambda bi, ti, ei, st: (bi, ei, 0, ti)),
                  pl.BlockSpec((1, tile, d), lambda bi, ti, ei, st: (bi, ti, 0)),
                  pl.BlockSpec((1, 1, 6 * D_MODEL), lambda bi, ti, ei, st: (bi, 0, 0)),
                  pl.BlockSpec((1, d), lambda bi, ti, ei, st: (0, 0)),
                  pl.BlockSpec((1, d), lambda bi, ti, ei, st: (0, 0))],
        out_specs=pl.BlockSpec((1, tile, d), lambda bi, ti, ei, st: (bi, ti, 0)),
        scratch_shapes=[pltpu.VMEM((tile, d), F32)],
    )
    return pl.pallas_call(
        functools.partial(_combine_kernel, chunk=chunk),
        grid_spec=gs,
        out_shape=jax.ShapeDtypeStruct((b, s, d), F32),
        compiler_params=_cparams(("arbitrary", "arbitrary", "arbitrary")),
        name="combine",
    )(starts, ye, pos4, x1, mod3, lng, lnb)


def _head_pad(w, n_heads, width, offset=0):
    k = w.shape[0]
    w3 = w.reshape(k, n_heads, width)
    out = jnp.zeros((k, n_heads, LANE), w.dtype).at[:, :, offset:offset + width].set(w3)
    return out.reshape(k, n_heads * LANE)


def _prep_weights(w_in, w_uq, w_o_mla):
    kr = w_in[:, MLA_KR_OFF:DIFF_Q_OFF]
    dqw = w_in[:, DIFF_Q_OFF:DIFF_K_OFF]
    dkw = w_in[:, DIFF_K_OFF:DIFF_V_OFF]
    dvw = w_in[:, DIFF_V_OFF:GATE_OFF]
    kr_pad = _head_pad(kr, 1, MLA_ROPE, MLA_NOPE)
    wx = jnp.concatenate([w_in[:, MLA_Q_OFF:MLA_KR_OFF], kr_pad, dqw, dkw, dvw, w_in[:, GATE_OFF:]],
                         axis=1).astype(BF16)
    wc = jnp.concatenate([w_in[:, MLA_KV_OFF:MLA_KR_OFF], kr_pad, dkw, dvw], axis=1).astype(BF16)
    wuq = _head_pad(w_uq, MLA_HEADS, MLA_NOPE + MLA_ROPE).astype(BF16)
    wom = jnp.zeros((MLA_HEADS, LANE, D_MODEL), w_o_mla.dtype).at[:, MLA_NOPE:, :].set(
        w_o_mla.reshape(MLA_HEADS, MLA_V, D_MODEL)).reshape(MLA_W, D_MODEL).astype(BF16)
    return wx, wc, wuq, wom


def kernel(x, c, ctx, c_ctx, w_ada, b_ada, w_in, mla_g_q, mla_w_uq, mla_g_kv, mla_w_ukv, mla_w_o, diff_lambda,
           diff_g_subln, diff_w_o, w_out, ln1_g, ln1_b, moe_w_router, moe_b_router, moe_w1, moe_w3, moe_w2,
           ln2_g, ln2_b):
    b, s, d = x.shape
    assert d == D_MODEL and w_ada.shape[0] == DEPTH == 1 and b < 8
    cap = EC_CAPACITY * s // N_EXPERTS
    tile = min(1024, s)
    gather_chunk = min(128, cap)
    scatter_chunk = min(256, cap)
    tm = min(512, s)
    tq = min(2048, s)
    tk = min(2048, s)

    cc = jnp.zeros((8, d), F32).at[:b].set(c).at[b].set(c_ctx)
    mod3 = _ada(cc, w_ada[0], b_ada[0]).reshape(8, 1, 6 * d)

    wx, wc, wuq, wom = _prep_weights(w_in[0], mla_w_uq[0], mla_w_o[0])
    gq = mla_g_q[0].reshape(1, -1)
    gkv = mla_g_kv[0].reshape(1, -1)
    wukv = mla_w_ukv[0].astype(BF16)
    tabs = _rope_tables(s)

    q, k, v, dq, dk, dv, gates = _proj_lat(x, mod3, wx, gq, wuq, gkv, wukv, tabs, tm)
    kc, vc, dkc, dvc = _proj_ctx(ctx, mod3, wc, gkv, wukv)

    o_mla = _mla_attn(q, kc, vc, k, v, tq, tk)
    o_diff = _diff_attn(diff_lambda[0], diff_g_subln[0].reshape(-1, 1), dq, dkc, dvc, dk, dv, tq // 2, tk)

    wr = moe_w_router[0].T
    wrh = wr.astype(BF16)
    wrl = (wr - wrh.astype(F32)).astype(BF16)
    x1, ha, hb, lgt = _mix(o_mla, o_diff, gates, x, mod3, wom, diff_w_o[0].astype(BF16), w_out[0].astype(BF16),
                           ln1_g[0].reshape(1, -1), ln1_b[0].reshape(1, -1), wrh, wrl, tm)

    pos, gate, starts = _route(lgt, moe_b_router[0], cap, tile)
    pos4 = pos.reshape(b, N_EXPERTS, 1, s)
    gate4 = gate.reshape(b, N_EXPERTS, 1, s)
    slots = _slots(starts, pos4, gate4, cap, tile, gather_chunk)
    row0 = (jnp.arange(b, dtype=I32) * s)[:, None, None]
    idx = (slots[..., 0].astype(I32) * 128 + slots[..., 1].astype(I32) + row0).reshape(-1)
    xa = _sc_gather(ha.reshape(b * s, -1), idx)
    xb = _sc_gather(hb.reshape(b * s, -1), idx)
    ye = _ffn(xa, xb, slots, moe_w1[0].astype(BF16), moe_w3[0].astype(BF16), moe_w2[0].astype(BF16))
    return _combine(starts, ye, pos4, x1, mod3, ln2_g[0].reshape(1, -1), ln2_b[0].reshape(1, -1), tile, scatter_chunk)
```
